```python
import math
import jax, jax.numpy as jnp
from jax import lax
import numpy as np

D_MODEL = 1024
BATCH = 1
SEQ = 16384
DEPTH = 1
DEC_BATCH = 32
DEC_SEQ = 1
PAST_LEN = 16384
PAGE_SIZE = 128

R_HEADS = 8
R_HEAD_DIM = 64
R_WIDTH = R_HEADS * R_HEAD_DIM
DECAY_LORA = 64
AAA_LORA = 64
GATE_LORA = 128
R_PROJ = 3 * R_WIDTH + DECAY_LORA + AAA_LORA + GATE_LORA
R_SPLITS = (R_WIDTH, 2 * R_WIDTH, 3 * R_WIDTH, 3 * R_WIDTH + DECAY_LORA,
            3 * R_WIDTH + DECAY_LORA + AAA_LORA)
GN_EPS = 64e-5
M_HEADS = 8
M_HEAD_DIM = 64
M_WIDTH = M_HEADS * M_HEAD_DIM
MOBA_BLOCK = 256
MOBA_TOPK = 3
QUERY_BLOCK = 128
ROT_DIM = M_HEAD_DIM // 4
ROPE_THETA = 500000.0
N_BRANCH = 2
IN_PROJ = R_PROJ + 3 * M_WIDTH + N_BRANCH * D_MODEL
IN_SPLITS = (R_PROJ, R_PROJ + M_WIDTH, R_PROJ + 2 * M_WIDTH, R_PROJ + 3 * M_WIDTH,
             R_PROJ + 3 * M_WIDTH + D_MODEL)
D_FF = ((8 * D_MODEL // 3 + 255) // 256) * 256
N_SUBLAYERS = 3
LN_EPS = 1e-5
DN_ALPHA = (2 * DEPTH) ** 0.25
DN_BETA = (8 * DEPTH) ** -0.25

kernel_name = 'rwkv7_moba_macaron_deepnorm_adaln_step'


def layer_norm(x, g, b):
    xf = x.astype(jnp.float32)
    mu = xf.mean(-1, keepdims=True)
    var = jnp.square(xf - mu).mean(-1, keepdims=True)
    return ((xf - mu) * lax.rsqrt(var + LN_EPS) * g + b).astype(x.dtype)


def swiglu(h, w_gate, w_up, w_down):
    return (jax.nn.silu(h @ w_gate) * (h @ w_up)) @ w_down


def rotary(x, pos):
    half = ROT_DIM // 2
    inv = ROPE_THETA ** (-jnp.arange(half, dtype=jnp.float32) * 2.0 / ROT_DIM)
    ang = pos.astype(jnp.float32)[:, None] * inv[None, :]
    cos = jnp.cos(ang)[None, :, None, :]
    sin = jnp.sin(ang)[None, :, None, :]
    xr = x[..., :ROT_DIM].astype(jnp.float32)
    x1, x2 = xr[..., :half], xr[..., half:]
    rot = jnp.concatenate([x1 * cos - x2 * sin, x2 * cos + x1 * sin], -1).astype(x.dtype)
    return jnp.concatenate([rot, x[..., ROT_DIM:]], -1)


def rwkv_branch(pr, shift_row, wkv0, mu_shift, w0, w2, a0, a2, g2, k_k, k_a, r_k, lnx_g, lnx_b):
    n, t, _ = pr.shape
    f32 = jnp.float32
    prev = jnp.concatenate([shift_row[:, None, :].astype(pr.dtype), pr[:, :-1]], axis=1)
    xs = pr + (prev - pr) * mu_shift
    r, k, v, wl, al, gl = jnp.split(xs, R_SPLITS, axis=-1)
    w = (w0 + jnp.tanh(wl) @ w2).astype(f32)
    decay = jnp.exp(-jnp.exp(-jax.nn.softplus(-w) - 0.5))
    a = jax.nn.sigmoid((a0 + al @ a2).astype(f32))
    g = jax.nn.sigmoid(gl) @ g2
    hs = (n, t, R_HEADS, R_HEAD_DIM)
    kk = (k * k_k).astype(f32).reshape(hs)
    kk = kk / jnp.maximum(jnp.sqrt(jnp.sum(kk * kk, -1, keepdims=True)), 1e-12)
    k_eff = (k.astype(f32) * (1.0 + (a - 1.0) * k_a.astype(f32))).reshape(hs)
    a_h = a.reshape(hs)
    r_h = r.astype(f32).reshape(hs)
    v_h = v.astype(f32).reshape(hs)
    w_h = decay.reshape(hs)

    def tm(z):
        return jnp.swapaxes(z, 0, 1)

    def step(S, inp):
        r_t, w_t, k_t, v_t, kk_t, a_t = inp
        S = (S * w_t[:, :, None, :]
             - jnp.einsum('nhvk,nhk->nhv', S, kk_t)[..., None] * (kk_t * a_t)[:, :, None, :]
             + v_t[..., None] * k_t[:, :, None, :])
        return S, jnp.einsum('nhvk,nhk->nhv', S, r_t)

    S, y = lax.scan(step, wkv0.astype(f32),
                    (tm(r_h), tm(w_h), tm(k_eff), tm(v_h), tm(kk), tm(a_h)))
    y = tm(y)
    ym = y.mean(-1, keepdims=True)
    yv = jnp.square(y - ym).mean(-1, keepdims=True)
    yn = ((y - ym) * lax.rsqrt(yv + GN_EPS)).reshape(n, t, R_WIDTH) * lnx_g + lnx_b
    bonus = (jnp.sum(r_h * k_eff * r_k, -1, keepdims=True) * v_h).reshape(n, t, R_WIDTH)
    out = ((yn + bonus) * g.astype(f32)).astype(pr.dtype)
    return out, S, pr[:, -1]


def moba_blocks(k_parts, v_parts):
    n = k_parts[0].shape[0]
    length = sum(p.shape[1] for p in k_parts)
    nb = -(-length // MOBA_BLOCK)
    pad = nb * MOBA_BLOCK - length
    zeros = jnp.zeros((n, pad, M_HEADS, M_HEAD_DIM), k_parts[0].dtype)
    kb = jnp.concatenate(list(k_parts) + [zeros], 1).reshape(n, nb, MOBA_BLOCK, M_HEADS, M_HEAD_DIM)
    vb = jnp.concatenate(list(v_parts) + [zeros.astype(v_parts[0].dtype)], 1).reshape(
        n, nb, MOBA_BLOCK, M_HEADS, M_HEAD_DIM)
    kmean = kb.astype(jnp.float32).mean(axis=2)
    return kb, vb, kmean


def moba_query_block(q, qpos, kb, vb, kmean):
    n, nq = q.shape[:2]
    nb = kb.shape[1]
    topk = min(MOBA_TOPK, nb)
    cur = qpos // MOBA_BLOCK
    gate = jnp.einsum('nqhd,nbhd->nqhb', q.astype(jnp.float32), kmean)
    fully_past = jnp.arange(nb)[None, :] < cur[:, None]
    gate = jnp.where(fully_past[None, :, None, :], gate, -jnp.inf)
    _, sel = lax.top_k(gate, topk)
    own = jnp.broadcast_to(cur[None, :, None, None], (n, nq, M_HEADS, 1)).astype(sel.dtype)
    idx = jnp.concatenate([sel, own], -1)
    ni = jnp.arange(n)[:, None, None, None]
    hi = jnp.arange(M_HEADS)[None, None, :, None]
    kg = kb[ni, idx, :, hi, :]
    vg = vb[ni, idx, :, hi, :]
    s = jnp.einsum('nqhd,nqhsjd->nqhsj', q, kg,
                   preferred_element_type=jnp.float32) * (M_HEAD_DIM ** -0.5)
    kpos = idx[..., None] * MOBA_BLOCK + jnp.arange(MOBA_BLOCK)
    is_own = (jnp.arange(topk + 1) == topk)[:, None]
    valid = jnp.where(is_own,
                      kpos <= qpos[None, :, None, None, None],
                      idx[..., None] < cur[None, :, None, None, None])
    s = jnp.where(valid, s, -jnp.inf)
    p = jax.nn.softmax(s.reshape(n, nq, M_HEADS, -1), axis=-1).reshape(s.shape)
    return jnp.einsum('nqhsj,nqhsjd->nqhd', p.astype(vg.dtype), vg)


def moba_attention(q, qpos, kb, vb, kmean):
    n, t = q.shape[:2]
    if t > QUERY_BLOCK and t % QUERY_BLOCK == 0:
        nc = t // QUERY_BLOCK
        qc = jnp.swapaxes(q.reshape(n, nc, QUERY_BLOCK, M_HEADS, M_HEAD_DIM), 0, 1)
        pc = qpos.reshape(nc, QUERY_BLOCK)
        oc = lax.map(lambda qp: moba_query_block(qp[0], qp[1], kb, vb, kmean), (qc, pc))
        return jnp.swapaxes(oc, 0, 1).reshape(n, t, M_HEADS, M_HEAD_DIM)
    return moba_query_block(q, qpos, kb, vb, kmean)


def decoder_layer(x, c, pos, past_k, past_v, wkv0, shift0,
                  w_ada, b_ada, ffn1_gate, ffn1_up, ffn1_down, ln1_g, ln1_b,
                  w_in, mu_shift, rwkv_w0, rwkv_w2, rwkv_a0, rwkv_a2, rwkv_g2,
                  rwkv_k_k, rwkv_k_a, rwkv_r_k, rwkv_lnx_g, rwkv_lnx_b,
                  w_br_rwkv, w_br_moba, w_out, ln2_g, ln2_b,
                  ffn2_gate, ffn2_up, ffn2_down, ln3_g, ln3_b):
    n, t, _ = x.shape
    mod = (jax.nn.silu(c) @ w_ada + b_ada).reshape(n, N_SUBLAYERS, 3, 1, D_MODEL)
    h = x * (1.0 + mod[:, 0, 1]) + mod[:, 0, 0]
    x = layer_norm(DN_ALPHA * x + 0.5 * mod[:, 0, 2] * swiglu(h, ffn1_gate, ffn1_up, ffn1_down), ln1_g, ln1_b)
    h = x * (1.0 + mod[:, 1, 1]) + mod[:, 1, 0]
    proj = h @ w_in
    pr, pq, pk, pv, pg_r, pg_m = jnp.split(proj, IN_SPLITS, axis=-1)
    r_out, wkv_new, shift_new = rwkv_branch(pr, shift0, wkv0, mu_shift, rwkv_w0, rwkv_w2, rwkv_a0,
                                            rwkv_a2, rwkv_g2, rwkv_k_k, rwkv_k_a, rwkv_r_k,
                                            rwkv_lnx_g, rwkv_lnx_b)
    hs = (n, t, M_HEADS, M_HEAD_DIM)
    q = rotary(pq.reshape(hs), pos)
    k_new = rotary(pk.reshape(hs), pos)
    v_new = pv.reshape(hs)
    if past_k is None:
        kb, vb, kmean = moba_blocks([k_new], [v_new])
    else:
        kb, vb, kmean = moba_blocks([past_k.astype(k_new.dtype), k_new], [past_v.astype(v_new.dtype), v_new])
    att = moba_attention(q, pos, kb, vb, kmean).reshape(n, t, M_WIDTH)
    g_r = jax.nn.sigmoid(pg_r.astype(jnp.float32)).astype(x.dtype)
    g_m = jax.nn.sigmoid(pg_m.astype(jnp.float32)).astype(x.dtype)
    merged = (g_r * (r_out @ w_br_rwkv) + g_m * (att @ w_br_moba)) @ w_out
    x = layer_norm(DN_ALPHA * x + mod[:, 1, 2] * merged, ln2_g, ln2_b)
    h = x * (1.0 + mod[:, 2, 1]) + mod[:, 2, 0]
    x = layer_norm(DN_ALPHA * x + 0.5 * mod[:, 2, 2] * swiglu(h, ffn2_gate, ffn2_up, ffn2_down), ln3_g, ln3_b)
    return x, k_new, v_new, wkv_new, shift_new


def setup_inputs(seed: int = 0) -> dict:
    key = jax.random.key(seed)
    keys = iter(jax.random.split(key, 48))

    def nrm(shape, scale):
        return jax.random.normal(next(keys), shape, jnp.float32) * scale

    L = DEPTH
    n_pages = PAST_LEN // PAGE_SIZE
    n_phys = (DEC_BATCH * n_pages * 5) // 4
    ds = D_MODEL ** -0.5
    x_prompt = nrm((BATCH, SEQ, D_MODEL), 1.0)
    x_sample = nrm((DEC_BATCH, DEC_SEQ, D_MODEL), 1.0)
    cache_k = nrm((L, n_phys, PAGE_SIZE, M_HEADS, M_HEAD_DIM), 1.0)
    cache_v = nrm((L, n_phys, PAGE_SIZE, M_HEADS, M_HEAD_DIM), 1.0)
    state_rwkv_wkv = nrm((L, DEC_BATCH, R_HEADS, R_HEAD_DIM, R_HEAD_DIM), 0.5)
    state_rwkv_shift = nrm((L, DEC_BATCH, R_PROJ), 1.0)
    page_table = jax.random.permutation(next(keys), n_phys)[: DEC_BATCH * n_pages].reshape(
        DEC_BATCH, n_pages).astype(jnp.int32)
    c_prompt = nrm((BATCH, D_MODEL), 1.0)
    c_sample = nrm((DEC_BATCH, D_MODEL), 1.0)
    return {
        'x_prompt': x_prompt, 'x_sample': x_sample,
        'cache_k': cache_k, 'cache_v': cache_v,
        'state_rwkv_wkv': state_rwkv_wkv, 'state_rwkv_shift': state_rwkv_shift,
        'page_table': page_table, 'c_prompt': c_prompt, 'c_sample': c_sample,
        'w_ada': nrm((L, D_MODEL, N_SUBLAYERS * 3 * D_MODEL), 0.5 * ds),
        'b_ada': nrm((L, N_SUBLAYERS * 3 * D_MODEL), 0.01),
        'ffn1_gate': nrm((L, D_MODEL, D_FF), ds),
        'ffn1_up': nrm((L, D_MODEL, D_FF), ds),
        'ffn1_down': nrm((L, D_FF, D_MODEL), DN_BETA * D_FF ** -0.5),
        'ln1_g': 1.0 + nrm((L, D_MODEL), 0.02), 'ln1_b': nrm((L, D_MODEL), 0.02),
        'w_in': nrm((L, D_MODEL, IN_PROJ), ds),
        'mu_shift': jax.random.uniform(next(keys), (L, R_PROJ), jnp.float32),
        'rwkv_w0': jax.random.uniform(next(keys), (L, R_WIDTH), jnp.float32, -2.0, 1.0),
        'rwkv_w2': nrm((L, DECAY_LORA, R_WIDTH), 0.1),
        'rwkv_a0': nrm((L, R_WIDTH), 0.1),
        'rwkv_a2': nrm((L, AAA_LORA, R_WIDTH), 0.1),
        'rwkv_g2': nrm((L, GATE_LORA, R_WIDTH), GATE_LORA ** -0.5),
        'rwkv_k_k': 0.85 + nrm((L, R_WIDTH), 0.02),
        'rwkv_k_a': 1.0 + nrm((L, R_WIDTH), 0.02),
        'rwkv_r_k': nrm((L, R_HEADS, R_HEAD_DIM), 0.1),
        'rwkv_lnx_g': 1.0 + nrm((L, R_WIDTH), 0.02), 'rwkv_lnx_b': nrm((L, R_WIDTH), 0.02),
        'w_br_rwkv': nrm((L, R_WIDTH, D_MODEL), R_WIDTH ** -0.5),
        'w_br_moba': nrm((L, M_WIDTH, D_MODEL), M_WIDTH ** -0.5),
        'w_out': nrm((L, D_MODEL, D_MODEL), DN_BETA * ds),
        'ln2_g': 1.0 + nrm((L, D_MODEL), 0.02), 'ln2_b': nrm((L, D_MODEL), 0.02),
        'ffn2_gate': nrm((L, D_MODEL, D_FF), ds),
        'ffn2_up': nrm((L, D_MODEL, D_FF), ds),
        'ffn2_down': nrm((L, D_FF, D_MODEL), DN_BETA * D_FF ** -0.5),
        'ln3_g': 1.0 + nrm((L, D_MODEL), 0.02), 'ln3_b': nrm((L, D_MODEL), 0.02),
    }


def reference(x_prompt, x_sample, cache_k, cache_v, state_rwkv_wkv, state_rwkv_shift, page_table,
              c_prompt, c_sample, w_ada, b_ada, ffn1_gate, ffn1_up, ffn1_down, ln1_g, ln1_b,
              w_in, mu_shift, rwkv_w0, rwkv_w2, rwkv_a0, rwkv_a2, rwkv_g2, rwkv_k_k, rwkv_k_a,
              rwkv_r_k, rwkv_lnx_g, rwkv_lnx_b, w_br_rwkv, w_br_moba, w_out, ln2_g, ln2_b,
              ffn2_gate, ffn2_up, ffn2_down, ln3_g, ln3_b):
    bp, tp = x_prompt.shape[:2]
    bs, ts = x_sample.shape[:2]
    n_pages = page_table.shape[1]
    past_len = n_pages * PAGE_SIZE
    pos_p = jnp.arange(tp, dtype=jnp.int32)
    pos_s = past_len + jnp.arange(ts, dtype=jnp.int32)
    y_p, y_s = x_prompt, x_sample
    kp_l, vp_l, ks_l, vs_l, wp_l, ws_l, sp_l, ss_l = [], [], [], [], [], [], [], []
    for l in range(DEPTH):
        lw = (w_ada[l], b_ada[l], ffn1_gate[l], ffn1_up[l], ffn1_down[l], ln1_g[l], ln1_b[l],
              w_in[l], mu_shift[l], rwkv_w0[l], rwkv_w2[l], rwkv_a0[l], rwkv_a2[l], rwkv_g2[l],
              rwkv_k_k[l], rwkv_k_a[l], rwkv_r_k[l], rwkv_lnx_g[l], rwkv_lnx_b[l],
              w_br_rwkv[l], w_br_moba[l], w_out[l], ln2_g[l], ln2_b[l],
              ffn2_gate[l], ffn2_up[l], ffn2_down[l], ln3_g[l], ln3_b[l])
        wkv0_p = jnp.zeros((bp, R_HEADS, R_HEAD_DIM, R_HEAD_DIM), jnp.float32)
        shift0_p = jnp.zeros((bp, R_PROJ), x_prompt.dtype)
        y_p, kp, vp, wp, sp = decoder_layer(y_p, c_prompt, pos_p, None, None, wkv0_p, shift0_p, *lw)
        past_k = cache_k[l][page_table].reshape(bs, past_len, M_HEADS, M_HEAD_DIM)
        past_v = cache_v[l][page_table].reshape(bs, past_len, M_HEADS, M_HEAD_DIM)
        y_s, ks, vs, ws, ss = decoder_layer(y_s, c_sample, pos_s, past_k, past_v,
                                            state_rwkv_wkv[l], state_rwkv_shift[l], *lw)
        kp_l.append(kp); vp_l.append(vp); ks_l.append(ks); vs_l.append(vs)
        wp_l.append(wp); ws_l.append(ws); sp_l.append(sp); ss_l.append(ss)
    return (y_p, y_s, jnp.stack(kp_l), jnp.stack(vp_l), jnp.stack(ks_l), jnp.stack(vs_l),
            jnp.stack(wp_l), jnp.stack(ws_l), jnp.stack(sp_l), jnp.stack(ss_l))
```

```python
import functools

import jax
import jax.numpy as jnp
from jax import lax
from jax.experimental import pallas as pl
from jax.experimental.pallas import tpu as pltpu

F32 = jnp.float32
MXU_DT = jnp.bfloat16

D_MODEL = 1024
PAGE_SIZE = 128
N_HEADS = 8
HEAD_DIM = 64
WIDTH = N_HEADS * HEAD_DIM
LORA_W = 64
LORA_A = 64
LORA_G = 128
R_PROJ = 3 * WIDTH + LORA_W + LORA_A + LORA_G
GN_EPS = 64e-5
MOBA_BLOCK = 256
MOBA_TOPK = 3
ROT_DIM = HEAD_DIM // 4
ROPE_THETA = 500000.0
IN_PROJ = R_PROJ + 3 * WIDTH + 2 * D_MODEL
D_FF = 2816
LN_EPS = 1e-5
DN_ALPHA = 2.0 ** 0.25
CHUNK = 64
LANES = 128
VMEM_LIMIT = 56 * 1024 * 1024
SAMPLE_PAGES_PER_STEP = 16
VT_ROWS = LANES + 16

NEG_INF = float("-inf")


def _cparams(*sem, vmem=VMEM_LIMIT):
    return pltpu.CompilerParams(dimension_semantics=sem, vmem_limit_bytes=vmem)


def _const_spec(shape):
    return pl.BlockSpec(shape, lambda *_: (0,) * len(shape), pipeline_mode=pl.Buffered(1))


def _row_spec(tm, width):
    return pl.BlockSpec((tm, width), lambda i: (i, 0))


def _mod_spec(arr, tm):
    if arr.shape[0] == 1:
        return pl.BlockSpec((1, arr.shape[1]), lambda i: (0, 0))
    return pl.BlockSpec((tm, arr.shape[1]), lambda i: (i, 0))


def _dot(a, b):
    return jnp.dot(a.astype(MXU_DT), b.astype(MXU_DT), preferred_element_type=F32)


_NN = (((1,), (0,)), ((), ()))
_NT = (((1,), (1,)), ((), ()))
_TN = (((0,), (0,)), ((), ()))


def _dg(a, b, dn):
    return lax.dot_general(a.astype(MXU_DT), b.astype(MXU_DT), dn, preferred_element_type=F32)


def _split(x):
    hi = x.astype(MXU_DT)
    lo = (x - hi.astype(F32)).astype(MXU_DT)
    return hi, lo


def _mm3(a, b, dn=_NN):
    f = lambda x, y: lax.dot_general(x, y, dn, preferred_element_type=F32)
    return f(a[0], b[0]) + (f(a[0], b[1]) + f(a[1], b[0]))


def _seg_sum(x, ones):
    hi, lo = _split(x)
    return (jnp.dot(hi, ones, preferred_element_type=F32)
            + jnp.dot(lo, ones, preferred_element_type=F32))


def _layer_norm(y, g, b):
    mu = jnp.mean(y, axis=-1, keepdims=True)
    yc = y - mu
    var = jnp.mean(yc * yc, axis=-1, keepdims=True)
    return yc * lax.rsqrt(var + LN_EPS) * g + b


def _ada_kernel(c_ref, w_ref, b_ref, o_ref):
    c = c_ref[...]
    o_ref[...] = _dot(c * jax.nn.sigmoid(c), w_ref[...]) + b_ref[...]


def _ada(c, w_ada, b_ada):
    n, d = c.shape
    nout = w_ada.shape[1]
    tn = 1024
    return pl.pallas_call(
        _ada_kernel,
        grid=(nout // tn,),
        in_specs=[pl.BlockSpec((n, d), lambda j: (0, 0)),
                  pl.BlockSpec((d, tn), lambda j: (0, j)),
                  pl.BlockSpec((1, tn), lambda j: (0, j))],
        out_specs=pl.BlockSpec((n, tn), lambda j: (0, j)),
        out_shape=jax.ShapeDtypeStruct((n, nout), F32),
        compiler_params=_cparams("parallel"),
    )(c, w_ada, b_ada)


def _ffn_kernel(x_ref, sh_ref, sc_ref, gt_ref, wg_ref, wu_ref, wd_ref, lg_ref, lb_ref, o_ref, *, n_ff):
    x = x_ref[...]
    h = (x * (1.0 + sc_ref[...]) + sh_ref[...]).astype(MXU_DT)
    step = D_FF // n_ff
    f = None
    for j in range(n_ff):
        cs = slice(j * step, (j + 1) * step)
        g = jnp.dot(h, wg_ref[:, cs], preferred_element_type=F32)
        u = jnp.dot(h, wu_ref[:, cs], preferred_element_type=F32)
        a = (g * jax.nn.sigmoid(g) * u).astype(MXU_DT)
        part = jnp.dot(a, wd_ref[cs, :], preferred_element_type=F32)
        f = part if f is None else f + part
    y = DN_ALPHA * x + (0.5 * gt_ref[...]) * f
    o_ref[...] = _layer_norm(y, lg_ref[...], lb_ref[...])


def _ffn(x, shift, scale, gate, wg, wu, wd, ln_g, ln_b, tm):
    t = x.shape[0]
    return pl.pallas_call(
        functools.partial(_ffn_kernel, n_ff=2),
        grid=(t // tm,),
        in_specs=[_row_spec(tm, D_MODEL), _mod_spec(shift, tm), _mod_spec(scale, tm), _mod_spec(gate, tm),
                  _const_spec((D_MODEL, D_FF)), _const_spec((D_MODEL, D_FF)), _const_spec((D_FF, D_MODEL)),
                  _const_spec((1, D_MODEL)), _const_spec((1, D_MODEL))],
        out_specs=_row_spec(tm, D_MODEL),
        out_shape=jax.ShapeDtypeStruct((t, D_MODEL), F32),
        compiler_params=_cparams("parallel"),
    )(x, shift, scale, gate, wg, wu, wd, ln_g, ln_b)


def _rotate(x, cos, s_up, s_dn):
    return x * cos + pltpu.roll(x, LANES - ROT_DIM // 2, 1) * s_up + pltpu.roll(x, ROT_DIM // 2, 1) * s_dn


def _inproj_kernel(x_ref, sh_ref, sc_ref, w_ref, cos_ref, sup_ref, sdn_ref,
                   pr_ref, q_ref, k_ref, v_ref, gr_ref, gm_ref, kb_ref, vt_ref, km_ref):
    h = (x_ref[...] * (1.0 + sc_ref[...]) + sh_ref[...]).astype(MXU_DT)
    tm = h.shape[0]

    def proj(lo, width):
        return jnp.dot(h, w_ref[:, lo:lo + width], preferred_element_type=F32)

    pr_ref[...] = proj(0, R_PROJ)
    cos, s_up, s_dn = cos_ref[...], sup_ref[...], sdn_ref[...]
    q = proj(R_PROJ, WIDTH)
    k = proj(R_PROJ + WIDTH, WIDTH)
    for j in range(WIDTH // LANES):
        ls = slice(j * LANES, (j + 1) * LANES)
        qj = _rotate(q[:, ls], cos, s_up, s_dn)
        kj = _rotate(k[:, ls], cos, s_up, s_dn)
        q_ref[:, ls] = (qj * (HEAD_DIM ** -0.5)).astype(MXU_DT)
        k_ref[:, ls] = kj
        kb_ref[:, ls] = kj.astype(MXU_DT)
        if km_ref is not None:
            km_ref[0, :, ls] = jnp.sum(kj.reshape(tm // MOBA_BLOCK, MOBA_BLOCK, LANES), axis=1) * (1.0 / MOBA_BLOCK)
    v = proj(R_PROJ + 2 * WIDTH, WIDTH)
    v_ref[...] = v
    if vt_ref is not None:
        for pair in range(WIDTH // LANES):
            base = pair * VT_ROWS
            vt_ref[base:base + LANES, :] = v[:, pair * LANES:(pair + 1) * LANES].T.astype(MXU_DT)
            vt_ref[base + LANES:base + VT_ROWS, :] = jnp.ones((VT_ROWS - LANES, tm), MXU_DT)
    gr_ref[...] = proj(R_PROJ + 3 * WIDTH, D_MODEL)
    gm_ref[...] = proj(R_PROJ + 3 * WIDTH + D_MODEL, D_MODEL)


def _inproj_prompt_kernel(*refs):
    _inproj_kernel(*refs)


def _inproj_sample_kernel(*refs):
    _inproj_kernel(*refs, None, None)


def _inproj(x, shift, scale, w_in, cos, s_up, s_dn, tm, prompt):
    t = x.shape[0]
    out_shape = [jax.ShapeDtypeStruct((t, R_PROJ), F32),
                 jax.ShapeDtypeStruct((t, WIDTH), MXU_DT),
                 jax.ShapeDtypeStruct((t, WIDTH), F32),
                 jax.ShapeDtypeStruct((t, WIDTH), F32),
                 jax.ShapeDtypeStruct((t, D_MODEL), F32),
                 jax.ShapeDtypeStruct((t, D_MODEL), F32),
                 jax.ShapeDtypeStruct((t, WIDTH), MXU_DT)]
    out_specs = [_row_spec(tm, R_PROJ), _row_spec(tm, WIDTH), _row_spec(tm, WIDTH), _row_spec(tm, WIDTH),
                 _row_spec(tm, D_MODEL), _row_spec(tm, D_MODEL), _row_spec(tm, WIDTH)]
    if prompt:
        nblk = tm // MOBA_BLOCK
        vt_rows = (WIDTH // LANES) * VT_ROWS
        out_shape += [jax.ShapeDtypeStruct((vt_rows, t), MXU_DT),
                      jax.ShapeDtypeStruct((t // tm, nblk, WIDTH), F32)]
        out_specs += [pl.BlockSpec((vt_rows, tm), lambda i: (0, i)),
                      pl.BlockSpec((1, nblk, WIDTH), lambda i: (i, 0, 0))]
    return pl.pallas_call(
        _inproj_prompt_kernel if prompt else _inproj_sample_kernel,
        grid=(t // tm,),
        in_specs=[_row_spec(tm, D_MODEL), _mod_spec(shift, tm), _mod_spec(scale, tm),
                  _const_spec((D_MODEL, IN_PROJ)),
                  _row_spec(tm, LANES), _row_spec(tm, LANES), _row_spec(tm, LANES)],
        out_specs=out_specs,
        out_shape=out_shape,
        compiler_params=_cparams("parallel"),
    )(x, shift, scale, w_in, cos, s_up, s_dn)


def _rope_tables(pos):
    half = ROT_DIM // 2
    inv = ROPE_THETA ** (-jnp.arange(half, dtype=F32) * 2.0 / ROT_DIM)
    ang = pos.astype(F32)[:, None] * inv[None, :]
    cos, sin = jnp.cos(ang), jnp.sin(ang)
    t = pos.shape[0]
    zeros = lambda n: jnp.zeros((t, n), F32)
    per_head = lambda parts: jnp.tile(jnp.concatenate(parts, axis=1), (1, LANES // HEAD_DIM))
    c = per_head([cos, cos, jnp.ones((t, HEAD_DIM - ROT_DIM), F32)])
    s_up = per_head([-sin, zeros(HEAD_DIM - half)])
    s_dn = per_head([zeros(half), sin, zeros(HEAD_DIM - ROT_DIM)])
    return c, s_up, s_dn


def _rwkv_pointwise(pr, prev, mu, wcat, w0, a0, g2, k_k, k_a, ones):
    xs = pr + (prev - pr) * mu
    r = xs[:, 0:WIDTH]
    k = xs[:, WIDTH:2 * WIDTH]
    v = xs[:, 2 * WIDTH:3 * WIDTH]
    la = xs[:, 3 * WIDTH:3 * WIDTH + LORA_W + LORA_A]
    gl = xs[:, 3 * WIDTH + LORA_W + LORA_A:R_PROJ]
    lane = lax.broadcasted_iota(jnp.int32, la.shape, 1)
    la = jnp.where(lane < LORA_W, jnp.tanh(la), la)
    wa = _dot(la, wcat)
    w = w0 + wa[:, :WIDTH]
    a = jax.nn.sigmoid(a0 + wa[:, WIDTH:])
    g = _dot(jax.nn.sigmoid(gl), g2)
    z = -w
    softplus = jnp.maximum(z, 0.0) + jnp.log1p(jnp.exp(-jnp.abs(z)))
    log_decay = -jnp.exp(-softplus - 0.5)
    kk = k * k_k
    kk = kk / jnp.maximum(jnp.sqrt(_seg_sum(kk * kk, ones)), 1e-12)
    k_eff = k * (1.0 + (a - 1.0) * k_a)
    return r, k_eff, v, kk, kk * a, log_decay, g


def _rwkv_post(y, r, k_eff, v, g, r_k, lnx_g, lnx_b, ones):
    inv = 1.0 / HEAD_DIM
    yc = y - _seg_sum(y, ones) * inv
    yv = _seg_sum(yc * yc, ones) * inv
    yn = yc * lax.rsqrt(yv + GN_EPS) * lnx_g + lnx_b
    bonus = _seg_sum(r * k_eff * r_k, ones) * v
    return (yn + bonus) * g


def _rwkv_prompt_kernel(pr_ref, shift0_ref, s0_ref, mu_ref, wcat_ref, w0_ref, a0_ref, g2_ref, kk_ref, ka_ref,
                        rk_ref, lg_ref, lb_ref, ones_ref, tri_ref, blk_ref,
                        out_ref, sfin_ref, shift_ref,
                        prev_scr, s_scr, rt_scr, kp_scr, bt_scr, kt_scr, v_scr, bc_scr, kc_scr, pc_scr, y_scr):
    i = pl.program_id(0)

    @pl.when(i == 0)
    def _():
        prev_scr[...] = shift0_ref[...]
        s_scr[...] = s0_ref[...]

    pr = pr_ref[...]
    tm = pr.shape[0]
    row = lax.broadcasted_iota(jnp.int32, (tm, 1), 0)
    prev = jnp.where(row == 0, prev_scr[...], pltpu.roll(pr, 1, 0))
    ones = ones_ref[...]
    r, k_eff, v, kk, b, lw, g = _rwkv_pointwise(pr, prev, mu_ref[...], wcat_ref[...], w0_ref[...], a0_ref[...],
                                                g2_ref[...], kk_ref[...], ka_ref[...], ones)
    l_hi = lw.astype(MXU_DT)
    rem = lw - l_hi.astype(F32)
    l_mid = rem.astype(MXU_DT)
    l_lo = (rem - l_mid.astype(F32)).astype(MXU_DT)
    cum = lambda m: (jnp.dot(m, l_hi, preferred_element_type=F32)
                     + (jnp.dot(m, l_mid, preferred_element_type=F32) + jnp.dot(m, l_lo, preferred_element_type=F32)))
    L = cum(tri_ref[...])
    LC = cum(blk_ref[...])
    e_inv = jnp.exp(-L)
    e_end = jnp.exp(LC - L)
    rt_scr[...] = r * jnp.exp(L)
    kp_scr[...] = kk * jnp.exp(L - lw)
    bt_scr[...] = b * e_inv
    kt_scr[...] = k_eff * e_inv
    bc_scr[...] = b * e_end
    kc_scr[...] = k_eff * e_end
    pc_scr[...] = jnp.exp(LC)
    v_scr[...] = v

    ri = lax.broadcasted_iota(jnp.int32, (CHUNK, CHUNK), 0)
    ci = lax.broadcasted_iota(jnp.int32, (CHUNK, CHUNK), 1)
    strict, incl, eye = ri > ci, ri >= ci, ri == ci

    def chunk(c, carry):
        r0 = pl.multiple_of(c * CHUNK, CHUNK)
        rows = pl.ds(r0, CHUNK)
        ys = []
        for h in range(N_HEADS):
            hs = slice(h * HEAD_DIM, (h + 1) * HEAD_DIM)
            kp, rt = kp_scr[rows, hs], rt_scr[rows, hs]
            kp_s, rt_s = _split(kp), _split(rt)
            bt_s, kt_s = _split(bt_scr[rows, hs]), _split(kt_scr[rows, hs])
            v_s, bc_s, kc_s = _split(v_scr[rows, hs]), _split(bc_scr[rows, hs]), _split(kc_scr[rows, hs])
            a_zb = jnp.where(strict, _mm3(kp_s, bt_s, _NT), 0.0)
            a_vk = jnp.where(strict, _mm3(kp_s, kt_s, _NT), 0.0)
            a_rb = jnp.where(incl, _mm3(rt_s, bt_s, _NT), 0.0)
            a_rk = jnp.where(incl, _mm3(rt_s, kt_s, _NT), 0.0)
            u = jnp.concatenate([kp, _mm3(_split(a_vk), v_s)], axis=1)
            x = -a_zb
            for j in range(6):
                x_s = _split(x)
                u = u + _mm3(x_s, _split(u))
                if j < 5:
                    x = _mm3(x_s, x_s)
            u_s = _split(u)
            gmat = _mm3(_split(a_rb), u_s)
            r_eff = rt - gmat[:, :HEAD_DIM]
            y0 = _mm3(_split(a_rk), v_s) - gmat[:, HEAD_DIM:]
            ut = _mm3(u_s, bc_s, _TN)
            m_mat = jnp.where(eye, pc_scr[pl.ds(r0, 1), hs], 0.0) - ut[:HEAD_DIM]
            n_mat = _mm3(v_s, kc_s, _TN) - ut[HEAD_DIM:]
            s0_s = _split(s_scr[h])
            ys.append(_mm3(_split(r_eff), s0_s, _NT) + y0)
            s_scr[h] = _mm3(s0_s, _split(m_mat)) + n_mat
        y_scr[rows, :] = jnp.concatenate(ys, axis=1)
        return carry

    lax.fori_loop(0, tm // CHUNK, chunk, 0)
    out_ref[...] = _rwkv_post(y_scr[...], r, k_eff, v, g, rk_ref[...], lg_ref[...], lb_ref[...], ones)
    prev_scr[...] = pr[tm - 1:tm, :]
    shift_ref[...] = pr[tm - 1:tm, :]
    sfin_ref[...] = s_scr[...]


def _rwkv_prompt(pr, shift0, s0, rp, tm):
    t = pr.shape[0]
    ch = lax.broadcasted_iota(jnp.int32, (tm, tm), 0) // CHUNK == lax.broadcasted_iota(jnp.int32, (tm, tm), 1) // CHUNK
    low = lax.broadcasted_iota(jnp.int32, (tm, tm), 0) >= lax.broadcasted_iota(jnp.int32, (tm, tm), 1)
    tri = (ch & low).astype(MXU_DT)
    blk = ch.astype(MXU_DT)
    vec = _const_spec((1, WIDTH))
    tile = pltpu.VMEM((tm, WIDTH), F32)
    return pl.pallas_call(
        _rwkv_prompt_kernel,
        grid=(t // tm,),
        in_specs=[_row_spec(tm, R_PROJ), _const_spec((1, R_PROJ)), _const_spec((N_HEADS, HEAD_DIM, HEAD_DIM)),
                  _const_spec((1, R_PROJ)), _const_spec((LORA_W + LORA_A, 2 * WIDTH)), vec, vec,
                  _const_spec((LORA_G, WIDTH)), vec, vec, vec, vec, vec,
                  _const_spec((WIDTH, WIDTH)), _const_spec((tm, tm)), _const_spec((tm, tm))],
        out_specs=[_row_spec(tm, WIDTH),
                   pl.BlockSpec((N_HEADS, HEAD_DIM, HEAD_DIM), lambda i: (0, 0, 0)),
                   pl.BlockSpec((1, R_PROJ), lambda i: (0, 0))],
        out_shape=[jax.ShapeDtypeStruct((t, WIDTH), F32),
                   jax.ShapeDtypeStruct((N_HEADS, HEAD_DIM, HEAD_DIM), F32),
                   jax.ShapeDtypeStruct((1, R_PROJ), F32)],
        scratch_shapes=[pltpu.VMEM((1, R_PROJ), F32), pltpu.VMEM((N_HEADS, HEAD_DIM, HEAD_DIM), F32)] + [tile] * 9,
        compiler_params=_cparams("arbitrary"),
    )(pr, shift0, s0, rp["mu"], rp["wcat"], rp["w0"], rp["a0"], rp["g2"], rp["k_k"], rp["k_a"], rp["r_k"],
      rp["lnx_g"], rp["lnx_b"], rp["ones"], tri, blk)


def _rwkv_step_kernel(pr_ref, shift_ref, s_ref, mu_ref, wcat_ref, w0_ref, a0_ref, g2_ref, kk_ref, ka_ref,
                      rk_ref, lg_ref, lb_ref, ones_ref, out_ref, snew_ref):
    ones = ones_ref[...]
    r, k_eff, v, kk, b, lw, g = _rwkv_pointwise(pr_ref[...], shift_ref[...], mu_ref[...], wcat_ref[...],
                                                w0_ref[...], a0_ref[...], g2_ref[...], kk_ref[...], ka_ref[...], ones)
    nb = r.shape[0]
    flat = lambda z: z.reshape(nb * HEAD_DIM, WIDTH)
    cube = lambda z: z.reshape(nb, HEAD_DIM, WIDTH)
    rows = lambda z: z[:, None, :]
    s = cube(s_ref[...])
    diag = (lax.broadcasted_iota(jnp.int32, (HEAD_DIM, WIDTH), 0)
            == lax.broadcasted_iota(jnp.int32, (HEAD_DIM, WIDTH), 1) % HEAD_DIM)[None]
    s_kk = cube(_seg_sum(flat(s * rows(kk)), ones))
    v_col = cube(_seg_sum(flat(jnp.where(diag, rows(v), 0.0)), ones))
    s_new = s * rows(jnp.exp(lw)) - s_kk * rows(b) + v_col * rows(k_eff)
    y_b = cube(_seg_sum(flat(s_new * rows(r)), ones))
    y = jnp.sum(jnp.where(diag, y_b, 0.0), axis=1)
    out_ref[...] = _rwkv_post(y, r, k_eff, v, g, rk_ref[...], lg_ref[...], lb_ref[...], ones)
    snew_ref[...] = flat(s_new)


def _rwkv_step(pr, shift, s_rows, rp, nb):
    n = pr.shape[0]
    vec = _const_spec((1, WIDTH))
    return pl.pallas_call(
        _rwkv_step_kernel,
        grid=(n // nb,),
        in_specs=[_row_spec(nb, R_PROJ), _row_spec(nb, R_PROJ), _row_spec(nb * HEAD_DIM, WIDTH),
                  _const_spec((1, R_PROJ)), _const_spec((LORA_W + LORA_A, 2 * WIDTH)), vec, vec,
                  _const_spec((LORA_G, WIDTH)), vec, vec, vec, vec, vec, _const_spec((WIDTH, WIDTH))],
        out_specs=[_row_spec(nb, WIDTH), _row_spec(nb * HEAD_DIM, WIDTH)],
        out_shape=[jax.ShapeDtypeStruct((n, WIDTH), F32), jax.ShapeDtypeStruct((n * HEAD_DIM, WIDTH), F32)],
        compiler_params=_cparams("parallel"),
    )(pr, shift, s_rows, rp["mu"], rp["wcat"], rp["w0"], rp["a0"], rp["g2"], rp["k_k"], rp["k_a"], rp["r_k"],
      rp["lnx_g"], rp["lnx_b"], rp["ones"])


def _moba_prompt_kernel(q_ref, km_ref, k_ref, vt_ref, o_ref, qm_scr, sel_scr, m_scr, acc_scr):
    c = pl.program_id(0)
    tq = q_ref.shape[0]
    nblk = km_ref.shape[0]
    lane = lax.broadcasted_iota(jnp.int32, (1, LANES), 1)
    sub = lax.broadcasted_iota(jnp.int32, (LANES, 1), 0)
    blk_id = lax.broadcasted_iota(jnp.int32, (nblk, tq), 0)
    key_i = lax.broadcasted_iota(jnp.int32, (MOBA_BLOCK, tq), 0)
    qry_i = lax.broadcasted_iota(jnp.int32, (MOBA_BLOCK, tq), 1)

    def head_lanes(h):
        return (lane < HEAD_DIM) if h % 2 == 0 else (lane >= HEAD_DIM)

    def values(h, b):
        rows = slice((h // 2) * VT_ROWS, (h // 2 + 1) * VT_ROWS)
        return vt_ref[rows, pl.ds(pl.multiple_of(b * MOBA_BLOCK, MOBA_BLOCK), MOBA_BLOCK)]

    def keys(h, b):
        ps = slice((h // 2) * LANES, (h // 2 + 1) * LANES)
        return k_ref[pl.ds(pl.multiple_of(b * MOBA_BLOCK, MOBA_BLOCK), MOBA_BLOCK), ps]

    for h in range(N_HEADS):
        ps = slice((h // 2) * LANES, (h // 2 + 1) * LANES)
        qm = jnp.where(head_lanes(h), q_ref[:, ps].astype(F32), 0.0).astype(MXU_DT)
        qm_scr[h] = qm
        gate = lax.dot_general(km_ref[:, ps], qm, _NT, preferred_element_type=F32)
        gate = jnp.where(blk_id < c, gate, NEG_INF)
        sel = jnp.zeros(gate.shape, jnp.bool_)
        for _ in range(MOBA_TOPK):
            top = jnp.max(gate, axis=0, keepdims=True)
            idx = jnp.min(jnp.where(gate == top, blk_id, nblk), axis=0, keepdims=True)
            hit = blk_id == idx
            sel = sel | hit
            gate = jnp.where(hit, NEG_INF, gate)
        sel_scr[h] = sel.astype(F32)
        s = lax.dot_general(keys(h, c), qm, _NT, preferred_element_type=F32)
        s = jnp.where(key_i <= qry_i, s, NEG_INF)
        m = jnp.max(s, axis=0, keepdims=True)
        p = jnp.exp(s - m).astype(MXU_DT)
        m_scr[h] = m
        acc_scr[h] = jnp.dot(values(h, c), p, preferred_element_type=F32)

    def past_block(b, carry):
        for h in range(N_HEADS):
            s = lax.dot_general(keys(h, b), qm_scr[h], _NT, preferred_element_type=F32)
            picked = sel_scr[h, pl.ds(b, 1), :] > 0.0
            m_old = m_scr[h]
            m_new = jnp.where(picked, jnp.maximum(m_old, jnp.max(s, axis=0, keepdims=True)), m_old)
            p = jnp.exp(s - jnp.where(picked, m_new, jnp.inf)).astype(MXU_DT)
            acc_scr[h] = acc_scr[h] * jnp.exp(m_old - m_new) + jnp.dot(values(h, b), p, preferred_element_type=F32)
            m_scr[h] = m_new
        return carry

    lax.fori_loop(0, c, past_block, 0)
    for pair in range(N_HEADS // 2):
        a0, a1 = acc_scr[2 * pair], acc_scr[2 * pair + 1]
        out_t = jnp.where(sub < HEAD_DIM, a0[:LANES] / a0[LANES:LANES + 1], a1[:LANES] / a1[LANES:LANES + 1])
        o_ref[:, pair * LANES:(pair + 1) * LANES] = out_t.T


def _moba_prompt(q, kmean, kb, vt):
    t = q.shape[0]
    nblk = kmean.shape[0]
    tq = MOBA_BLOCK
    return pl.pallas_call(
        _moba_prompt_kernel,
        grid=(t // tq,),
        in_specs=[_row_spec(tq, WIDTH), _const_spec((nblk, WIDTH)), _const_spec((t, WIDTH)),
                  _const_spec(vt.shape)],
        out_specs=_row_spec(tq, WIDTH),
        out_shape=jax.ShapeDtypeStruct((t, WIDTH), F32),
        scratch_shapes=[pltpu.VMEM((N_HEADS, tq, LANES), MXU_DT), pltpu.VMEM((N_HEADS, nblk, tq), F32),
                        pltpu.VMEM((N_HEADS, 1, tq), F32), pltpu.VMEM((N_HEADS, VT_ROWS, tq), F32)],
        compiler_params=_cparams("parallel"),
    )(q, kmean, kb, vt)


def _moba_sample_score_kernel(pt_ref, q_ref, knew_ref, *refs):
    del pt_ref
    pg = SAMPLE_PAGES_PER_STEP
    pages = refs[:pg]
    p_ref, pself_ref, idx_ref, qrow_scr, sc_scr, gate_scr = refs[pg:]
    j = pl.program_id(1)
    n_pages = sc_scr.shape[0]
    nblk = gate_scr.shape[0]

    @pl.when(j == 0)
    def _():
        head = lax.broadcasted_iota(jnp.int32, (N_HEADS, WIDTH), 0)
        lane_head = lax.broadcasted_iota(jnp.int32, (N_HEADS, WIDTH), 1) // HEAD_DIM
        qrow_scr[...] = jnp.where(head == lane_head, q_ref[0].astype(F32), 0.0)

    q_f32 = qrow_scr[...]
    q_rows = q_f32.astype(MXU_DT)
    for t in range(pg // 2):
        halves = (pages[2 * t][0], pages[2 * t + 1][0])
        for e, page in enumerate(halves):
            sc_scr[j * pg + 2 * t + e] = lax.dot_general(q_rows, page.astype(MXU_DT), _NT,
                                                         preferred_element_type=F32)
        kmean = (jnp.sum(halves[0], axis=0, keepdims=True) + jnp.sum(halves[1], axis=0, keepdims=True)) \
            * (1.0 / MOBA_BLOCK)
        gate = jnp.sum(q_f32 * kmean.astype(MXU_DT).astype(F32), axis=1, keepdims=True)
        gate_scr[j * (pg // 2) + t] = jnp.broadcast_to(gate, (N_HEADS, LANES))

    @pl.when(j == pl.num_programs(1) - 1)
    def _():
        gate = gate_scr[...]
        blk_id = lax.broadcasted_iota(jnp.int32, gate.shape, 0)
        sel = jnp.zeros(gate.shape, jnp.bool_)
        for r in range(MOBA_TOPK):
            top = jnp.max(gate, axis=0, keepdims=True)
            idx = jnp.min(jnp.where(gate == top, blk_id, nblk), axis=0, keepdims=True)
            hit = blk_id == idx
            sel = sel | hit
            gate = jnp.where(hit, NEG_INF, gate)
            idx_ref[0, r] = idx[0]
        pages_per_block = MOBA_BLOCK // PAGE_SIZE
        sel_pages = jnp.broadcast_to(sel[:, None], (nblk, pages_per_block, N_HEADS, LANES)).reshape(
            n_pages, N_HEADS, LANES)
        s = jnp.where(sel_pages, sc_scr[...], NEG_INF)
        s_self = jnp.sum(q_f32 * knew_ref[0].astype(MXU_DT).astype(F32), axis=1, keepdims=True)
        m = jnp.maximum(jnp.max(jnp.max(s, axis=0), axis=1, keepdims=True), s_self)
        p = jnp.exp(s - m)
        p_self = jnp.exp(s_self - m)
        inv = 1.0 / (jnp.sum(jnp.sum(p, axis=0), axis=1, keepdims=True) + p_self)
        p_ref[0] = p * inv
        pself_ref[0] = jnp.broadcast_to(p_self * inv, (N_HEADS, LANES))


def _moba_sample_scores(q, k_new, cache_k, page_table):
    n, n_pages = page_table.shape
    pg = SAMPLE_PAGES_PER_STEP
    nblk = n_pages * PAGE_SIZE // MOBA_BLOCK

    def page_spec(e):
        return pl.BlockSpec((1, PAGE_SIZE, WIDTH), lambda i, j, pt: (pt[i * n_pages + j * pg + e], 0, 0))

    grid_spec = pltpu.PrefetchScalarGridSpec(
        num_scalar_prefetch=1,
        grid=(n, n_pages // pg),
        in_specs=[pl.BlockSpec((1, 1, WIDTH), lambda i, j, pt: (i, 0, 0)),
                  pl.BlockSpec((1, 1, WIDTH), lambda i, j, pt: (i, 0, 0))] + [page_spec(e) for e in range(pg)],
        out_specs=[pl.BlockSpec((1, n_pages, N_HEADS, LANES), lambda i, j, pt: (i, 0, 0, 0)),
                   pl.BlockSpec((1, N_HEADS, LANES), lambda i, j, pt: (i, 0, 0)),
                   pl.BlockSpec((1, MOBA_TOPK, N_HEADS, LANES), lambda i, j, pt: (i, 0, 0, 0))],
        scratch_shapes=[pltpu.VMEM((N_HEADS, WIDTH), F32), pltpu.VMEM((n_pages, N_HEADS, LANES), F32),
                        pltpu.VMEM((nblk, N_HEADS, LANES), F32)],
    )
    return pl.pallas_call(
        _moba_sample_score_kernel,
        grid_spec=grid_spec,
        out_shape=[jax.ShapeDtypeStruct((n, n_pages, N_HEADS, LANES), F32),
                   jax.ShapeDtypeStruct((n, N_HEADS, LANES), F32),
                   jax.ShapeDtypeStruct((n, MOBA_TOPK, N_HEADS, LANES), jnp.int32)],
        compiler_params=_cparams("parallel", "arbitrary"),
    )(page_table.reshape(-1), q.reshape(n, 1, WIDTH), k_new.reshape(n, 1, WIDTH), *([cache_k] * pg))


def _moba_sample_value_kernel(phys_ref, logi_ref, *refs):
    del phys_ref, logi_ref
    npg = MOBA_TOPK * MOBA_BLOCK // PAGE_SIZE
    p_refs, v_refs = refs[:npg], refs[npg:2 * npg]
    pself_ref, vnew_ref, o_ref = refs[2 * npg:]
    h = pl.program_id(1)
    own = lax.broadcasted_iota(jnp.int32, (N_HEADS, LANES), 0) == h
    rnd = lambda z: z.astype(MXU_DT).astype(F32)
    acc = jnp.where(own, rnd(pself_ref[0]), 0.0) * rnd(vnew_ref[0])
    for e in range(npg):
        acc = acc + _dot(jnp.where(own, p_refs[e][0, 0], 0.0), v_refs[e][0])
    o_ref[0, 0] = acc


def _moba_sample_values(p, p_self, v_new, cache_v, phys, logi):
    n = p.shape[0]
    npg = MOBA_TOPK * MOBA_BLOCK // PAGE_SIZE

    def slot(i, h, e):
        return (i * N_HEADS + h) * npg + e

    p_specs = [pl.BlockSpec((1, 1, N_HEADS, LANES),
                            functools.partial(lambda i, h, ph, lg, e: (i, lg[slot(i, h, e)], 0, 0), e=e))
               for e in range(npg)]
    v_specs = [pl.BlockSpec((1, PAGE_SIZE, LANES),
                            functools.partial(lambda i, h, ph, lg, e: (ph[slot(i, h, e)], 0, h // 2), e=e))
               for e in range(npg)]
    grid_spec = pltpu.PrefetchScalarGridSpec(
        num_scalar_prefetch=2,
        grid=(n, N_HEADS),
        in_specs=p_specs + v_specs + [pl.BlockSpec((1, N_HEADS, LANES), lambda i, h, ph, lg: (i, 0, 0)),
                                      pl.BlockSpec((1, 1, LANES), lambda i, h, ph, lg: (i, 0, h // 2))],
        out_specs=pl.BlockSpec((1, 1, N_HEADS, LANES), lambda i, h, ph, lg: (i, h, 0, 0)),
    )
    return pl.pallas_call(
        _moba_sample_value_kernel,
        grid_spec=grid_spec,
        out_shape=jax.ShapeDtypeStruct((n, N_HEADS, N_HEADS, LANES), F32),
        compiler_params=_cparams("parallel", "arbitrary"),
    )(phys, logi, *([p] * npg), *([cache_v] * npg), p_self, v_new.reshape(n, 1, WIDTH))


def _moba_sample(q, k_new, v_new, cache_k, cache_v, page_table):
    n = q.shape[0]
    p, p_self, idx = _moba_sample_scores(q, k_new, cache_k, page_table)
    pages_per_block = MOBA_BLOCK // PAGE_SIZE
    blocks = jnp.transpose(idx[..., 0], (0, 2, 1))
    logi = (blocks[..., None] * pages_per_block + jnp.arange(pages_per_block, dtype=jnp.int32)).reshape(
        n, N_HEADS, -1)
    phys = jnp.take_along_axis(page_table[:, None, :], logi, axis=2)
    out = _moba_sample_values(p, p_self, v_new, cache_v, phys.reshape(-1), logi.reshape(-1))
    heads = jnp.arange(N_HEADS)
    own = out[:, heads, heads].reshape(n, N_HEADS, LANES // HEAD_DIM, HEAD_DIM)
    return own[:, heads, heads % (LANES // HEAD_DIM)].reshape(n, WIDTH)


def _merge_kernel(x_ref, r_ref, a_ref, gr_ref, gm_ref, gt_ref, wr_ref, wm_ref, wo_ref, lg_ref, lb_ref, o_ref):
    br = jax.nn.sigmoid(gr_ref[...]) * _dot(r_ref[...], wr_ref[...])
    bm = jax.nn.sigmoid(gm_ref[...]) * _dot(a_ref[...], wm_ref[...])
    merged = _dot(br + bm, wo_ref[...])
    o_ref[...] = _layer_norm(DN_ALPHA * x_ref[...] + gt_ref[...] * merged, lg_ref[...], lb_ref[...])


def _merge(x, r_out, att, g_r, g_m, gate, wr, wm, wo, ln_g, ln_b, tm):
    t = x.shape[0]
    return pl.pallas_call(
        _merge_kernel,
        grid=(t // tm,),
        in_specs=[_row_spec(tm, D_MODEL), _row_spec(tm, WIDTH), _row_spec(tm, WIDTH), _row_spec(tm, D_MODEL),
                  _row_spec(tm, D_MODEL), _mod_spec(gate, tm),
                  _const_spec((WIDTH, D_MODEL)), _const_spec((WIDTH, D_MODEL)), _const_spec((D_MODEL, D_MODEL)),
                  _const_spec((1, D_MODEL)), _const_spec((1, D_MODEL))],
        out_specs=_row_spec(tm, D_MODEL),
        out_shape=jax.ShapeDtypeStruct((t, D_MODEL), F32),
        compiler_params=_cparams("parallel"),
    )(x, r_out, att, g_r, g_m, gate, wr, wm, wo, ln_g, ln_b)


def _tile(t, want):
    return want if t % want == 0 else t


def _layer(x, mod, pos, wts, rp, rwkv_fn, moba_fn, prompt):
    t = x.shape[0]
    tm = _tile(t, 512)
    row = lambda z: z.reshape(1, -1)
    x1 = _ffn(x, mod[0], mod[1], mod[2], wts["ffn1_gate"], wts["ffn1_up"], wts["ffn1_down"],
              row(wts["ln1_g"]), row(wts["ln1_b"]), tm)
    cos, s_up, s_dn = _rope_tables(pos)
    proj = _inproj(x1, mod[3], mod[4], wts["w_in"], cos, s_up, s_dn, tm, prompt)
    pr, q, k_new, v_new, g_r, g_m, kb = proj[:7]
    r_out, wkv_new, shift_new = rwkv_fn(pr)
    att = moba_fn(q, k_new, v_new, kb, proj[7:])
    x2 = _merge(x1, r_out, att, g_r, g_m, mod[5], wts["w_br_rwkv"], wts["w_br_moba"], wts["w_out"],
                row(wts["ln2_g"]), row(wts["ln2_b"]), tm)
    y = _ffn(x2, mod[6], mod[7], mod[8], wts["ffn2_gate"], wts["ffn2_up"], wts["ffn2_down"],
             row(wts["ln3_g"]), row(wts["ln3_b"]), tm)
    return y, k_new, v_new, wkv_new, shift_new


def kernel(x_prompt, x_sample, cache_k, cache_v, state_rwkv_wkv, state_rwkv_shift, page_table, c_prompt, c_sample, w_ada, b_ada, ffn1_gate, ffn1_up, ffn1_down, ln1_g, ln1_b, w_in, mu_shift, rwkv_w0, rwkv_w2, rwkv_a0, rwkv_a2, rwkv_g2, rwkv_k_k, rwkv_k_a, rwkv_r_k, rwkv_lnx_g, rwkv_lnx_b, w_br_rwkv, w_br_moba, w_out, ln2_g, ln2_b, ffn2_gate, ffn2_up, ffn2_down, ln3_g, ln3_b):
    assert x_prompt.shape[0] == 1 and x_sample.shape[1] == 1 and w_ada.shape[0] == 1
    tp = x_prompt.shape[1]
    ns = x_sample.shape[0]
    n_pages = page_table.shape[1]
    past_len = n_pages * PAGE_SIZE
    assert tp % MOBA_BLOCK == 0 and past_len % MOBA_BLOCK == 0 and n_pages % SAMPLE_PAGES_PER_STEP == 0

    bf = lambda z: z[0].astype(MXU_DT)
    wts = {"ffn1_gate": bf(ffn1_gate), "ffn1_up": bf(ffn1_up), "ffn1_down": bf(ffn1_down),
           "ffn2_gate": bf(ffn2_gate), "ffn2_up": bf(ffn2_up), "ffn2_down": bf(ffn2_down),
           "w_in": bf(w_in), "w_br_rwkv": bf(w_br_rwkv), "w_br_moba": bf(w_br_moba), "w_out": bf(w_out),
           "ln1_g": ln1_g[0], "ln1_b": ln1_b[0], "ln2_g": ln2_g[0], "ln2_b": ln2_b[0],
           "ln3_g": ln3_g[0], "ln3_b": ln3_b[0]}
    zw = jnp.zeros((LORA_W, WIDTH), MXU_DT)
    head_of = jnp.arange(WIDTH) // HEAD_DIM
    row = lambda z: z.reshape(1, -1)
    rp = {"mu": row(mu_shift[0]),
          "wcat": jnp.concatenate([jnp.concatenate([bf(rwkv_w2), zw], axis=1),
                                   jnp.concatenate([zw, bf(rwkv_a2)], axis=1)], axis=0),
          "w0": row(rwkv_w0[0]), "a0": row(rwkv_a0[0]), "g2": bf(rwkv_g2),
          "k_k": row(rwkv_k_k[0]), "k_a": row(rwkv_k_a[0]), "r_k": row(rwkv_r_k[0]),
          "lnx_g": row(rwkv_lnx_g[0]), "lnx_b": row(rwkv_lnx_b[0]),
          "ones": (head_of[:, None] == head_of[None, :]).astype(MXU_DT)}

    n_mod = ((1 + ns + 7) // 8) * 8
    c_all = jnp.concatenate([c_prompt, c_sample, jnp.zeros((n_mod - 1 - ns, D_MODEL), F32)], axis=0)
    ada = _ada(c_all, w_ada[0], b_ada)
    mod_p = [ada[0:1, j * D_MODEL:(j + 1) * D_MODEL] for j in range(9)]
    mod_s = [ada[1:1 + ns, j * D_MODEL:(j + 1) * D_MODEL] for j in range(9)]

    def rwkv_p(pr):
        r_out, s_fin, shift = _rwkv_prompt(pr, jnp.zeros((1, R_PROJ), F32),
                                           jnp.zeros((N_HEADS, HEAD_DIM, HEAD_DIM), F32), rp, _tile(tp, 256))
        return r_out, s_fin, shift

    def moba_p(q, k_new, v_new, kb, extra):
        vt, kmean = extra
        return _moba_prompt(q, kmean.reshape(-1, WIDTH).astype(MXU_DT), kb, vt)

    y_p, k_p, v_p, wkv_p, shift_p = _layer(x_prompt[0], mod_p, jnp.arange(tp, dtype=jnp.int32), wts, rp,
                                           rwkv_p, moba_p, True)

    ck = cache_k[0].reshape(cache_k.shape[1], PAGE_SIZE, WIDTH)
    cv = cache_v[0].reshape(cache_v.shape[1], PAGE_SIZE, WIDTH)

    def rwkv_s(pr):
        s_rows = jnp.transpose(state_rwkv_wkv[0], (0, 2, 1, 3)).reshape(ns * HEAD_DIM, WIDTH)
        r_out, s_new = _rwkv_step(pr, state_rwkv_shift[0], s_rows, rp, _tile(ns, 8))
        s_new = jnp.transpose(s_new.reshape(ns, HEAD_DIM, N_HEADS, HEAD_DIM), (0, 2, 1, 3))
        return r_out, s_new, pr

    def moba_s(q, k_new, v_new, kb, extra):
        return _moba_sample(q, k_new, v_new, ck, cv, page_table)

    y_s, k_s, v_s, wkv_s, shift_s = _layer(x_sample[:, 0], mod_s, jnp.full((ns,), past_len, jnp.int32), wts, rp,
                                           rwkv_s, moba_s, False)

    heads = lambda z, n, t: z.reshape(1, n, t, N_HEADS, HEAD_DIM)
    return (y_p[None], y_s[:, None],
            heads(k_p, 1, tp), heads(v_p, 1, tp), heads(k_s, ns, 1), heads(v_s, ns, 1),
            wkv_p[None, None], wkv_s[None], shift_p[None], shift_s[None])
```

```python
import functools

import jax
import jax.numpy as jnp
from jax import lax
from jax.experimental import pallas as pl
from jax.experimental.pallas import tpu as pltpu

F32 = jnp.float32
MXU_DT = jnp.bfloat16

D_MODEL = 1024
PAGE_SIZE = 128
N_HEADS = 8
HEAD_DIM = 64
WIDTH = N_HEADS * HEAD_DIM
LORA_W = 64
LORA_A = 64
LORA_G = 128
R_PROJ = 3 * WIDTH + LORA_W + LORA_A + LORA_G
GN_EPS = 64e-5
MOBA_BLOCK = 256
MOBA_TOPK = 3
ROT_DIM = HEAD_DIM // 4
ROPE_THETA = 500000.0
IN_PROJ = R_PROJ + 3 * WIDTH + 2 * D_MODEL
D_FF = 2816
LN_EPS = 1e-5
DN_ALPHA = 2.0 ** 0.25
CHUNK = 64
LANES = 128
VMEM_LIMIT = 56 * 1024 * 1024
SAMPLE_PAGES_PER_STEP = 16
VT_ROWS = HEAD_DIM + 16

NEG_INF = float("-inf")


def _cparams(*sem, vmem=VMEM_LIMIT):
    return pltpu.CompilerParams(dimension_semantics=sem, vmem_limit_bytes=vmem)


def _const_spec(shape):
    return pl.BlockSpec(shape, lambda *_: (0,) * len(shape), pipeline_mode=pl.Buffered(1))


def _row_spec(tm, width):
    return pl.BlockSpec((tm, width), lambda i: (i, 0))


def _mod_spec(arr, tm):
    if arr.shape[0] == 1:
        return pl.BlockSpec((1, arr.shape[1]), lambda i: (0, 0))
    return pl.BlockSpec((tm, arr.shape[1]), lambda i: (i, 0))


def _dot(a, b):
    return jnp.dot(a.astype(MXU_DT), b.astype(MXU_DT), preferred_element_type=F32)


_NN = (((1,), (0,)), ((), ()))
_NT = (((1,), (1,)), ((), ()))
_TN = (((0,), (0,)), ((), ()))


def _dg(a, b, dn):
    return lax.dot_general(a.astype(MXU_DT), b.astype(MXU_DT), dn, preferred_element_type=F32)


def _split(x):
    hi = x.astype(MXU_DT)
    lo = (x - hi.astype(F32)).astype(MXU_DT)
    return hi, lo


def _mm3(a, b, dn=_NN):
    f = lambda x, y: lax.dot_general(x, y, dn, preferred_element_type=F32)
    return f(a[0], b[0]) + (f(a[0], b[1]) + f(a[1], b[0]))


def _seg_sum(x, ones):
    hi, lo = _split(x)
    return (jnp.dot(hi, ones, preferred_element_type=F32)
            + jnp.dot(lo, ones, preferred_element_type=F32))


def _layer_norm(y, g, b):
    mu = jnp.mean(y, axis=-1, keepdims=True)
    yc = y - mu
    var = jnp.mean(yc * yc, axis=-1, keepdims=True)
    return yc * lax.rsqrt(var + LN_EPS) * g + b


def _ada_kernel(c_ref, w_ref, b_ref, o_ref):
    c = c_ref[...]
    o_ref[...] = _dot(c * jax.nn.sigmoid(c), w_ref[...]) + b_ref[...]


def _ada(c, w_ada, b_ada):
    n, d = c.shape
    nout = w_ada.shape[1]
    tn = 1024
    return pl.pallas_call(
        _ada_kernel,
        grid=(nout // tn,),
        in_specs=[pl.BlockSpec((n, d), lambda j: (0, 0)),
                  pl.BlockSpec((d, tn), lambda j: (0, j)),
                  pl.BlockSpec((1, tn), lambda j: (0, j))],
        out_specs=pl.BlockSpec((n, tn), lambda j: (0, j)),
        out_shape=jax.ShapeDtypeStruct((n, nout), F32),
        compiler_params=_cparams("parallel"),
    )(c, w_ada, b_ada)


def _ffn_kernel(x_ref, sh_ref, sc_ref, gt_ref, wg_ref, wu_ref, wd_ref, lg_ref, lb_ref, o_ref, *, n_ff):
    x = x_ref[...]
    h = (x * (1.0 + sc_ref[...]) + sh_ref[...]).astype(MXU_DT)
    step = D_FF // n_ff
    f = None
    for j in range(n_ff):
        cs = slice(j * step, (j + 1) * step)
        g = jnp.dot(h, wg_ref[:, cs], preferred_element_type=F32)
        u = jnp.dot(h, wu_ref[:, cs], preferred_element_type=F32)
        a = (g * jax.nn.sigmoid(g) * u).astype(MXU_DT)
        part = jnp.dot(a, wd_ref[cs, :], preferred_element_type=F32)
        f = part if f is None else f + part
    y = DN_ALPHA * x + (0.5 * gt_ref[...]) * f
    o_ref[...] = _layer_norm(y, lg_ref[...], lb_ref[...])


def _ffn(x, shift, scale, gate, wg, wu, wd, ln_g, ln_b, tm):
    t = x.shape[0]
    return pl.pallas_call(
        functools.partial(_ffn_kernel, n_ff=2),
        grid=(t // tm,),
        in_specs=[_row_spec(tm, D_MODEL), _mod_spec(shift, tm), _mod_spec(scale, tm), _mod_spec(gate, tm),
                  _const_spec((D_MODEL, D_FF)), _const_spec((D_MODEL, D_FF)), _const_spec((D_FF, D_MODEL)),
                  _const_spec((1, D_MODEL)), _const_spec((1, D_MODEL))],
        out_specs=_row_spec(tm, D_MODEL),
        out_shape=jax.ShapeDtypeStruct((t, D_MODEL), F32),
        compiler_params=_cparams("parallel"),
    )(x, shift, scale, gate, wg, wu, wd, ln_g, ln_b)


def _rotate(x, cos, s_up, s_dn):
    return x * cos + pltpu.roll(x, LANES - ROT_DIM // 2, 1) * s_up + pltpu.roll(x, ROT_DIM // 2, 1) * s_dn


def _inproj_kernel(x_ref, sh_ref, sc_ref, w_ref, cos_ref, sup_ref, sdn_ref,
                   pr_ref, q_ref, k_ref, v_ref, gr_ref, gm_ref, kb_ref, vt_ref, km_ref):
    h = (x_ref[...] * (1.0 + sc_ref[...]) + sh_ref[...]).astype(MXU_DT)
    tm = h.shape[0]

    def proj(lo, width):
        return jnp.dot(h, w_ref[:, lo:lo + width], preferred_element_type=F32)

    pr_ref[...] = proj(0, R_PROJ)
    cos, s_up, s_dn = cos_ref[...], sup_ref[...], sdn_ref[...]
    q = proj(R_PROJ, WIDTH)
    k = proj(R_PROJ + WIDTH, WIDTH)
    for j in range(WIDTH // LANES):
        ls = slice(j * LANES, (j + 1) * LANES)
        qj = _rotate(q[:, ls], cos, s_up, s_dn)
        kj = _rotate(k[:, ls], cos, s_up, s_dn)
        q_ref[:, ls] = (qj * (HEAD_DIM ** -0.5)).astype(MXU_DT)
        k_ref[:, ls] = kj
        kb_ref[:, ls] = kj.astype(MXU_DT)
        if km_ref is not None:
            km_ref[0, :, ls] = jnp.sum(kj.reshape(tm // MOBA_BLOCK, MOBA_BLOCK, LANES), axis=1) * (1.0 / MOBA_BLOCK)
    v = proj(R_PROJ + 2 * WIDTH, WIDTH)
    v_ref[...] = v
    if vt_ref is not None:
        v_t = v.T
        for head in range(N_HEADS):
            base = head * VT_ROWS
            vt_ref[base:base + HEAD_DIM, :] = v_t[head * HEAD_DIM:(head + 1) * HEAD_DIM, :].astype(MXU_DT)
            vt_ref[base + HEAD_DIM:base + VT_ROWS, :] = jnp.ones((VT_ROWS - HEAD_DIM, tm), MXU_DT)
    gr_ref[...] = proj(R_PROJ + 3 * WIDTH, D_MODEL)
    gm_ref[...] = proj(R_PROJ + 3 * WIDTH + D_MODEL, D_MODEL)


def _inproj_prompt_kernel(*refs):
    _inproj_kernel(*refs)


def _inproj_sample_kernel(*refs):
    _inproj_kernel(*refs, None, None)


def _inproj(x, shift, scale, w_in, cos, s_up, s_dn, tm, prompt):
    t = x.shape[0]
    out_shape = [jax.ShapeDtypeStruct((t, R_PROJ), F32),
                 jax.ShapeDtypeStruct((t, WIDTH), MXU_DT),
                 jax.ShapeDtypeStruct((t, WIDTH), F32),
                 jax.ShapeDtypeStruct((t, WIDTH), F32),
                 jax.ShapeDtypeStruct((t, D_MODEL), F32),
                 jax.ShapeDtypeStruct((t, D_MODEL), F32),
                 jax.ShapeDtypeStruct((t, WIDTH), MXU_DT)]
    out_specs = [_row_spec(tm, R_PROJ), _row_spec(tm, WIDTH), _row_spec(tm, WIDTH), _row_spec(tm, WIDTH),
                 _row_spec(tm, D_MODEL), _row_spec(tm, D_MODEL), _row_spec(tm, WIDTH)]
    if prompt:
        nblk = tm // MOBA_BLOCK
        vt_rows = N_HEADS * VT_ROWS
        out_shape += [jax.ShapeDtypeStruct((vt_rows, t), MXU_DT),
                      jax.ShapeDtypeStruct((t // tm, nblk, WIDTH), F32)]
        out_specs += [pl.BlockSpec((vt_rows, tm), lambda i: (0, i)),
                      pl.BlockSpec((1, nblk, WIDTH), lambda i: (i, 0, 0))]
    return pl.pallas_call(
        _inproj_prompt_kernel if prompt else _inproj_sample_kernel,
        grid=(t // tm,),
        in_specs=[_row_spec(tm, D_MODEL), _mod_spec(shift, tm), _mod_spec(scale, tm),
                  _const_spec((D_MODEL, IN_PROJ)),
                  _row_spec(tm, LANES), _row_spec(tm, LANES), _row_spec(tm, LANES)],
        out_specs=out_specs,
        out_shape=out_shape,
        compiler_params=_cparams("parallel"),
    )(x, shift, scale, w_in, cos, s_up, s_dn)


def _rope_tables(pos):
    half = ROT_DIM // 2
    inv = ROPE_THETA ** (-jnp.arange(half, dtype=F32) * 2.0 / ROT_DIM)
    ang = pos.astype(F32)[:, None] * inv[None, :]
    cos, sin = jnp.cos(ang), jnp.sin(ang)
    t = pos.shape[0]
    zeros = lambda n: jnp.zeros((t, n), F32)
    per_head = lambda parts: jnp.tile(jnp.concatenate(parts, axis=1), (1, LANES // HEAD_DIM))
    c = per_head([cos, cos, jnp.ones((t, HEAD_DIM - ROT_DIM), F32)])
    s_up = per_head([-sin, zeros(HEAD_DIM - half)])
    s_dn = per_head([zeros(half), sin, zeros(HEAD_DIM - ROT_DIM)])
    return c, s_up, s_dn


def _rwkv_pointwise(pr, prev, mu, wcat, w0, a0, g2, k_k, k_a, ones):
    xs = pr + (prev - pr) * mu
    r = xs[:, 0:WIDTH]
    k = xs[:, WIDTH:2 * WIDTH]
    v = xs[:, 2 * WIDTH:3 * WIDTH]
    la = xs[:, 3 * WIDTH:3 * WIDTH + LORA_W + LORA_A]
    gl = xs[:, 3 * WIDTH + LORA_W + LORA_A:R_PROJ]
    lane = lax.broadcasted_iota(jnp.int32, la.shape, 1)
    la = jnp.where(lane < LORA_W, jnp.tanh(la), la)
    wa = _dot(la, wcat)
    w = w0 + wa[:, :WIDTH]
    a = jax.nn.sigmoid(a0 + wa[:, WIDTH:])
    g = _dot(jax.nn.sigmoid(gl), g2)
    z = -w
    softplus = jnp.maximum(z, 0.0) + jnp.log1p(jnp.exp(-jnp.abs(z)))
    log_decay = -jnp.exp(-softplus - 0.5)
    kk = k * k_k
    kk = kk / jnp.maximum(jnp.sqrt(_seg_sum(kk * kk, ones)), 1e-12)
    k_eff = k * (1.0 + (a - 1.0) * k_a)
    return r, k_eff, v, kk, kk * a, log_decay, g


def _rwkv_post(y, r, k_eff, v, g, r_k, lnx_g, lnx_b, ones):
    inv = 1.0 / HEAD_DIM
    yc = y - _seg_sum(y, ones) * inv
    yv = _seg_sum(yc * yc, ones) * inv
    yn = yc * lax.rsqrt(yv + GN_EPS) * lnx_g + lnx_b
    bonus = _seg_sum(r * k_eff * r_k, ones) * v
    return (yn + bonus) * g


def _rwkv_prompt_kernel(pr_ref, shift0_ref, s0_ref, mu_ref, wcat_ref, w0_ref, a0_ref, g2_ref, kk_ref, ka_ref,
                        rk_ref, lg_ref, lb_ref, ones_ref, tri_ref, blk_ref,
                        out_ref, sfin_ref, shift_ref,
                        prev_scr, st_scr, rt_scr, kp_scr, bt_scr, kt_scr, v_scr, bc_scr, kc_scr, pc_scr, y_scr):
    i = pl.program_id(0)

    @pl.when(i == 0)
    def _():
        prev_scr[...] = shift0_ref[...]
        st_scr[...] = s0_ref[...]

    pr = pr_ref[...]
    tm = pr.shape[0]
    row = lax.broadcasted_iota(jnp.int32, (tm, 1), 0)
    prev = jnp.where(row == 0, prev_scr[...], pltpu.roll(pr, 1, 0))
    ones = ones_ref[...]
    r, k_eff, v, kk, b, lw, g = _rwkv_pointwise(pr, prev, mu_ref[...], wcat_ref[...], w0_ref[...], a0_ref[...],
                                                g2_ref[...], kk_ref[...], ka_ref[...], ones)
    l_hi = lw.astype(MXU_DT)
    rem = lw - l_hi.astype(F32)
    l_mid = rem.astype(MXU_DT)
    l_lo = (rem - l_mid.astype(F32)).astype(MXU_DT)
    cum = lambda m: (jnp.dot(m, l_hi, preferred_element_type=F32)
                     + (jnp.dot(m, l_mid, preferred_element_type=F32) + jnp.dot(m, l_lo, preferred_element_type=F32)))
    L = cum(tri_ref[...])
    LC = cum(blk_ref[...])
    e_inv = jnp.exp(-L)
    e_end = jnp.exp(LC - L)
    rt_scr[...] = r * jnp.exp(L)
    kp_scr[...] = kk * jnp.exp(L - lw)
    bt_scr[...] = b * e_inv
    kt_scr[...] = k_eff * e_inv
    bc_scr[...] = b * e_end
    kc_scr[...] = k_eff * e_end
    pc_scr[...] = jnp.exp(LC)
    v_scr[...] = v

    ri = lax.broadcasted_iota(jnp.int32, (CHUNK, CHUNK), 0)
    ci = lax.broadcasted_iota(jnp.int32, (CHUNK, CHUNK), 1)
    strict, incl, eye = ri > ci, ri >= ci, ri == ci

    n_chunks = tm // CHUNK
    probs = [(c, h) for c in range(n_chunks) for h in range(N_HEADS)]
    each = lambda fn, *lists: [fn(*args) for args in zip(*lists)]

    def blocks(ref):
        return [ref[c * CHUNK:(c + 1) * CHUNK, h * HEAD_DIM:(h + 1) * HEAD_DIM] for c, h in probs]

    def blocks_t(ref):
        per_chunk = [ref[c * CHUNK:(c + 1) * CHUNK, :].T for c in range(n_chunks)]
        return [per_chunk[c][h * HEAD_DIM:(h + 1) * HEAD_DIM, :] for c, h in probs]

    kp, rt, v_b = blocks(kp_scr), blocks(rt_scr), blocks(v_scr)
    kp_s, bt_s, kt_s, v_s = each(_split, kp), each(_split, blocks(bt_scr)), each(_split, blocks(kt_scr)), \
        each(_split, v_b)
    bt_1, kt_1 = [s[0] for s in bt_s], [s[0] for s in kt_s]
    a_zb = each(lambda a, b_: jnp.where(strict, _mm3(a, b_, _NT), 0.0), kp_s, bt_s)
    a_vk = each(lambda a, b_: jnp.where(strict, _mm3(a, b_, _NT), 0.0), kp_s, kt_s)
    a_rb = each(lambda a, b_: jnp.where(incl, _dg(a, b_, _NT), 0.0), rt, bt_1)
    a_rk = each(lambda a, b_: jnp.where(incl, _dg(a, b_, _NT), 0.0), rt, kt_1)
    u = each(lambda a, w: jnp.concatenate([a, w], axis=1), kp,
             each(lambda a, b_: _mm3(_split(a), b_), a_vk, v_s))
    x = [-a for a in a_zb]
    for j in range(6):
        x_s = each(_split, x)
        u = each(lambda u_, xs: u_ + _mm3(xs, _split(u_)), u, x_s)
        if j < 5:
            x = each(lambda xs: _mm3(xs, xs), x_s)
    u_s = each(_split, u)
    gmat = each(lambda a, us: _dg(a, us[0], _NN), a_rb, u_s)
    r_eff = each(lambda a, g_: a - g_[:, :HEAD_DIM], rt, gmat)
    y0 = each(lambda a, b_, g_: _dg(a, b_[0], _NN) - g_[:, HEAD_DIM:], a_rk, v_s, gmat)
    bu = each(lambda a, us: _mm3(_split(a), us), blocks_t(bc_scr), u_s)
    pc = [pc_scr[c * CHUNK:c * CHUNK + 1, h * HEAD_DIM:(h + 1) * HEAD_DIM] for c, h in probs]
    m_t = each(lambda p_, b_: jnp.where(eye, p_, 0.0) - b_[:, :HEAD_DIM], pc, bu)
    n_t = each(lambda a, b_, c_: _mm3(_split(a), b_) - c_[:, HEAD_DIM:], blocks_t(kc_scr), v_s, bu)
    st = [st_scr[h] for h in range(N_HEADS)]
    for c in range(n_chunks):
        sl = slice(c * N_HEADS, (c + 1) * N_HEADS)
        y_scr[c * CHUNK:(c + 1) * CHUNK, :] = jnp.concatenate(
            each(lambda a, s_, b_: _dg(a, s_, _NN) + b_, r_eff[sl], st, y0[sl]), axis=1)
        st = each(lambda m_, s_, n_: _mm3(_split(m_), _split(s_)) + n_, m_t[sl], st, n_t[sl])
    for h in range(N_HEADS):
        st_scr[h] = st[h]
    out_ref[...] = _rwkv_post(y_scr[...], r, k_eff, v, g, rk_ref[...], lg_ref[...], lb_ref[...], ones)
    prev_scr[...] = pr[tm - 1:tm, :]
    shift_ref[...] = pr[tm - 1:tm, :]

    @pl.when(i == pl.num_programs(0) - 1)
    def _():
        for h in range(N_HEADS):
            sfin_ref[h] = st[h].T


def _rwkv_prompt(pr, shift0, s0, rp, tm):
    t = pr.shape[0]
    ch = lax.broadcasted_iota(jnp.int32, (tm, tm), 0) // CHUNK == lax.broadcasted_iota(jnp.int32, (tm, tm), 1) // CHUNK
    low = lax.broadcasted_iota(jnp.int32, (tm, tm), 0) >= lax.broadcasted_iota(jnp.int32, (tm, tm), 1)
    tri = (ch & low).astype(MXU_DT)
    blk = ch.astype(MXU_DT)
    vec = _const_spec((1, WIDTH))
    tile = pltpu.VMEM((tm, WIDTH), F32)
    return pl.pallas_call(
        _rwkv_prompt_kernel,
        grid=(t // tm,),
        in_specs=[_row_spec(tm, R_PROJ), _const_spec((1, R_PROJ)), _const_spec((N_HEADS, HEAD_DIM, HEAD_DIM)),
                  _const_spec((1, R_PROJ)), _const_spec((LORA_W + LORA_A, 2 * WIDTH)), vec, vec,
                  _const_spec((LORA_G, WIDTH)), vec, vec, vec, vec, vec,
                  _const_spec((WIDTH, WIDTH)), _const_spec((tm, tm)), _const_spec((tm, tm))],
        out_specs=[_row_spec(tm, WIDTH),
                   pl.BlockSpec((N_HEADS, HEAD_DIM, HEAD_DIM), lambda i: (0, 0, 0)),
                   pl.BlockSpec((1, R_PROJ), lambda i: (0, 0))],
        out_shape=[jax.ShapeDtypeStruct((t, WIDTH), F32),
                   jax.ShapeDtypeStruct((N_HEADS, HEAD_DIM, HEAD_DIM), F32),
                   jax.ShapeDtypeStruct((1, R_PROJ), F32)],
        scratch_shapes=[pltpu.VMEM((1, R_PROJ), F32), pltpu.VMEM((N_HEADS, HEAD_DIM, HEAD_DIM), F32)] + [tile] * 9,
        compiler_params=_cparams("arbitrary"),
    )(pr, shift0, jnp.swapaxes(s0, 1, 2), rp["mu"], rp["wcat"], rp["w0"], rp["a0"], rp["g2"], rp["k_k"], rp["k_a"],
      rp["r_k"], rp["lnx_g"], rp["lnx_b"], rp["ones"], tri, blk)


def _rwkv_step_kernel(pr_ref, shift_ref, s_ref, mu_ref, wcat_ref, w0_ref, a0_ref, g2_ref, kk_ref, ka_ref,
                      rk_ref, lg_ref, lb_ref, ones_ref, out_ref, snew_ref):
    ones = ones_ref[...]
    r, k_eff, v, kk, b, lw, g = _rwkv_pointwise(pr_ref[...], shift_ref[...], mu_ref[...], wcat_ref[...],
                                                w0_ref[...], a0_ref[...], g2_ref[...], kk_ref[...], ka_ref[...], ones)
    nb = r.shape[0]
    flat = lambda z: z.reshape(nb * HEAD_DIM, WIDTH)
    cube = lambda z: z.reshape(nb, HEAD_DIM, WIDTH)
    rows = lambda z: z[:, None, :]
    s = cube(s_ref[...])
    diag = (lax.broadcasted_iota(jnp.int32, (HEAD_DIM, WIDTH), 0)
            == lax.broadcasted_iota(jnp.int32, (HEAD_DIM, WIDTH), 1) % HEAD_DIM)[None]
    s_kk = cube(_seg_sum(flat(s * rows(kk)), ones))
    v_col = cube(_seg_sum(flat(jnp.where(diag, rows(v), 0.0)), ones))
    s_new = s * rows(jnp.exp(lw)) - s_kk * rows(b) + v_col * rows(k_eff)
    y_b = cube(_seg_sum(flat(s_new * rows(r)), ones))
    y = jnp.sum(jnp.where(diag, y_b, 0.0), axis=1)
    out_ref[...] = _rwkv_post(y, r, k_eff, v, g, rk_ref[...], lg_ref[...], lb_ref[...], ones)
    snew_ref[...] = flat(s_new)


def _rwkv_step(pr, shift, s_rows, rp, nb):
    n = pr.shape[0]
    vec = _const_spec((1, WIDTH))
    return pl.pallas_call(
        _rwkv_step_kernel,
        grid=(n // nb,),
        in_specs=[_row_spec(nb, R_PROJ), _row_spec(nb, R_PROJ), _row_spec(nb * HEAD_DIM, WIDTH),
                  _const_spec((1, R_PROJ)), _const_spec((LORA_W + LORA_A, 2 * WIDTH)), vec, vec,
                  _const_spec((LORA_G, WIDTH)), vec, vec, vec, vec, vec, _const_spec((WIDTH, WIDTH))],
        out_specs=[_row_spec(nb, WIDTH), _row_spec(nb * HEAD_DIM, WIDTH)],
        out_shape=[jax.ShapeDtypeStruct((n, WIDTH), F32), jax.ShapeDtypeStruct((n * HEAD_DIM, WIDTH), F32)],
        compiler_params=_cparams("parallel"),
    )(pr, shift, s_rows, rp["mu"], rp["wcat"], rp["w0"], rp["a0"], rp["g2"], rp["k_k"], rp["k_a"], rp["r_k"],
      rp["lnx_g"], rp["lnx_b"], rp["ones"])


def _moba_prompt_kernel(q_ref, km_ref, k_ref, vt_ref, o_ref, qm_scr, sel_scr, m_scr, acc_scr,
                        s0_scr, s1_scr, cm_scr):
    c = pl.program_id(0)
    tq = q_ref.shape[0]
    nblk = km_ref.shape[0]
    lane = lax.broadcasted_iota(jnp.int32, (1, LANES), 1)
    blk_id = lax.broadcasted_iota(jnp.int32, (nblk, tq), 0)
    key_i = lax.broadcasted_iota(jnp.int32, (MOBA_BLOCK, tq), 0)
    qry_i = lax.broadcasted_iota(jnp.int32, (MOBA_BLOCK, tq), 1)

    def head_lanes(h):
        return (lane < HEAD_DIM) if h % 2 == 0 else (lane >= HEAD_DIM)

    def values(h, b):
        rows = slice(h * VT_ROWS, (h + 1) * VT_ROWS)
        return vt_ref[rows, pl.ds(pl.multiple_of(b * MOBA_BLOCK, MOBA_BLOCK), MOBA_BLOCK)]

    def keys(h, b):
        ps = slice((h // 2) * LANES, (h // 2 + 1) * LANES)
        return k_ref[pl.ds(pl.multiple_of(b * MOBA_BLOCK, MOBA_BLOCK), MOBA_BLOCK), ps]

    for h in range(N_HEADS):
        ps = slice((h // 2) * LANES, (h // 2 + 1) * LANES)
        qm = jnp.where(head_lanes(h), q_ref[:, ps].astype(F32), 0.0).astype(MXU_DT)
        qm_scr[h] = qm
        gate = lax.dot_general(km_ref[:, ps], qm, _NT, preferred_element_type=F32)
        gate = jnp.where(blk_id < c, gate, NEG_INF)
        sel = jnp.zeros(gate.shape, jnp.bool_)
        for _ in range(MOBA_TOPK):
            top = jnp.max(gate, axis=0, keepdims=True)
            idx = jnp.min(jnp.where(gate == top, blk_id, nblk), axis=0, keepdims=True)
            hit = blk_id == idx
            sel = sel | hit
            gate = jnp.where(hit, NEG_INF, gate)
        sel_scr[h] = sel.astype(F32)
        s = lax.dot_general(keys(h, c), qm, _NT, preferred_element_type=F32)
        s = jnp.where(key_i <= qry_i, s, NEG_INF)
        m = jnp.max(s, axis=0, keepdims=True)
        p = jnp.exp(s - m).astype(MXU_DT)
        m_scr[h] = m
        acc_scr[h] = jnp.dot(values(h, c), p, preferred_element_type=F32)

    def scores(b, s_scr, slot):
        for h in range(N_HEADS):
            s = lax.dot_general(keys(h, b), qm_scr[h], _NT, preferred_element_type=F32)
            s_scr[h] = s
            cm_scr[slot, h] = jnp.max(s, axis=0, keepdims=True)

    def absorb(b, s_scr, slot):
        for h in range(N_HEADS):
            picked = sel_scr[h, pl.ds(b, 1), :] > 0.0
            m_old = m_scr[h]
            m_new = jnp.where(picked, jnp.maximum(m_old, cm_scr[slot, h]), m_old)
            p = jnp.exp(s_scr[h] - jnp.where(picked, m_new, jnp.inf)).astype(MXU_DT)
            acc_scr[h] = acc_scr[h] * jnp.exp(m_old - m_new) + jnp.dot(values(h, b), p, preferred_element_type=F32)
            m_scr[h] = m_new

    def block_pair(i, carry):
        scores(2 * i, s0_scr, 0)
        scores(2 * i + 1, s1_scr, 1)
        absorb(2 * i, s0_scr, 0)
        absorb(2 * i + 1, s1_scr, 1)
        return carry

    lax.fori_loop(0, c // 2, block_pair, 0)

    @pl.when(c % 2 == 1)
    def _():
        scores(c - 1, s0_scr, 0)
        absorb(c - 1, s0_scr, 0)

    for pair in range(N_HEADS // 2):
        outs = []
        for h in (2 * pair, 2 * pair + 1):
            a = acc_scr[h]
            outs.append((a[:HEAD_DIM] / a[HEAD_DIM:HEAD_DIM + 1]).T)
        o_ref[:, pair * LANES:(pair + 1) * LANES] = jnp.concatenate(outs, axis=1)


def _moba_prompt(q, kmean, kb, vt):
    t = q.shape[0]
    nblk = kmean.shape[0]
    tq = MOBA_BLOCK
    return pl.pallas_call(
        _moba_prompt_kernel,
        grid=(t // tq,),
        in_specs=[_row_spec(tq, WIDTH), _const_spec((nblk, WIDTH)), _const_spec((t, WIDTH)),
                  _const_spec(vt.shape)],
        out_specs=_row_spec(tq, WIDTH),
        out_shape=jax.ShapeDtypeStruct((t, WIDTH), F32),
        scratch_shapes=[pltpu.VMEM((N_HEADS, tq, LANES), MXU_DT), pltpu.VMEM((N_HEADS, nblk, tq), F32),
                        pltpu.VMEM((N_HEADS, 1, tq), F32), pltpu.VMEM((N_HEADS, VT_ROWS, tq), F32),
                        pltpu.VMEM((N_HEADS, MOBA_BLOCK, tq), F32), pltpu.VMEM((N_HEADS, MOBA_BLOCK, tq), F32),
                        pltpu.VMEM((2, N_HEADS, 1, tq), F32)],
        compiler_params=_cparams("parallel"),
    )(q, kmean, kb, vt)


def _moba_sample_score_kernel(pt_ref, q_ref, knew_ref, *refs):
    del pt_ref
    pg = SAMPLE_PAGES_PER_STEP
    pages = refs[:pg]
    p_ref, pself_ref, idx_ref, sc_scr, gate_scr = refs[pg:]
    j = pl.program_id(1)
    n_pages = sc_scr.shape[0]
    nblk = gate_scr.shape[0]
    rnd = lambda z: z.astype(MXU_DT).astype(F32)
    q = q_ref[0]
    for t in range(pg // 2):
        halves = (pages[2 * t][0], pages[2 * t + 1][0])
        for e, page in enumerate(halves):
            sc_scr[j * pg + 2 * t + e] = jnp.sum(rnd(page) * q, axis=1, keepdims=True)
        kmean = jnp.sum(halves[0] + halves[1], axis=2, keepdims=True) * (1.0 / MOBA_BLOCK)
        gate = jnp.sum(q[:, :, :1] * rnd(kmean), axis=1, keepdims=True)
        gate_scr[j * (pg // 2) + t] = jnp.broadcast_to(gate, (N_HEADS, 1, LANES))

    @pl.when(j == pl.num_programs(1) - 1)
    def _():
        gate = gate_scr[...]
        blk_id = lax.broadcasted_iota(jnp.int32, gate.shape, 0)
        sel = jnp.zeros(gate.shape, jnp.bool_)
        for r in range(MOBA_TOPK):
            top = jnp.max(gate, axis=0, keepdims=True)
            idx = jnp.min(jnp.where(gate == top, blk_id, nblk), axis=0, keepdims=True)
            hit = blk_id == idx
            sel = sel | hit
            gate = jnp.where(hit, NEG_INF, gate)
            idx_ref[0, r] = idx[0]
        pages_per_block = MOBA_BLOCK // PAGE_SIZE
        sel_f = jnp.where(sel, 1.0, 0.0)
        sel_pages = jnp.broadcast_to(sel_f[:, None], (nblk, pages_per_block, N_HEADS, 1, LANES)).reshape(
            n_pages, N_HEADS, 1, LANES)
        s = jnp.where(sel_pages > 0.0, sc_scr[...], NEG_INF)
        s_self = jnp.sum(q * rnd(knew_ref[0]), axis=1, keepdims=True)
        m = jnp.maximum(jnp.max(jnp.max(s, axis=0), axis=2, keepdims=True), s_self)
        p = jnp.exp(s - m)
        p_self = jnp.exp(s_self - m)
        inv = 1.0 / (jnp.sum(jnp.sum(p, axis=0), axis=2, keepdims=True) + p_self)
        p_ref[0] = p * inv
        pself_ref[0] = p_self * inv


def _moba_sample_scores(q_lanes, knew_lanes, cache_kt, page_table):
    n, n_pages = page_table.shape
    pg = SAMPLE_PAGES_PER_STEP
    nblk = n_pages * PAGE_SIZE // MOBA_BLOCK
    page_block = (1, N_HEADS, HEAD_DIM, PAGE_SIZE)

    def page_spec(e):
        return pl.BlockSpec(page_block, lambda i, j, pt: (pt[i * n_pages + j * pg + e], 0, 0, 0))

    per_seq = pl.BlockSpec(page_block, lambda i, j, pt: (i, 0, 0, 0))
    grid_spec = pltpu.PrefetchScalarGridSpec(
        num_scalar_prefetch=1,
        grid=(n, n_pages // pg),
        in_specs=[per_seq, per_seq] + [page_spec(e) for e in range(pg)],
        out_specs=[pl.BlockSpec((1, n_pages, N_HEADS, 1, LANES), lambda i, j, pt: (i, 0, 0, 0, 0)),
                   pl.BlockSpec((1, N_HEADS, 1, LANES), lambda i, j, pt: (i, 0, 0, 0)),
                   pl.BlockSpec((1, MOBA_TOPK, N_HEADS, 1, LANES), lambda i, j, pt: (i, 0, 0, 0, 0))],
        scratch_shapes=[pltpu.VMEM((n_pages, N_HEADS, 1, LANES), F32), pltpu.VMEM((nblk, N_HEADS, 1, LANES), F32)],
    )
    return pl.pallas_call(
        _moba_sample_score_kernel,
        grid_spec=grid_spec,
        out_shape=[jax.ShapeDtypeStruct((n, n_pages, N_HEADS, 1, LANES), F32),
                   jax.ShapeDtypeStruct((n, N_HEADS, 1, LANES), F32),
                   jax.ShapeDtypeStruct((n, MOBA_TOPK, N_HEADS, 1, LANES), jnp.int32)],
        compiler_params=_cparams("parallel", "arbitrary"),
    )(page_table.reshape(-1), q_lanes, knew_lanes, *([cache_kt] * pg))


def _moba_sample_value_kernel(phys_ref, logi_ref, *refs):
    del phys_ref, logi_ref
    npg = MOBA_TOPK * MOBA_BLOCK // PAGE_SIZE
    p_refs, v_refs = refs[:npg], refs[npg:2 * npg]
    pself_ref, vnew_ref, o_ref = refs[2 * npg:]
    h = pl.program_id(1)
    rnd = lambda z: z.astype(MXU_DT).astype(F32)
    acc = rnd(pself_ref[0, h][:, :HEAD_DIM]) * rnd(vnew_ref[0, 0])
    for e in range(npg):
        p_rows = jnp.broadcast_to(p_refs[e][0, 0, h], (N_HEADS, PAGE_SIZE))
        acc = acc + _dg(p_rows, v_refs[e][0, 0], _NT)[0:1]
    o_ref[0, 0] = acc


def _moba_sample_values(p, p_self, v_new, cache_vt, phys, logi):
    n = p.shape[0]
    npg = MOBA_TOPK * MOBA_BLOCK // PAGE_SIZE

    def slot(i, h, e):
        return (i * N_HEADS + h) * npg + e

    p_specs = [pl.BlockSpec((1, 1, N_HEADS, 1, LANES),
                            functools.partial(lambda i, h, ph, lg, e: (i, lg[slot(i, h, e)], 0, 0, 0), e=e))
               for e in range(npg)]
    v_specs = [pl.BlockSpec((1, 1, HEAD_DIM, PAGE_SIZE),
                            functools.partial(lambda i, h, ph, lg, e: (ph[slot(i, h, e)], h, 0, 0), e=e))
               for e in range(npg)]
    head_row = pl.BlockSpec((1, 1, 1, HEAD_DIM), lambda i, h, ph, lg: (i, h, 0, 0))
    grid_spec = pltpu.PrefetchScalarGridSpec(
        num_scalar_prefetch=2,
        grid=(n, N_HEADS),
        in_specs=p_specs + v_specs + [pl.BlockSpec((1, N_HEADS, 1, LANES), lambda i, h, ph, lg: (i, 0, 0, 0)),
                                      head_row],
        out_specs=head_row,
    )
    return pl.pallas_call(
        _moba_sample_value_kernel,
        grid_spec=grid_spec,
        out_shape=jax.ShapeDtypeStruct((n, N_HEADS, 1, HEAD_DIM), F32),
        compiler_params=_cparams("parallel", "arbitrary"),
    )(phys, logi, *([p] * npg), *([cache_vt] * npg), p_self, v_new.reshape(n, N_HEADS, 1, HEAD_DIM))


def _moba_sample(q, k_new, v_new, cache_kt, cache_vt, page_table):
    n = q.shape[0]
    lanes = lambda z: jnp.broadcast_to(z.astype(F32).reshape(n, N_HEADS, HEAD_DIM, 1),
                                       (n, N_HEADS, HEAD_DIM, PAGE_SIZE))
    p, p_self, idx = _moba_sample_scores(lanes(q), lanes(k_new), cache_kt, page_table)
    pages_per_block = MOBA_BLOCK // PAGE_SIZE
    blocks = jnp.transpose(idx[..., 0, 0], (0, 2, 1))
    logi = (blocks[..., None] * pages_per_block + jnp.arange(pages_per_block, dtype=jnp.int32)).reshape(
        n, N_HEADS, -1)
    phys = jnp.take_along_axis(page_table[:, None, :], logi, axis=2)
    out = _moba_sample_values(p, p_self, v_new, cache_vt, phys.reshape(-1), logi.reshape(-1))
    return out.reshape(n, WIDTH)


def _merge_kernel(x_ref, r_ref, a_ref, gr_ref, gm_ref, gt_ref, wr_ref, wm_ref, wo_ref, lg_ref, lb_ref, o_ref):
    br = jax.nn.sigmoid(gr_ref[...]) * _dot(r_ref[...], wr_ref[...])
    bm = jax.nn.sigmoid(gm_ref[...]) * _dot(a_ref[...], wm_ref[...])
    merged = _dot(br + bm, wo_ref[...])
    o_ref[...] = _layer_norm(DN_ALPHA * x_ref[...] + gt_ref[...] * merged, lg_ref[...], lb_ref[...])


def _merge(x, r_out, att, g_r, g_m, gate, wr, wm, wo, ln_g, ln_b, tm):
    t = x.shape[0]
    return pl.pallas_call(
        _merge_kernel,
        grid=(t // tm,),
        in_specs=[_row_spec(tm, D_MODEL), _row_spec(tm, WIDTH), _row_spec(tm, WIDTH), _row_spec(tm, D_MODEL),
                  _row_spec(tm, D_MODEL), _mod_spec(gate, tm),
                  _const_spec((WIDTH, D_MODEL)), _const_spec((WIDTH, D_MODEL)), _const_spec((D_MODEL, D_MODEL)),
                  _const_spec((1, D_MODEL)), _const_spec((1, D_MODEL))],
        out_specs=_row_spec(tm, D_MODEL),
        out_shape=jax.ShapeDtypeStruct((t, D_MODEL), F32),
        compiler_params=_cparams("parallel"),
    )(x, r_out, att, g_r, g_m, gate, wr, wm, wo, ln_g, ln_b)


def _tile(t, want):
    return want if t % want == 0 else t


def _layer(x, mod, pos, wts, rp, rwkv_fn, moba_fn, prompt):
    t = x.shape[0]
    tm = _tile(t, 512)
    row = lambda z: z.reshape(1, -1)
    x1 = _ffn(x, mod[0], mod[1], mod[2], wts["ffn1_gate"], wts["ffn1_up"], wts["ffn1_down"],
              row(wts["ln1_g"]), row(wts["ln1_b"]), tm)
    cos, s_up, s_dn = _rope_tables(pos)
    proj = _inproj(x1, mod[3], mod[4], wts["w_in"], cos, s_up, s_dn, tm, prompt)
    pr, q, k_new, v_new, g_r, g_m, kb = proj[:7]
    r_out, wkv_new, shift_new = rwkv_fn(pr)
    att = moba_fn(q, k_new, v_new, kb, proj[7:])
    x2 = _merge(x1, r_out, att, g_r, g_m, mod[5], wts["w_br_rwkv"], wts["w_br_moba"], wts["w_out"],
                row(wts["ln2_g"]), row(wts["ln2_b"]), tm)
    y = _ffn(x2, mod[6], mod[7], mod[8], wts["ffn2_gate"], wts["ffn2_up"], wts["ffn2_down"],
             row(wts["ln3_g"]), row(wts["ln3_b"]), tm)
    return y, k_new, v_new, wkv_new, shift_new


def kernel(x_prompt, x_sample, cache_k, cache_v, state_rwkv_wkv, state_rwkv_shift, page_table, c_prompt, c_sample, w_ada, b_ada, ffn1_gate, ffn1_up, ffn1_down, ln1_g, ln1_b, w_in, mu_shift, rwkv_w0, rwkv_w2, rwkv_a0, rwkv_a2, rwkv_g2, rwkv_k_k, rwkv_k_a, rwkv_r_k, rwkv_lnx_g, rwkv_lnx_b, w_br_rwkv, w_br_moba, w_out, ln2_g, ln2_b, ffn2_gate, ffn2_up, ffn2_down, ln3_g, ln3_b):
    assert x_prompt.shape[0] == 1 and x_sample.shape[1] == 1 and w_ada.shape[0] == 1
    tp = x_prompt.shape[1]
    ns = x_sample.shape[0]
    n_pages = page_table.shape[1]
    past_len = n_pages * PAGE_SIZE
    assert tp % MOBA_BLOCK == 0 and past_len % MOBA_BLOCK == 0 and n_pages % SAMPLE_PAGES_PER_STEP == 0

    bf = lambda z: z[0].astype(MXU_DT)
    wts = {"ffn1_gate": bf(ffn1_gate), "ffn1_up": bf(ffn1_up), "ffn1_down": bf(ffn1_down),
           "ffn2_gate": bf(ffn2_gate), "ffn2_up": bf(ffn2_up), "ffn2_down": bf(ffn2_down),
           "w_in": bf(w_in), "w_br_rwkv": bf(w_br_rwkv), "w_br_moba": bf(w_br_moba), "w_out": bf(w_out),
           "ln1_g": ln1_g[0], "ln1_b": ln1_b[0], "ln2_g": ln2_g[0], "ln2_b": ln2_b[0],
           "ln3_g": ln3_g[0], "ln3_b": ln3_b[0]}
    zw = jnp.zeros((LORA_W, WIDTH), MXU_DT)
    head_of = jnp.arange(WIDTH) // HEAD_DIM
    row = lambda z: z.reshape(1, -1)
    rp = {"mu": row(mu_shift[0]),
          "wcat": jnp.concatenate([jnp.concatenate([bf(rwkv_w2), zw], axis=1),
                                   jnp.concatenate([zw, bf(rwkv_a2)], axis=1)], axis=0),
          "w0": row(rwkv_w0[0]), "a0": row(rwkv_a0[0]), "g2": bf(rwkv_g2),
          "k_k": row(rwkv_k_k[0]), "k_a": row(rwkv_k_a[0]), "r_k": row(rwkv_r_k[0]),
          "lnx_g": row(rwkv_lnx_g[0]), "lnx_b": row(rwkv_lnx_b[0]),
          "ones": (head_of[:, None] == head_of[None, :]).astype(MXU_DT)}

    n_mod = ((1 + ns + 7) // 8) * 8
    c_all = jnp.concatenate([c_prompt, c_sample, jnp.zeros((n_mod - 1 - ns, D_MODEL), F32)], axis=0)
    ada = _ada(c_all, w_ada[0], b_ada)
    mod_p = [ada[0:1, j * D_MODEL:(j + 1) * D_MODEL] for j in range(9)]
    mod_s = [ada[1:1 + ns, j * D_MODEL:(j + 1) * D_MODEL] for j in range(9)]

    def rwkv_p(pr):
        r_out, s_fin, shift = _rwkv_prompt(pr, jnp.zeros((1, R_PROJ), F32),
                                           jnp.zeros((N_HEADS, HEAD_DIM, HEAD_DIM), F32), rp, _tile(tp, 256))
        return r_out, s_fin, shift

    def moba_p(q, k_new, v_new, kb, extra):
        vt, kmean = extra
        return _moba_prompt(q, kmean.reshape(-1, WIDTH).astype(MXU_DT), kb, vt)

    y_p, k_p, v_p, wkv_p, shift_p = _layer(x_prompt[0], mod_p, jnp.arange(tp, dtype=jnp.int32), wts, rp,
                                           rwkv_p, moba_p, True)

    ck, cv = jnp.transpose(cache_k[0], (0, 2, 3, 1)), jnp.transpose(cache_v[0], (0, 2, 3, 1))

    def rwkv_s(pr):
        s_rows = jnp.transpose(state_rwkv_wkv[0], (0, 2, 1, 3)).reshape(ns * HEAD_DIM, WIDTH)
        r_out, s_new = _rwkv_step(pr, state_rwkv_shift[0], s_rows, rp, _tile(ns, 8))
        s_new = jnp.transpose(s_new.reshape(ns, HEAD_DIM, N_HEADS, HEAD_DIM), (0, 2, 1, 3))
        return r_out, s_new, pr

    def moba_s(q, k_new, v_new, kb, extra):
        return _moba_sample(q, k_new, v_new, ck, cv, page_table)

    y_s, k_s, v_s, wkv_s, shift_s = _layer(x_sample[:, 0], mod_s, jnp.full((ns,), past_len, jnp.int32), wts, rp,
                                           rwkv_s, moba_s, False)

    heads = lambda z, n, t: z.reshape(1, n, t, N_HEADS, HEAD_DIM)
    return (y_p[None], y_s[:, None],
            heads(k_p, 1, tp), heads(v_p, 1, tp), heads(k_s, ns, 1), heads(v_s, ns, 1),
            wkv_p[None, None], wkv_s[None], shift_p[None], shift_s[None])
```

```python
import functools

import jax
import jax.numpy as jnp
from jax import lax
from jax.experimental import pallas as pl
from jax.experimental.pallas import tpu as pltpu

F32 = jnp.float32
MXU_DT = jnp.bfloat16

D_MODEL = 1024
PAGE_SIZE = 128
N_HEADS = 8
HEAD_DIM = 64
WIDTH = N_HEADS * HEAD_DIM
LORA_W = 64
LORA_A = 64
LORA_G = 128
R_PROJ = 3 * WIDTH + LORA_W + LORA_A + LORA_G
GN_EPS = 64e-5
MOBA_BLOCK = 256
MOBA_TOPK = 3
ROT_DIM = HEAD_DIM // 4
ROPE_THETA = 500000.0
IN_PROJ = R_PROJ + 3 * WIDTH + 2 * D_MODEL
D_FF = 2816
LN_EPS = 1e-5
DN_ALPHA = 2.0 ** 0.25
CHUNK = 64
LANES = 128
VMEM_LIMIT = 56 * 1024 * 1024
SAMPLE_PAGES_PER_STEP = 16
VT_ROWS = HEAD_DIM + 16

NEG_INF = float("-inf")
LOG2_E = 1.4426950408889634


def _cparams(*sem, vmem=VMEM_LIMIT):
    return pltpu.CompilerParams(dimension_semantics=sem, vmem_limit_bytes=vmem)


def _const_spec(shape):
    return pl.BlockSpec(shape, lambda *_: (0,) * len(shape), pipeline_mode=pl.Buffered(1))


def _row_spec(tm, width):
    return pl.BlockSpec((tm, width), lambda i: (i, 0))


def _mod_spec(arr, tm):
    if arr.shape[0] == 1:
        return pl.BlockSpec((1, arr.shape[1]), lambda i: (0, 0))
    return pl.BlockSpec((tm, arr.shape[1]), lambda i: (i, 0))


def _dot(a, b):
    return jnp.dot(a.astype(MXU_DT), b.astype(MXU_DT), preferred_element_type=F32)


_NN = (((1,), (0,)), ((), ()))
_NT = (((1,), (1,)), ((), ()))
_TN = (((0,), (0,)), ((), ()))


def _dg(a, b, dn):
    return lax.dot_general(a.astype(MXU_DT), b.astype(MXU_DT), dn, preferred_element_type=F32)


def _split(x):
    hi = x.astype(MXU_DT)
    lo = (x - hi.astype(F32)).astype(MXU_DT)
    return hi, lo


def _seg_sum(x, ones):
    hi, lo = _split(x)
    return (jnp.dot(hi, ones, preferred_element_type=F32)
            + jnp.dot(lo, ones, preferred_element_type=F32))


def _layer_norm(y, g, b):
    mu = jnp.mean(y, axis=-1, keepdims=True)
    yc = y - mu
    var = jnp.mean(yc * yc, axis=-1, keepdims=True)
    return yc * lax.rsqrt(var + LN_EPS) * g + b


def _ada_kernel(c_ref, w_ref, b_ref, o_ref):
    c = c_ref[...]
    o_ref[...] = _dot(c * jax.nn.sigmoid(c), w_ref[...]) + b_ref[...]


def _ada(c, w_ada, b_ada):
    n, d = c.shape
    nout = w_ada.shape[1]
    tn = 1024
    return pl.pallas_call(
        _ada_kernel,
        grid=(nout // tn,),
        in_specs=[pl.BlockSpec((n, d), lambda j: (0, 0)),
                  pl.BlockSpec((d, tn), lambda j: (0, j)),
                  pl.BlockSpec((1, tn), lambda j: (0, j))],
        out_specs=pl.BlockSpec((n, tn), lambda j: (0, j)),
        out_shape=jax.ShapeDtypeStruct((n, nout), F32),
        compiler_params=_cparams("parallel"),
    )(c, w_ada, b_ada)


def _ffn_kernel(x_ref, sh_ref, sc_ref, gt_ref, wg_ref, wu_ref, wd_ref, lg_ref, lb_ref, o_ref, *, n_ff):
    x = x_ref[...]
    h = (x * (1.0 + sc_ref[...]) + sh_ref[...]).astype(MXU_DT)
    step = D_FF // n_ff
    f = None
    for j in range(n_ff):
        cs = slice(j * step, (j + 1) * step)
        g = jnp.dot(h, wg_ref[:, cs], preferred_element_type=F32)
        u = jnp.dot(h, wu_ref[:, cs], preferred_element_type=F32)
        a = (g * jax.nn.sigmoid(g) * u).astype(MXU_DT)
        part = jnp.dot(a, wd_ref[cs, :], preferred_element_type=F32)
        f = part if f is None else f + part
    y = DN_ALPHA * x + (0.5 * gt_ref[...]) * f
    o_ref[...] = _layer_norm(y, lg_ref[...], lb_ref[...])


def _ffn(x, shift, scale, gate, wg, wu, wd, ln_g, ln_b, tm):
    t = x.shape[0]
    return pl.pallas_call(
        functools.partial(_ffn_kernel, n_ff=2),
        grid=(t // tm,),
        in_specs=[_row_spec(tm, D_MODEL), _mod_spec(shift, tm), _mod_spec(scale, tm), _mod_spec(gate, tm),
                  _const_spec((D_MODEL, D_FF)), _const_spec((D_MODEL, D_FF)), _const_spec((D_FF, D_MODEL)),
                  _const_spec((1, D_MODEL)), _const_spec((1, D_MODEL))],
        out_specs=_row_spec(tm, D_MODEL),
        out_shape=jax.ShapeDtypeStruct((t, D_MODEL), F32),
        compiler_params=_cparams("parallel"),
    )(x, shift, scale, gate, wg, wu, wd, ln_g, ln_b)


def _rotate(x, cos, s_up, s_dn):
    return x * cos + pltpu.roll(x, LANES - ROT_DIM // 2, 1) * s_up + pltpu.roll(x, ROT_DIM // 2, 1) * s_dn


def _inproj_kernel(x_ref, sh_ref, sc_ref, w_ref, cos_ref, sup_ref, sdn_ref,
                   pr_ref, q_ref, k_ref, v_ref, gr_ref, gm_ref, kb_ref, vt_ref, km_ref, *, q_scale):
    h = (x_ref[...] * (1.0 + sc_ref[...]) + sh_ref[...]).astype(MXU_DT)
    tm = h.shape[0]

    def proj(lo, width):
        return jnp.dot(h, w_ref[:, lo:lo + width], preferred_element_type=F32)

    pr_ref[...] = proj(0, R_PROJ)
    cos, s_up, s_dn = cos_ref[...], sup_ref[...], sdn_ref[...]
    q = proj(R_PROJ, WIDTH)
    k = proj(R_PROJ + WIDTH, WIDTH)
    for j in range(WIDTH // LANES):
        ls = slice(j * LANES, (j + 1) * LANES)
        qj = _rotate(q[:, ls], cos, s_up, s_dn)
        kj = _rotate(k[:, ls], cos, s_up, s_dn)
        q_ref[:, ls] = (qj * q_scale).astype(MXU_DT)
        k_ref[:, ls] = kj
        kb_ref[:, ls] = kj.astype(MXU_DT)
        if km_ref is not None:
            km_ref[0, :, ls] = jnp.sum(kj.reshape(tm // MOBA_BLOCK, MOBA_BLOCK, LANES), axis=1) * (1.0 / MOBA_BLOCK)
    v = proj(R_PROJ + 2 * WIDTH, WIDTH)
    v_ref[...] = v
    if vt_ref is not None:
        v_t = v.T
        for head in range(N_HEADS):
            base = head * VT_ROWS
            vt_ref[base:base + HEAD_DIM, :] = v_t[head * HEAD_DIM:(head + 1) * HEAD_DIM, :].astype(MXU_DT)
            vt_ref[base + HEAD_DIM:base + VT_ROWS, :] = jnp.ones((VT_ROWS - HEAD_DIM, tm), MXU_DT)
    gr_ref[...] = proj(R_PROJ + 3 * WIDTH, D_MODEL)
    gm_ref[...] = proj(R_PROJ + 3 * WIDTH + D_MODEL, D_MODEL)


def _inproj_prompt_kernel(*refs):
    _inproj_kernel(*refs, q_scale=HEAD_DIM ** -0.5 * LOG2_E)


def _inproj_sample_kernel(*refs):
    _inproj_kernel(*refs, None, None, q_scale=HEAD_DIM ** -0.5)


def _inproj(x, shift, scale, w_in, cos, s_up, s_dn, tm, prompt):
    t = x.shape[0]
    out_shape = [jax.ShapeDtypeStruct((t, R_PROJ), F32),
                 jax.ShapeDtypeStruct((t, WIDTH), MXU_DT),
                 jax.ShapeDtypeStruct((t, WIDTH), F32),
                 jax.ShapeDtypeStruct((t, WIDTH), F32),
                 jax.ShapeDtypeStruct((t, D_MODEL), F32),
                 jax.ShapeDtypeStruct((t, D_MODEL), F32),
                 jax.ShapeDtypeStruct((t, WIDTH), MXU_DT)]
    out_specs = [_row_spec(tm, R_PROJ), _row_spec(tm, WIDTH), _row_spec(tm, WIDTH), _row_spec(tm, WIDTH),
                 _row_spec(tm, D_MODEL), _row_spec(tm, D_MODEL), _row_spec(tm, WIDTH)]
    if prompt:
        nblk = tm // MOBA_BLOCK
        vt_rows = N_HEADS * VT_ROWS
        out_shape += [jax.ShapeDtypeStruct((vt_rows, t), MXU_DT),
                      jax.ShapeDtypeStruct((t // tm, nblk, WIDTH), F32)]
        out_specs += [pl.BlockSpec((vt_rows, tm), lambda i: (0, i)),
                      pl.BlockSpec((1, nblk, WIDTH), lambda i: (i, 0, 0))]
    return pl.pallas_call(
        _inproj_prompt_kernel if prompt else _inproj_sample_kernel,
        grid=(t // tm,),
        in_specs=[_row_spec(tm, D_MODEL), _mod_spec(shift, tm), _mod_spec(scale, tm),
                  _const_spec((D_MODEL, IN_PROJ)),
                  _row_spec(tm, LANES), _row_spec(tm, LANES), _row_spec(tm, LANES)],
        out_specs=out_specs,
        out_shape=out_shape,
        compiler_params=_cparams("parallel"),
    )(x, shift, scale, w_in, cos, s_up, s_dn)


def _rope_tables(pos):
    half = ROT_DIM // 2
    inv = ROPE_THETA ** (-jnp.arange(half, dtype=F32) * 2.0 / ROT_DIM)
    ang = pos.astype(F32)[:, None] * inv[None, :]
    cos, sin = jnp.cos(ang), jnp.sin(ang)
    t = pos.shape[0]
    zeros = lambda n: jnp.zeros((t, n), F32)
    per_head = lambda parts: jnp.tile(jnp.concatenate(parts, axis=1), (1, LANES // HEAD_DIM))
    c = per_head([cos, cos, jnp.ones((t, HEAD_DIM - ROT_DIM), F32)])
    s_up = per_head([-sin, zeros(HEAD_DIM - half)])
    s_dn = per_head([zeros(half), sin, zeros(HEAD_DIM - ROT_DIM)])
    return c, s_up, s_dn


def _rwkv_pointwise(pr, prev, mu, wcat, w0, a0, g2, k_k, k_a, ones):
    xs = pr + (prev - pr) * mu
    r = xs[:, 0:WIDTH]
    k = xs[:, WIDTH:2 * WIDTH]
    v = xs[:, 2 * WIDTH:3 * WIDTH]
    la = xs[:, 3 * WIDTH:3 * WIDTH + LORA_W + LORA_A]
    gl = xs[:, 3 * WIDTH + LORA_W + LORA_A:R_PROJ]
    lane = lax.broadcasted_iota(jnp.int32, la.shape, 1)
    la = jnp.where(lane < LORA_W, jnp.tanh(la), la)
    wa = _dot(la, wcat)
    w = w0 + wa[:, :WIDTH]
    a = jax.nn.sigmoid(a0 + wa[:, WIDTH:])
    g = _dot(jax.nn.sigmoid(gl), g2)
    z = -w
    softplus = jnp.maximum(z, 0.0) + jnp.log1p(jnp.exp(-jnp.abs(z)))
    log_decay = -jnp.exp(-softplus - 0.5)
    kk = k * k_k
    kk = kk / jnp.maximum(jnp.sqrt(_seg_sum(kk * kk, ones)), 1e-12)
    k_eff = k * (1.0 + (a - 1.0) * k_a)
    return r, k_eff, v, kk, kk * a, log_decay, g


def _rwkv_post(y, r, k_eff, v, g, r_k, lnx_g, lnx_b, ones):
    inv = 1.0 / HEAD_DIM
    yc = y - _seg_sum(y, ones) * inv
    yv = _seg_sum(yc * yc, ones) * inv
    yn = yc * lax.rsqrt(yv + GN_EPS) * lnx_g + lnx_b
    bonus = _seg_sum(r * k_eff * r_k, ones) * v
    return (yn + bonus) * g


def _rwkv_prompt_kernel(pr_ref, shift0_ref, s0_ref, mu_ref, wcat_ref, w0_ref, a0_ref, g2_ref, kk_ref, ka_ref,
                        rk_ref, lg_ref, lb_ref, ones_ref, tri_ref, blk_ref,
                        out_ref, sfin_ref, shift_ref,
                        prev_scr, st_scr, rt_scr, kp_scr, bt_scr, kt_scr, v_scr, bc_scr, kc_scr, pc_scr, y_scr):
    i = pl.program_id(0)

    @pl.when(i == 0)
    def _():
        prev_scr[...] = shift0_ref[...]
        st_scr[...] = s0_ref[...]

    pr = pr_ref[...]
    tm = pr.shape[0]
    row = lax.broadcasted_iota(jnp.int32, (tm, 1), 0)
    prev = jnp.where(row == 0, prev_scr[...], pltpu.roll(pr, 1, 0))
    ones = ones_ref[...]
    r, k_eff, v, kk, b, lw, g = _rwkv_pointwise(pr, prev, mu_ref[...], wcat_ref[...], w0_ref[...], a0_ref[...],
                                                g2_ref[...], kk_ref[...], ka_ref[...], ones)
    l_hi = lw.astype(MXU_DT)
    rem = lw - l_hi.astype(F32)
    l_mid = rem.astype(MXU_DT)
    l_lo = (rem - l_mid.astype(F32)).astype(MXU_DT)
    cum = lambda m: (jnp.dot(m, l_hi, preferred_element_type=F32)
                     + (jnp.dot(m, l_mid, preferred_element_type=F32) + jnp.dot(m, l_lo, preferred_element_type=F32)))
    L = cum(tri_ref[...])
    LC = cum(blk_ref[...])
    e_inv = jnp.exp(-L)
    e_end = jnp.exp(LC - L)
    rt_scr[...] = r * jnp.exp(L)
    kp_scr[...] = kk * jnp.exp(L - lw)
    bt_scr[...] = b * e_inv
    kt_scr[...] = k_eff * e_inv
    bc_scr[...] = b * e_end
    kc_scr[...] = k_eff * e_end
    pc_scr[...] = jnp.exp(LC)
    v_scr[...] = v

    ri = lax.broadcasted_iota(jnp.int32, (CHUNK, CHUNK), 0)
    ci = lax.broadcasted_iota(jnp.int32, (CHUNK, CHUNK), 1)
    strict, incl, eye = ri > ci, ri >= ci, ri == ci

    n_chunks = tm // CHUNK
    probs = [(c, h) for c in range(n_chunks) for h in range(N_HEADS)]
    each = lambda fn, *lists: [fn(*args) for args in zip(*lists)]

    def blocks(ref):
        return [ref[c * CHUNK:(c + 1) * CHUNK, h * HEAD_DIM:(h + 1) * HEAD_DIM] for c, h in probs]

    def blocks_t(ref):
        per_chunk = [ref[c * CHUNK:(c + 1) * CHUNK, :].T for c in range(n_chunks)]
        return [per_chunk[c][h * HEAD_DIM:(h + 1) * HEAD_DIM, :] for c, h in probs]

    kp, rt, v_b = blocks(kp_scr), blocks(rt_scr), blocks(v_scr)
    bk = each(lambda a, b_: jnp.concatenate([a, b_], axis=0), blocks(bt_scr), blocks(kt_scr))
    bk = each(lambda b_: b_.astype(MXU_DT), bk)
    v_m = each(lambda a: a.astype(MXU_DT), v_b)
    akk = each(lambda a, b_: _dg(a, b_, _NT), kp, bk)
    arr = each(lambda a, b_: _dg(a, b_, _NT), rt, bk)
    a_zb = [jnp.where(strict, a[:, :CHUNK], 0.0) for a in akk]
    a_vk = [jnp.where(strict, a[:, CHUNK:], 0.0) for a in akk]
    a_rb = [jnp.where(incl, a[:, :CHUNK], 0.0) for a in arr]
    a_rk = [jnp.where(incl, a[:, CHUNK:], 0.0) for a in arr]
    u0 = each(lambda a, w, v_: jnp.concatenate([a, _dg(w, v_, _NN)], axis=1), kp, a_vk, v_m)
    half = lambda s: (ri // (2 * s) == ci // (2 * s)) & (ri % (2 * s) >= s) & (ci % (2 * s) < s)
    t_inv = [jnp.where(eye, 1.0, 0.0) - jnp.where(half(1), a, 0.0) for a in a_zb]
    s_blk = 2
    while s_blk < CHUNK:
        lower = half(s_blk)
        t_inv = each(lambda t_, a: t_ - _dg(_dg(t_, jnp.where(lower, a, 0.0), _NN), t_, _NN), t_inv, a_zb)
        s_blk *= 2
    u = each(lambda t_, u_: _dg(t_, u_, _NN), t_inv, u0)
    u_m = each(lambda a: a.astype(MXU_DT), u)
    gmat = each(lambda a, u_: _dg(a, u_, _NN), a_rb, u_m)
    r_eff = each(lambda a, g_: a - g_[:, :HEAD_DIM], rt, gmat)
    y0 = each(lambda a, b_, g_: _dg(a, b_, _NN) - g_[:, HEAD_DIM:], a_rk, v_m, gmat)
    bu = each(lambda a, u_: _dg(a, u_, _NN), blocks_t(bc_scr), u_m)
    pc = [pc_scr[c * CHUNK:c * CHUNK + 1, h * HEAD_DIM:(h + 1) * HEAD_DIM] for c, h in probs]
    m_t = each(lambda p_, b_: jnp.where(eye, p_, 0.0) - b_[:, :HEAD_DIM], pc, bu)
    n_t = each(lambda a, v_, c_: _dg(a, v_, _NN) - c_[:, HEAD_DIM:], blocks_t(kc_scr), v_m, bu)
    st = [st_scr[h] for h in range(N_HEADS)]
    for c in range(n_chunks):
        sl = slice(c * N_HEADS, (c + 1) * N_HEADS)
        st_m = each(lambda a: a.astype(MXU_DT), st)
        y_scr[c * CHUNK:(c + 1) * CHUNK, :] = jnp.concatenate(
            each(lambda a, s_, b_: _dg(a, s_, _NN) + b_, r_eff[sl], st_m, y0[sl]), axis=1)
        st = each(lambda m_, s_, n_: _dg(m_, s_, _NN) + n_, m_t[sl], st_m, n_t[sl])
    for h in range(N_HEADS):
        st_scr[h] = st[h]
    out_ref[...] = _rwkv_post(y_scr[...], r, k_eff, v, g, rk_ref[...], lg_ref[...], lb_ref[...], ones)
    prev_scr[...] = pr[tm - 1:tm, :]
    shift_ref[...] = pr[tm - 1:tm, :]

    @pl.when(i == pl.num_programs(0) - 1)
    def _():
        for h in range(N_HEADS):
            sfin_ref[h] = st[h].T


def _rwkv_prompt(pr, shift0, s0, rp, tm):
    t = pr.shape[0]
    ch = lax.broadcasted_iota(jnp.int32, (tm, tm), 0) // CHUNK == lax.broadcasted_iota(jnp.int32, (tm, tm), 1) // CHUNK
    low = lax.broadcasted_iota(jnp.int32, (tm, tm), 0) >= lax.broadcasted_iota(jnp.int32, (tm, tm), 1)
    tri = (ch & low).astype(MXU_DT)
    blk = ch.astype(MXU_DT)
    vec = _const_spec((1, WIDTH))
    tile = pltpu.VMEM((tm, WIDTH), F32)
    return pl.pallas_call(
        _rwkv_prompt_kernel,
        grid=(t // tm,),
        in_specs=[_row_spec(tm, R_PROJ), _const_spec((1, R_PROJ)), _const_spec((N_HEADS, HEAD_DIM, HEAD_DIM)),
                  _const_spec((1, R_PROJ)), _const_spec((LORA_W + LORA_A, 2 * WIDTH)), vec, vec,
                  _const_spec((LORA_G, WIDTH)), vec, vec, vec, vec, vec,
                  _const_spec((WIDTH, WIDTH)), _const_spec((tm, tm)), _const_spec((tm, tm))],
        out_specs=[_row_spec(tm, WIDTH),
                   pl.BlockSpec((N_HEADS, HEAD_DIM, HEAD_DIM), lambda i: (0, 0, 0)),
                   pl.BlockSpec((1, R_PROJ), lambda i: (0, 0))],
        out_shape=[jax.ShapeDtypeStruct((t, WIDTH), F32),
                   jax.ShapeDtypeStruct((N_HEADS, HEAD_DIM, HEAD_DIM), F32),
                   jax.ShapeDtypeStruct((1, R_PROJ), F32)],
        scratch_shapes=[pltpu.VMEM((1, R_PROJ), F32), pltpu.VMEM((N_HEADS, HEAD_DIM, HEAD_DIM), F32)] + [tile] * 9,
        compiler_params=_cparams("arbitrary"),
    )(pr, shift0, jnp.swapaxes(s0, 1, 2), rp["mu"], rp["wcat"], rp["w0"], rp["a0"], rp["g2"], rp["k_k"], rp["k_a"],
      rp["r_k"], rp["lnx_g"], rp["lnx_b"], rp["ones"], tri, blk)


def _rwkv_step_kernel(pr_ref, shift_ref, s_ref, mu_ref, wcat_ref, w0_ref, a0_ref, g2_ref, kk_ref, ka_ref,
                      rk_ref, lg_ref, lb_ref, ones_ref, out_ref, snew_ref):
    ones = ones_ref[...]
    r, k_eff, v, kk, b, lw, g = _rwkv_pointwise(pr_ref[...], shift_ref[...], mu_ref[...], wcat_ref[...],
                                                w0_ref[...], a0_ref[...], g2_ref[...], kk_ref[...], ka_ref[...], ones)
    nb = r.shape[0]
    flat = lambda z: z.reshape(nb * HEAD_DIM, WIDTH)
    cube = lambda z: z.reshape(nb, HEAD_DIM, WIDTH)
    rows = lambda z: z[:, None, :]
    s = cube(s_ref[...])
    diag = (lax.broadcasted_iota(jnp.int32, (HEAD_DIM, WIDTH), 0)
            == lax.broadcasted_iota(jnp.int32, (HEAD_DIM, WIDTH), 1) % HEAD_DIM)[None]
    s_kk = cube(_seg_sum(flat(s * rows(kk)), ones))
    v_col = cube(_seg_sum(flat(jnp.where(diag, rows(v), 0.0)), ones))
    s_new = s * rows(jnp.exp(lw)) - s_kk * rows(b) + v_col * rows(k_eff)
    y_b = cube(_seg_sum(flat(s_new * rows(r)), ones))
    y = jnp.sum(jnp.where(diag, y_b, 0.0), axis=1)
    out_ref[...] = _rwkv_post(y, r, k_eff, v, g, rk_ref[...], lg_ref[...], lb_ref[...], ones)
    snew_ref[...] = flat(s_new)


def _rwkv_step(pr, shift, s_rows, rp, nb):
    n = pr.shape[0]
    vec = _const_spec((1, WIDTH))
    return pl.pallas_call(
        _rwkv_step_kernel,
        grid=(n // nb,),
        in_specs=[_row_spec(nb, R_PROJ), _row_spec(nb, R_PROJ), _row_spec(nb * HEAD_DIM, WIDTH),
                  _const_spec((1, R_PROJ)), _const_spec((LORA_W + LORA_A, 2 * WIDTH)), vec, vec,
                  _const_spec((LORA_G, WIDTH)), vec, vec, vec, vec, vec, _const_spec((WIDTH, WIDTH))],
        out_specs=[_row_spec(nb, WIDTH), _row_spec(nb * HEAD_DIM, WIDTH)],
        out_shape=[jax.ShapeDtypeStruct((n, WIDTH), F32), jax.ShapeDtypeStruct((n * HEAD_DIM, WIDTH), F32)],
        compiler_params=_cparams("parallel"),
    )(pr, shift, s_rows, rp["mu"], rp["wcat"], rp["w0"], rp["a0"], rp["g2"], rp["k_k"], rp["k_a"], rp["r_k"],
      rp["lnx_g"], rp["lnx_b"], rp["ones"])


def _moba_prompt_kernel(q_ref, km_ref, k_ref, vt_ref, o_ref, qm_scr, sel_scr, m_scr, acc_scr,
                        s0_scr, s1_scr, cm_scr):
    c = pl.program_id(0)
    tq = q_ref.shape[0]
    nblk = km_ref.shape[0]
    lane = lax.broadcasted_iota(jnp.int32, (1, LANES), 1)
    blk_id = lax.broadcasted_iota(jnp.int32, (nblk, tq), 0)
    key_i = lax.broadcasted_iota(jnp.int32, (MOBA_BLOCK, tq), 0)
    qry_i = lax.broadcasted_iota(jnp.int32, (MOBA_BLOCK, tq), 1)

    def head_lanes(h):
        return (lane < HEAD_DIM) if h % 2 == 0 else (lane >= HEAD_DIM)

    def values(h, b):
        rows = slice(h * VT_ROWS, (h + 1) * VT_ROWS)
        return vt_ref[rows, pl.ds(pl.multiple_of(b * MOBA_BLOCK, MOBA_BLOCK), MOBA_BLOCK)]

    def keys(h, b):
        ps = slice((h // 2) * LANES, (h // 2 + 1) * LANES)
        return k_ref[pl.ds(pl.multiple_of(b * MOBA_BLOCK, MOBA_BLOCK), MOBA_BLOCK), ps]

    for h in range(N_HEADS):
        ps = slice((h // 2) * LANES, (h // 2 + 1) * LANES)
        qm = jnp.where(head_lanes(h), q_ref[:, ps].astype(F32), 0.0).astype(MXU_DT)
        qm_scr[h] = qm
        gate = lax.dot_general(km_ref[:, ps], qm, _NT, preferred_element_type=F32)
        gate = jnp.where(blk_id < c, gate, NEG_INF)
        sel = jnp.zeros(gate.shape, jnp.bool_)
        for _ in range(MOBA_TOPK):
            top = jnp.max(gate, axis=0, keepdims=True)
            idx = jnp.min(jnp.where(gate == top, blk_id, nblk), axis=0, keepdims=True)
            hit = blk_id == idx
            sel = sel | hit
            gate = jnp.where(hit, NEG_INF, gate)
        sel_scr[h] = sel.astype(F32)
        s = lax.dot_general(keys(h, c), qm, _NT, preferred_element_type=F32)
        s = jnp.where(key_i <= qry_i, s, NEG_INF)
        m = jnp.max(s, axis=0, keepdims=True)
        p = jnp.exp2(s - m).astype(MXU_DT)
        m_scr[h] = m
        acc_scr[h] = jnp.dot(values(h, c), p, preferred_element_type=F32)

    def scores(b, s_scr, slot):
        for h in range(N_HEADS):
            s = lax.dot_general(keys(h, b), qm_scr[h], _NT, preferred_element_type=F32)
            s_scr[h] = s
            cm_scr[slot, h] = jnp.max(s, axis=0, keepdims=True)

    def absorb(b, s_scr, slot):
        for h in range(N_HEADS):
            picked = sel_scr[h, pl.ds(b, 1), :] > 0.0
            m_old = m_scr[h]
            m_new = jnp.where(picked, jnp.maximum(m_old, cm_scr[slot, h]), m_old)
            p = jnp.exp2(s_scr[h] - jnp.where(picked, m_new, jnp.inf)).astype(MXU_DT)
            acc_scr[h] = acc_scr[h] * jnp.exp2(m_old - m_new) + jnp.dot(values(h, b), p, preferred_element_type=F32)
            m_scr[h] = m_new

    def block_pair(i, carry):
        scores(2 * i, s0_scr, 0)
        scores(2 * i + 1, s1_scr, 1)
        absorb(2 * i, s0_scr, 0)
        absorb(2 * i + 1, s1_scr, 1)
        return carry

    lax.fori_loop(0, c // 2, block_pair, 0)

    @pl.when(c % 2 == 1)
    def _():
        scores(c - 1, s0_scr, 0)
        absorb(c - 1, s0_scr, 0)

    for pair in range(N_HEADS // 2):
        outs = []
        for h in (2 * pair, 2 * pair + 1):
            a = acc_scr[h]
            outs.append((a[:HEAD_DIM] / a[HEAD_DIM:HEAD_DIM + 1]).T)
        o_ref[:, pair * LANES:(pair + 1) * LANES] = jnp.concatenate(outs, axis=1)


def _moba_prompt(q, kmean, kb, vt):
    t = q.shape[0]
    nblk = kmean.shape[0]
    tq = MOBA_BLOCK
    return pl.pallas_call(
        _moba_prompt_kernel,
        grid=(t // tq,),
        in_specs=[_row_spec(tq, WIDTH), _const_spec((nblk, WIDTH)), _const_spec((t, WIDTH)),
                  _const_spec(vt.shape)],
        out_specs=_row_spec(tq, WIDTH),
        out_shape=jax.ShapeDtypeStruct((t, WIDTH), F32),
        scratch_shapes=[pltpu.VMEM((N_HEADS, tq, LANES), MXU_DT), pltpu.VMEM((N_HEADS, nblk, tq), F32),
                        pltpu.VMEM((N_HEADS, 1, tq), F32), pltpu.VMEM((N_HEADS, VT_ROWS, tq), F32),
                        pltpu.VMEM((N_HEADS, MOBA_BLOCK, tq), F32), pltpu.VMEM((N_HEADS, MOBA_BLOCK, tq), F32),
                        pltpu.VMEM((2, N_HEADS, 1, tq), F32)],
        compiler_params=_cparams("parallel"),
    )(q, kmean, kb, vt)


def _moba_sample_score_kernel(pt_ref, q_ref, knew_ref, *refs):
    del pt_ref
    pg = SAMPLE_PAGES_PER_STEP
    pages = refs[:pg]
    p_ref, pself_ref, idx_ref, sc_scr, gate_scr = refs[pg:]
    j = pl.program_id(1)
    nblk = gate_scr.shape[1]
    rnd = lambda z: z.astype(MXU_DT).astype(F32)
    q = q_ref[0]
    for t in range(pg // 2):
        halves = (pages[2 * t][0], pages[2 * t + 1][0])
        sc = jnp.concatenate([jnp.sum(page * q, axis=1, keepdims=True) for page in halves], axis=2)
        kmean = jnp.sum(halves[0] + halves[1], axis=2, keepdims=True) * (1.0 / MOBA_BLOCK)
        gate = jnp.broadcast_to(jnp.sum(q[:, :, :1] * rnd(kmean), axis=1, keepdims=True), (N_HEADS, 1, LANES))
        blk = pl.ds(j * (pg // 2) + t, 1)
        for h in range(N_HEADS):
            sc_scr[h, blk, :] = sc[h]
            gate_scr[h, blk, :] = gate[h]

    @pl.when(j == pl.num_programs(1) - 1)
    def _():
        gate = gate_scr[...]
        blk_id = lax.broadcasted_iota(jnp.int32, gate.shape, 1)
        sel = jnp.zeros(gate.shape, jnp.bool_)
        for r in range(MOBA_TOPK):
            top = jnp.max(gate, axis=1, keepdims=True)
            idx = jnp.min(jnp.where(gate == top, blk_id, nblk), axis=1, keepdims=True)
            hit = blk_id == idx
            sel = sel | hit
            gate = jnp.where(hit, NEG_INF, gate)
            idx_ref[0, r] = idx
        sel_f = jnp.where(sel, 1.0, 0.0)
        picked = jnp.concatenate([sel_f] * (MOBA_BLOCK // LANES), axis=2) > 0.0
        s = jnp.where(picked, sc_scr[...], NEG_INF)
        s_self = jnp.sum(q * rnd(knew_ref[0]), axis=1, keepdims=True)
        m = jnp.maximum(jnp.max(jnp.max(s, axis=2, keepdims=True), axis=1, keepdims=True), s_self)
        p = jnp.exp(s - m[:, :, :1])
        p_self = jnp.exp(s_self - m)
        inv = 1.0 / (jnp.sum(jnp.sum(p, axis=2, keepdims=True), axis=1, keepdims=True) + p_self)
        p_ref[0] = p * inv[:, :, :1]
        pself_ref[0] = p_self * inv


def _moba_sample_scores(q_lanes, knew_lanes, cache_kt, page_table):
    n, n_pages = page_table.shape
    pg = SAMPLE_PAGES_PER_STEP
    nblk = n_pages * PAGE_SIZE // MOBA_BLOCK
    page_block = (1, N_HEADS, HEAD_DIM, PAGE_SIZE)

    def page_spec(e):
        return pl.BlockSpec(page_block, lambda i, j, pt: (pt[i * n_pages + j * pg + e], 0, 0, 0))

    per_seq = pl.BlockSpec(page_block, lambda i, j, pt: (i, 0, 0, 0))
    grid_spec = pltpu.PrefetchScalarGridSpec(
        num_scalar_prefetch=1,
        grid=(n, n_pages // pg),
        in_specs=[per_seq, per_seq] + [page_spec(e) for e in range(pg)],
        out_specs=[pl.BlockSpec((1, N_HEADS, nblk, MOBA_BLOCK), lambda i, j, pt: (i, 0, 0, 0)),
                   pl.BlockSpec((1, N_HEADS, 1, LANES), lambda i, j, pt: (i, 0, 0, 0)),
                   pl.BlockSpec((1, MOBA_TOPK, N_HEADS, 1, LANES), lambda i, j, pt: (i, 0, 0, 0, 0))],
        scratch_shapes=[pltpu.VMEM((N_HEADS, nblk, MOBA_BLOCK), F32), pltpu.VMEM((N_HEADS, nblk, LANES), F32)],
    )
    return pl.pallas_call(
        _moba_sample_score_kernel,
        grid_spec=grid_spec,
        out_shape=[jax.ShapeDtypeStruct((n, N_HEADS, nblk, MOBA_BLOCK), F32),
                   jax.ShapeDtypeStruct((n, N_HEADS, 1, LANES), F32),
                   jax.ShapeDtypeStruct((n, MOBA_TOPK, N_HEADS, 1, LANES), jnp.int32)],
        compiler_params=_cparams("parallel", "arbitrary"),
    )(page_table.reshape(-1), q_lanes, knew_lanes, *([cache_kt] * pg))


def _moba_sample_value_kernel(phys_ref, blk_ref, p_ref, pself_ref, vnew_ref, *refs):
    del phys_ref
    pages_per_block = MOBA_BLOCK // PAGE_SIZE
    v_refs, o_ref = refs[:-1], refs[-1]
    i = pl.program_id(0)
    rnd = lambda z: z.astype(MXU_DT).astype(F32)
    for h in range(N_HEADS):
        acc = rnd(pself_ref[0, h][:, :HEAD_DIM]) * rnd(vnew_ref[0, h])
        for r in range(MOBA_TOPK):
            slot = h * MOBA_TOPK + r
            p_row = p_ref[0, h, pl.ds(blk_ref[i * N_HEADS * MOBA_TOPK + slot], 1), :]
            for e in range(pages_per_block):
                p_rows = jnp.broadcast_to(p_row[:, e * PAGE_SIZE:(e + 1) * PAGE_SIZE], (N_HEADS, PAGE_SIZE))
                v_t = v_refs[slot * pages_per_block + e][0, 0]
                acc = acc + _dg(p_rows, v_t, _NT)[0:1]
        o_ref[0, h] = acc


def _moba_sample_values(p, p_self, v_new, cache_vt, phys, blocks):
    n, _, nblk, _ = p.shape
    pages_per_block = MOBA_BLOCK // PAGE_SIZE
    per_seq = N_HEADS * MOBA_TOPK * pages_per_block

    def page_spec(h, e):
        return pl.BlockSpec((1, 1, HEAD_DIM, PAGE_SIZE),
                            lambda i, ph, bk: (ph[i * per_seq + h * MOBA_TOPK * pages_per_block + e], h, 0, 0))

    whole = lambda shape: pl.BlockSpec((1,) + shape, lambda i, ph, bk: (i,) + (0,) * len(shape))
    grid_spec = pltpu.PrefetchScalarGridSpec(
        num_scalar_prefetch=2,
        grid=(n,),
        in_specs=[whole((N_HEADS, nblk, MOBA_BLOCK)), whole((N_HEADS, 1, LANES)), whole((N_HEADS, 1, HEAD_DIM))]
        + [page_spec(h, e) for h in range(N_HEADS) for e in range(MOBA_TOPK * pages_per_block)],
        out_specs=whole((N_HEADS, 1, HEAD_DIM)),
    )
    return pl.pallas_call(
        _moba_sample_value_kernel,
        grid_spec=grid_spec,
        out_shape=jax.ShapeDtypeStruct((n, N_HEADS, 1, HEAD_DIM), F32),
        compiler_params=_cparams("parallel"),
    )(phys, blocks, p, p_self, v_new.reshape(n, N_HEADS, 1, HEAD_DIM), *([cache_vt] * per_seq))


def _moba_sample(q, k_new, v_new, cache_kt, cache_vt, page_table):
    n = q.shape[0]
    lanes = lambda z: jnp.broadcast_to(z.astype(F32).reshape(n, N_HEADS, HEAD_DIM, 1),
                                       (n, N_HEADS, HEAD_DIM, PAGE_SIZE))
    p, p_self, idx = _moba_sample_scores(lanes(q), lanes(k_new), cache_kt, page_table)
    pages_per_block = MOBA_BLOCK // PAGE_SIZE
    blocks = jnp.transpose(idx[..., 0, 0], (0, 2, 1))
    logi = (blocks[..., None] * pages_per_block + jnp.arange(pages_per_block, dtype=jnp.int32)).reshape(
        n, N_HEADS, -1)
    phys = jnp.take_along_axis(page_table[:, None, :], logi, axis=2)
    out = _moba_sample_values(p, p_self, v_new, cache_vt, phys.reshape(-1), blocks.reshape(-1))
    return out.reshape(n, WIDTH)


def _merge_kernel(x_ref, r_ref, a_ref, gr_ref, gm_ref, gt_ref, wr_ref, wm_ref, wo_ref, lg_ref, lb_ref, o_ref):
    br = jax.nn.sigmoid(gr_ref[...]) * _dot(r_ref[...], wr_ref[...])
    bm = jax.nn.sigmoid(gm_ref[...]) * _dot(a_ref[...], wm_ref[...])
    merged = _dot(br + bm, wo_ref[...])
    o_ref[...] = _layer_norm(DN_ALPHA * x_ref[...] + gt_ref[...] * merged, lg_ref[...], lb_ref[...])


def _merge(x, r_out, att, g_r, g_m, gate, wr, wm, wo, ln_g, ln_b, tm):
    t = x.shape[0]
    return pl.pallas_call(
        _merge_kernel,
        grid=(t // tm,),
        in_specs=[_row_spec(tm, D_MODEL), _row_spec(tm, WIDTH), _row_spec(tm, WIDTH), _row_spec(tm, D_MODEL),
                  _row_spec(tm, D_MODEL), _mod_spec(gate, tm),
                  _const_spec((WIDTH, D_MODEL)), _const_spec((WIDTH, D_MODEL)), _const_spec((D_MODEL, D_MODEL)),
                  _const_spec((1, D_MODEL)), _const_spec((1, D_MODEL))],
        out_specs=_row_spec(tm, D_MODEL),
        out_shape=jax.ShapeDtypeStruct((t, D_MODEL), F32),
        compiler_params=_cparams("parallel"),
    )(x, r_out, att, g_r, g_m, gate, wr, wm, wo, ln_g, ln_b)


def _tile(t, want):
    return want if t % want == 0 else t


def _layer(x, mod, pos, wts, rp, rwkv_fn, moba_fn, prompt):
    t = x.shape[0]
    tm = _tile(t, 512)
    row = lambda z: z.reshape(1, -1)
    x1 = _ffn(x, mod[0], mod[1], mod[2], wts["ffn1_gate"], wts["ffn1_up"], wts["ffn1_down"],
              row(wts["ln1_g"]), row(wts["ln1_b"]), tm)
    cos, s_up, s_dn = _rope_tables(pos)
    proj = _inproj(x1, mod[3], mod[4], wts["w_in"], cos, s_up, s_dn, tm, prompt)
    pr, q, k_new, v_new, g_r, g_m, kb = proj[:7]
    r_out, wkv_new, shift_new = rwkv_fn(pr)
    att = moba_fn(q, k_new, v_new, kb, proj[7:])
    x2 = _merge(x1, r_out, att, g_r, g_m, mod[5], wts["w_br_rwkv"], wts["w_br_moba"], wts["w_out"],
                row(wts["ln2_g"]), row(wts["ln2_b"]), tm)
    y = _ffn(x2, mod[6], mod[7], mod[8], wts["ffn2_gate"], wts["ffn2_up"], wts["ffn2_down"],
             row(wts["ln3_g"]), row(wts["ln3_b"]), tm)
    return y, k_new, v_new, wkv_new, shift_new


def kernel(x_prompt, x_sample, cache_k, cache_v, state_rwkv_wkv, state_rwkv_shift, page_table, c_prompt, c_sample, w_ada, b_ada, ffn1_gate, ffn1_up, ffn1_down, ln1_g, ln1_b, w_in, mu_shift, rwkv_w0, rwkv_w2, rwkv_a0, rwkv_a2, rwkv_g2, rwkv_k_k, rwkv_k_a, rwkv_r_k, rwkv_lnx_g, rwkv_lnx_b, w_br_rwkv, w_br_moba, w_out, ln2_g, ln2_b, ffn2_gate, ffn2_up, ffn2_down, ln3_g, ln3_b):
    assert x_prompt.shape[0] == 1 and x_sample.shape[1] == 1 and w_ada.shape[0] == 1
    tp = x_prompt.shape[1]
    ns = x_sample.shape[0]
    n_pages = page_table.shape[1]
    past_len = n_pages * PAGE_SIZE
    assert tp % MOBA_BLOCK == 0 and past_len % MOBA_BLOCK == 0 and n_pages % SAMPLE_PAGES_PER_STEP == 0

    bf = lambda z: z[0].astype(MXU_DT)
    wts = {"ffn1_gate": bf(ffn1_gate), "ffn1_up": bf(ffn1_up), "ffn1_down": bf(ffn1_down),
           "ffn2_gate": bf(ffn2_gate), "ffn2_up": bf(ffn2_up), "ffn2_down": bf(ffn2_down),
           "w_in": bf(w_in), "w_br_rwkv": bf(w_br_rwkv), "w_br_moba": bf(w_br_moba), "w_out": bf(w_out),
           "ln1_g": ln1_g[0], "ln1_b": ln1_b[0], "ln2_g": ln2_g[0], "ln2_b": ln2_b[0],
           "ln3_g": ln3_g[0], "ln3_b": ln3_b[0]}
    zw = jnp.zeros((LORA_W, WIDTH), MXU_DT)
    head_of = jnp.arange(WIDTH) // HEAD_DIM
    row = lambda z: z.reshape(1, -1)
    rp = {"mu": row(mu_shift[0]),
          "wcat": jnp.concatenate([jnp.concatenate([bf(rwkv_w2), zw], axis=1),
                                   jnp.concatenate([zw, bf(rwkv_a2)], axis=1)], axis=0),
          "w0": row(rwkv_w0[0]), "a0": row(rwkv_a0[0]), "g2": bf(rwkv_g2),
          "k_k": row(rwkv_k_k[0]), "k_a": row(rwkv_k_a[0]), "r_k": row(rwkv_r_k[0]),
          "lnx_g": row(rwkv_lnx_g[0]), "lnx_b": row(rwkv_lnx_b[0]),
          "ones": (head_of[:, None] == head_of[None, :]).astype(MXU_DT)}

    n_mod = ((1 + ns + 7) // 8) * 8
    c_all = jnp.concatenate([c_prompt, c_sample, jnp.zeros((n_mod - 1 - ns, D_MODEL), F32)], axis=0)
    ada = _ada(c_all, w_ada[0], b_ada)
    mod_p = [ada[0:1, j * D_MODEL:(j + 1) * D_MODEL] for j in range(9)]
    mod_s = [ada[1:1 + ns, j * D_MODEL:(j + 1) * D_MODEL] for j in range(9)]

    def rwkv_p(pr):
        r_out, s_fin, shift = _rwkv_prompt(pr, jnp.zeros((1, R_PROJ), F32),
                                           jnp.zeros((N_HEADS, HEAD_DIM, HEAD_DIM), F32), rp, _tile(tp, 256))
        return r_out, s_fin, shift

    def moba_p(q, k_new, v_new, kb, extra):
        vt, kmean = extra
        return _moba_prompt(q, kmean.reshape(-1, WIDTH).astype(MXU_DT), kb, vt)

    y_p, k_p, v_p, wkv_p, shift_p = _layer(x_prompt[0], mod_p, jnp.arange(tp, dtype=jnp.int32), wts, rp,
                                           rwkv_p, moba_p, True)

    ck, cv = jnp.transpose(cache_k[0], (0, 2, 3, 1)), jnp.transpose(cache_v[0], (0, 2, 3, 1))

    def rwkv_s(pr):
        s_rows = jnp.transpose(state_rwkv_wkv[0], (0, 2, 1, 3)).reshape(ns * HEAD_DIM, WIDTH)
        r_out, s_new = _rwkv_step(pr, state_rwkv_shift[0], s_rows, rp, _tile(ns, 8))
        s_new = jnp.transpose(s_new.reshape(ns, HEAD_DIM, N_HEADS, HEAD_DIM), (0, 2, 1, 3))
        return r_out, s_new, pr

    def moba_s(q, k_new, v_new, kb, extra):
        return _moba_sample(q, k_new, v_new, ck, cv, page_table)

    y_s, k_s, v_s, wkv_s, shift_s = _layer(x_sample[:, 0], mod_s, jnp.full((ns,), past_len, jnp.int32), wts, rp,
                                           rwkv_s, moba_s, False)

    heads = lambda z, n, t: z.reshape(1, n, t, N_HEADS, HEAD_DIM)
    return (y_p[None], y_s[:, None],
            heads(k_p, 1, tp), heads(v_p, 1, tp), heads(k_s, ns, 1), heads(v_s, ns, 1),
            wkv_p[None, None], wkv_s[None], shift_p[None], shift_s[None])
```

```python
import functools

import jax
import jax.numpy as jnp
from jax import lax
from jax.experimental import pallas as pl
from jax.experimental.pallas import tpu as pltpu

F32 = jnp.float32
MXU_DT = jnp.bfloat16

D_MODEL = 1024
PAGE_SIZE = 128
N_HEADS = 8
HEAD_DIM = 64
WIDTH = N_HEADS * HEAD_DIM
LORA_W = 64
LORA_A = 64
LORA_G = 128
R_PROJ = 3 * WIDTH + LORA_W + LORA_A + LORA_G
GN_EPS = 64e-5
MOBA_BLOCK = 256
MOBA_TOPK = 3
ROT_DIM = HEAD_DIM // 4
ROPE_THETA = 500000.0
IN_PROJ = R_PROJ + 3 * WIDTH + 2 * D_MODEL
D_FF = 2816
LN_EPS = 1e-5
DN_ALPHA = 2.0 ** 0.25
CHUNK = 64
LANES = 128
VMEM_LIMIT = 56 * 1024 * 1024
SAMPLE_PAGES_PER_STEP = 16
VT_ROWS = HEAD_DIM + 16

NEG_INF = float("-inf")
LOG2_E = 1.4426950408889634


def _cparams(*sem, vmem=VMEM_LIMIT):
    return pltpu.CompilerParams(dimension_semantics=sem, vmem_limit_bytes=vmem)


def _const_spec(shape):
    return pl.BlockSpec(shape, lambda *_: (0,) * len(shape), pipeline_mode=pl.Buffered(1))


def _row_spec(tm, width):
    return pl.BlockSpec((tm, width), lambda i: (i, 0))


def _mod_spec(arr, tm):
    if arr.shape[0] == 1:
        return pl.BlockSpec((1, arr.shape[1]), lambda i: (0, 0))
    return pl.BlockSpec((tm, arr.shape[1]), lambda i: (i, 0))


def _dot(a, b):
    return jnp.dot(a.astype(MXU_DT), b.astype(MXU_DT), preferred_element_type=F32)


_NN = (((1,), (0,)), ((), ()))
_NT = (((1,), (1,)), ((), ()))
_TN = (((0,), (0,)), ((), ()))


def _dg(a, b, dn):
    return lax.dot_general(a.astype(MXU_DT), b.astype(MXU_DT), dn, preferred_element_type=F32)


def _split(x):
    hi = x.astype(MXU_DT)
    lo = (x - hi.astype(F32)).astype(MXU_DT)
    return hi, lo


def _seg_sum(x, ones):
    hi, lo = _split(x)
    return (jnp.dot(hi, ones, preferred_element_type=F32)
            + jnp.dot(lo, ones, preferred_element_type=F32))


def _layer_norm(y, g, b):
    mu = jnp.mean(y, axis=-1, keepdims=True)
    yc = y - mu
    var = jnp.mean(yc * yc, axis=-1, keepdims=True)
    return yc * lax.rsqrt(var + LN_EPS) * g + b


def _ada_kernel(c_ref, w_ref, b_ref, o_ref):
    c = c_ref[...]
    o_ref[...] = _dot(c * jax.nn.sigmoid(c), w_ref[...]) + b_ref[...]


def _ada(c, w_ada, b_ada):
    n, d = c.shape
    nout = w_ada.shape[1]
    tn = 1024
    return pl.pallas_call(
        _ada_kernel,
        grid=(nout // tn,),
        in_specs=[pl.BlockSpec((n, d), lambda j: (0, 0)),
                  pl.BlockSpec((d, tn), lambda j: (0, j)),
                  pl.BlockSpec((1, tn), lambda j: (0, j))],
        out_specs=pl.BlockSpec((n, tn), lambda j: (0, j)),
        out_shape=jax.ShapeDtypeStruct((n, nout), F32),
        compiler_params=_cparams("parallel"),
    )(c, w_ada, b_ada)


def _ffn_kernel(x_ref, sh_ref, sc_ref, gt_ref, wg_ref, wu_ref, wd_ref, lg_ref, lb_ref, o_ref, *, n_ff):
    x = x_ref[...]
    h = (x * (1.0 + sc_ref[...]) + sh_ref[...]).astype(MXU_DT)
    step = D_FF // n_ff
    f = None
    for j in range(n_ff):
        cs = slice(j * step, (j + 1) * step)
        g = jnp.dot(h, wg_ref[:, cs], preferred_element_type=F32)
        u = jnp.dot(h, wu_ref[:, cs], preferred_element_type=F32)
        a = (g * jax.nn.sigmoid(g) * u).astype(MXU_DT)
        part = jnp.dot(a, wd_ref[cs, :], preferred_element_type=F32)
        f = part if f is None else f + part
    y = DN_ALPHA * x + (0.5 * gt_ref[...]) * f
    o_ref[...] = _layer_norm(y, lg_ref[...], lb_ref[...])


def _ffn(x, shift, scale, gate, wg, wu, wd, ln_g, ln_b, tm):
    t = x.shape[0]
    return pl.pallas_call(
        functools.partial(_ffn_kernel, n_ff=2),
        grid=(t // tm,),
        in_specs=[_row_spec(tm, D_MODEL), _mod_spec(shift, tm), _mod_spec(scale, tm), _mod_spec(gate, tm),
                  _const_spec((D_MODEL, D_FF)), _const_spec((D_MODEL, D_FF)), _const_spec((D_FF, D_MODEL)),
                  _const_spec((1, D_MODEL)), _const_spec((1, D_MODEL))],
        out_specs=_row_spec(tm, D_MODEL),
        out_shape=jax.ShapeDtypeStruct((t, D_MODEL), F32),
        compiler_params=_cparams("parallel"),
    )(x, shift, scale, gate, wg, wu, wd, ln_g, ln_b)


def _rotate(x, cos, s_up, s_dn):
    return x * cos + pltpu.roll(x, LANES - ROT_DIM // 2, 1) * s_up + pltpu.roll(x, ROT_DIM // 2, 1) * s_dn


def _inproj_kernel(x_ref, sh_ref, sc_ref, w_ref, cos_ref, sup_ref, sdn_ref,
                   pr_ref, q_ref, k_ref, v_ref, gr_ref, gm_ref, kb_ref, vt_ref, km_ref, *, q_scale):
    h = (x_ref[...] * (1.0 + sc_ref[...]) + sh_ref[...]).astype(MXU_DT)
    tm = h.shape[0]

    def proj(lo, width):
        return jnp.dot(h, w_ref[:, lo:lo + width], preferred_element_type=F32)

    pr_ref[...] = proj(0, R_PROJ)
    cos, s_up, s_dn = cos_ref[...], sup_ref[...], sdn_ref[...]
    q = proj(R_PROJ, WIDTH)
    k = proj(R_PROJ + WIDTH, WIDTH)
    k_rot = []
    for j in range(WIDTH // LANES):
        ls = slice(j * LANES, (j + 1) * LANES)
        qj = _rotate(q[:, ls], cos, s_up, s_dn)
        kj = _rotate(k[:, ls], cos, s_up, s_dn)
        q_ref[:, ls] = (qj * q_scale).astype(MXU_DT)
        kb_ref[:, ls] = kj.astype(MXU_DT)
        k_rot.append(kj)
        if km_ref is not None:
            km_ref[0, :, ls] = jnp.sum(kj.reshape(tm // MOBA_BLOCK, MOBA_BLOCK, LANES), axis=1) * (1.0 / MOBA_BLOCK)
    k_rot = jnp.concatenate(k_rot, axis=1)
    v = proj(R_PROJ + 2 * WIDTH, WIDTH)
    if vt_ref is None:
        k_ref[...] = k_rot
        v_ref[...] = v
    else:
        v_t = v.T
        k_ref[...] = k_rot.T
        v_ref[...] = v_t
        for head in range(N_HEADS):
            base = head * VT_ROWS
            vt_ref[base:base + HEAD_DIM, :] = v_t[head * HEAD_DIM:(head + 1) * HEAD_DIM, :].astype(MXU_DT)
            vt_ref[base + HEAD_DIM:base + VT_ROWS, :] = jnp.ones((VT_ROWS - HEAD_DIM, tm), MXU_DT)
    gr_ref[...] = proj(R_PROJ + 3 * WIDTH, D_MODEL)
    gm_ref[...] = proj(R_PROJ + 3 * WIDTH + D_MODEL, D_MODEL)


def _inproj_prompt_kernel(*refs):
    _inproj_kernel(*refs, q_scale=HEAD_DIM ** -0.5 * LOG2_E)


def _inproj_sample_kernel(*refs):
    _inproj_kernel(*refs, None, None, q_scale=HEAD_DIM ** -0.5)


def _inproj(x, shift, scale, w_in, cos, s_up, s_dn, tm, prompt):
    t = x.shape[0]
    out_shape = [jax.ShapeDtypeStruct((t, R_PROJ), F32),
                 jax.ShapeDtypeStruct((t, WIDTH), MXU_DT),
                 jax.ShapeDtypeStruct((t, WIDTH), F32),
                 jax.ShapeDtypeStruct((t, WIDTH), F32),
                 jax.ShapeDtypeStruct((t, D_MODEL), F32),
                 jax.ShapeDtypeStruct((t, D_MODEL), F32),
                 jax.ShapeDtypeStruct((t, WIDTH), MXU_DT)]
    out_specs = [_row_spec(tm, R_PROJ), _row_spec(tm, WIDTH), _row_spec(tm, WIDTH), _row_spec(tm, WIDTH),
                 _row_spec(tm, D_MODEL), _row_spec(tm, D_MODEL), _row_spec(tm, WIDTH)]
    if prompt:
        nblk = tm // MOBA_BLOCK
        vt_rows = N_HEADS * VT_ROWS
        out_shape[2] = out_shape[3] = jax.ShapeDtypeStruct((WIDTH, t), F32)
        out_specs[2] = out_specs[3] = pl.BlockSpec((WIDTH, tm), lambda i: (0, i))
        out_shape += [jax.ShapeDtypeStruct((vt_rows, t), MXU_DT),
                      jax.ShapeDtypeStruct((t // tm, nblk, WIDTH), F32)]
        out_specs += [pl.BlockSpec((vt_rows, tm), lambda i: (0, i)),
                      pl.BlockSpec((1, nblk, WIDTH), lambda i: (i, 0, 0))]
    return pl.pallas_call(
        _inproj_prompt_kernel if prompt else _inproj_sample_kernel,
        grid=(t // tm,),
        in_specs=[_row_spec(tm, D_MODEL), _mod_spec(shift, tm), _mod_spec(scale, tm),
                  _const_spec((D_MODEL, IN_PROJ)),
                  _row_spec(tm, LANES), _row_spec(tm, LANES), _row_spec(tm, LANES)],
        out_specs=out_specs,
        out_shape=out_shape,
        compiler_params=_cparams("parallel"),
    )(x, shift, scale, w_in, cos, s_up, s_dn)


def _rope_tables(pos):
    half = ROT_DIM // 2
    inv = ROPE_THETA ** (-jnp.arange(half, dtype=F32) * 2.0 / ROT_DIM)
    ang = pos.astype(F32)[:, None] * inv[None, :]
    cos, sin = jnp.cos(ang), jnp.sin(ang)
    t = pos.shape[0]
    zeros = lambda n: jnp.zeros((t, n), F32)
    per_head = lambda parts: jnp.tile(jnp.concatenate(parts, axis=1), (1, LANES // HEAD_DIM))
    c = per_head([cos, cos, jnp.ones((t, HEAD_DIM - ROT_DIM), F32)])
    s_up = per_head([-sin, zeros(HEAD_DIM - half)])
    s_dn = per_head([zeros(half), sin, zeros(HEAD_DIM - ROT_DIM)])
    return c, s_up, s_dn


def _rwkv_pointwise(pr, prev, mu, wcat, w0, a0, g2, k_k, k_a, ones):
    xs = pr + (prev - pr) * mu
    r = xs[:, 0:WIDTH]
    k = xs[:, WIDTH:2 * WIDTH]
    v = xs[:, 2 * WIDTH:3 * WIDTH]
    la = xs[:, 3 * WIDTH:3 * WIDTH + LORA_W + LORA_A]
    gl = xs[:, 3 * WIDTH + LORA_W + LORA_A:R_PROJ]
    lane = lax.broadcasted_iota(jnp.int32, la.shape, 1)
    la = jnp.where(lane < LORA_W, jnp.tanh(la), la)
    wa = _dot(la, wcat)
    w = w0 + wa[:, :WIDTH]
    a = jax.nn.sigmoid(a0 + wa[:, WIDTH:])
    g = _dot(jax.nn.sigmoid(gl), g2)
    z = -w
    softplus = jnp.maximum(z, 0.0) + jnp.log1p(jnp.exp(-jnp.abs(z)))
    log_decay = -jnp.exp(-softplus - 0.5)
    kk = k * k_k
    kk = kk / jnp.maximum(jnp.sqrt(_seg_sum(kk * kk, ones)), 1e-12)
    k_eff = k * (1.0 + (a - 1.0) * k_a)
    return r, k_eff, v, kk, kk * a, log_decay, g


def _rwkv_post(y, r, k_eff, v, g, r_k, lnx_g, lnx_b, ones):
    inv = 1.0 / HEAD_DIM
    yc = y - _seg_sum(y, ones) * inv
    yv = _seg_sum(yc * yc, ones) * inv
    yn = yc * lax.rsqrt(yv + GN_EPS) * lnx_g + lnx_b
    bonus = _seg_sum(r * k_eff * r_k, ones) * v
    return (yn + bonus) * g


def _rwkv_prompt_kernel(pr_ref, shift0_ref, s0_ref, mu_ref, wcat_ref, w0_ref, a0_ref, g2_ref, kk_ref, ka_ref,
                        rk_ref, lg_ref, lb_ref, ones_ref, tri_ref, blk_ref,
                        out_ref, sfin_ref, shift_ref,
                        prev_scr, st_scr, rt_scr, kp_scr, bt_scr, kt_scr, v_scr, bc_scr, kc_scr, pc_scr, y_scr):
    i = pl.program_id(0)

    @pl.when(i == 0)
    def _():
        prev_scr[...] = shift0_ref[...]
        st_scr[...] = s0_ref[...]

    pr = pr_ref[...]
    tm = pr.shape[0]
    row = lax.broadcasted_iota(jnp.int32, (tm, 1), 0)
    prev = jnp.where(row == 0, prev_scr[...], pltpu.roll(pr, 1, 0))
    ones = ones_ref[...]
    r, k_eff, v, kk, b, lw, g = _rwkv_pointwise(pr, prev, mu_ref[...], wcat_ref[...], w0_ref[...], a0_ref[...],
                                                g2_ref[...], kk_ref[...], ka_ref[...], ones)
    l_hi = lw.astype(MXU_DT)
    rem = lw - l_hi.astype(F32)
    l_mid = rem.astype(MXU_DT)
    l_lo = (rem - l_mid.astype(F32)).astype(MXU_DT)
    cum = lambda m: (jnp.dot(m, l_hi, preferred_element_type=F32)
                     + (jnp.dot(m, l_mid, preferred_element_type=F32) + jnp.dot(m, l_lo, preferred_element_type=F32)))
    L = cum(tri_ref[...])
    LC = cum(blk_ref[...])
    e_inv = jnp.exp(-L)
    e_end = jnp.exp(LC - L)
    rt_scr[...] = r * jnp.exp(L)
    kp_scr[...] = kk * jnp.exp(L - lw)
    bt_scr[...] = b * e_inv
    kt_scr[...] = k_eff * e_inv
    bc_scr[...] = b * e_end
    kc_scr[...] = k_eff * e_end
    pc_scr[...] = jnp.exp(LC)
    v_scr[...] = v

    ri = lax.broadcasted_iota(jnp.int32, (CHUNK, CHUNK), 0)
    ci = lax.broadcasted_iota(jnp.int32, (CHUNK, CHUNK), 1)
    strict, incl, eye = ri > ci, ri >= ci, ri == ci

    n_chunks = tm // CHUNK
    probs = [(c, h) for c in range(n_chunks) for h in range(N_HEADS)]
    each = lambda fn, *lists: [fn(*args) for args in zip(*lists)]

    def blocks(ref):
        return [ref[c * CHUNK:(c + 1) * CHUNK, h * HEAD_DIM:(h + 1) * HEAD_DIM] for c, h in probs]

    def blocks_t(ref):
        per_chunk = [ref[c * CHUNK:(c + 1) * CHUNK, :].T for c in range(n_chunks)]
        return [per_chunk[c][h * HEAD_DIM:(h + 1) * HEAD_DIM, :] for c, h in probs]

    kp, rt, v_b = blocks(kp_scr), blocks(rt_scr), blocks(v_scr)
    bk = each(lambda a, b_: jnp.concatenate([a, b_], axis=0), blocks(bt_scr), blocks(kt_scr))
    bk = each(lambda b_: b_.astype(MXU_DT), bk)
    v_m = each(lambda a: a.astype(MXU_DT), v_b)
    akk = each(lambda a, b_: _dg(a, b_, _NT), kp, bk)
    arr = each(lambda a, b_: _dg(a, b_, _NT), rt, bk)
    a_zb = [jnp.where(strict, a[:, :CHUNK], 0.0) for a in akk]
    a_vk = [jnp.where(strict, a[:, CHUNK:], 0.0) for a in akk]
    a_rb = [jnp.where(incl, a[:, :CHUNK], 0.0) for a in arr]
    a_rk = [jnp.where(incl, a[:, CHUNK:], 0.0) for a in arr]
    u0 = each(lambda a, w, v_: jnp.concatenate([a, _dg(w, v_, _NN)], axis=1), kp, a_vk, v_m)
    half = lambda s: (ri // (2 * s) == ci // (2 * s)) & (ri % (2 * s) >= s) & (ci % (2 * s) < s)
    t_inv = [jnp.where(eye, 1.0, 0.0) - jnp.where(half(1), a, 0.0) for a in a_zb]
    s_blk = 2
    while s_blk < CHUNK:
        lower = half(s_blk)
        t_inv = each(lambda t_, a: t_ - _dg(_dg(t_, jnp.where(lower, a, 0.0), _NN), t_, _NN), t_inv, a_zb)
        s_blk *= 2
    u = each(lambda t_, u_: _dg(t_, u_, _NN), t_inv, u0)
    u_m = each(lambda a: a.astype(MXU_DT), u)
    gmat = each(lambda a, u_: _dg(a, u_, _NN), a_rb, u_m)
    r_eff = each(lambda a, g_: a - g_[:, :HEAD_DIM], rt, gmat)
    y0 = each(lambda a, b_, g_: _dg(a, b_, _NN) - g_[:, HEAD_DIM:], a_rk, v_m, gmat)
    bu = each(lambda a, u_: _dg(a, u_, _NN), blocks_t(bc_scr), u_m)
    pc = [pc_scr[c * CHUNK:c * CHUNK + 1, h * HEAD_DIM:(h + 1) * HEAD_DIM] for c, h in probs]
    m_t = each(lambda p_, b_: jnp.where(eye, p_, 0.0) - b_[:, :HEAD_DIM], pc, bu)
    n_t = each(lambda a, v_, c_: _dg(a, v_, _NN) - c_[:, HEAD_DIM:], blocks_t(kc_scr), v_m, bu)
    st = [st_scr[h] for h in range(N_HEADS)]
    for c in range(n_chunks):
        sl = slice(c * N_HEADS, (c + 1) * N_HEADS)
        st_m = each(lambda a: a.astype(MXU_DT), st)
        y_scr[c * CHUNK:(c + 1) * CHUNK, :] = jnp.concatenate(
            each(lambda a, s_, b_: _dg(a, s_, _NN) + b_, r_eff[sl], st_m, y0[sl]), axis=1)
        st = each(lambda m_, s_, n_: _dg(m_, s_, _NN) + n_, m_t[sl], st_m, n_t[sl])
    for h in range(N_HEADS):
        st_scr[h] = st[h]
    out_ref[...] = _rwkv_post(y_scr[...], r, k_eff, v, g, rk_ref[...], lg_ref[...], lb_ref[...], ones)
    prev_scr[...] = pr[tm - 1:tm, :]
    shift_ref[...] = pr[tm - 1:tm, :]

    @pl.when(i == pl.num_programs(0) - 1)
    def _():
        for h in range(N_HEADS):
            sfin_ref[h] = st[h].T


def _rwkv_prompt(pr, shift0, s0, rp, tm):
    t = pr.shape[0]
    ch = lax.broadcasted_iota(jnp.int32, (tm, tm), 0) // CHUNK == lax.broadcasted_iota(jnp.int32, (tm, tm), 1) // CHUNK
    low = lax.broadcasted_iota(jnp.int32, (tm, tm), 0) >= lax.broadcasted_iota(jnp.int32, (tm, tm), 1)
    tri = (ch & low).astype(MXU_DT)
    blk = ch.astype(MXU_DT)
    vec = _const_spec((1, WIDTH))
    tile = pltpu.VMEM((tm, WIDTH), F32)
    return pl.pallas_call(
        _rwkv_prompt_kernel,
        grid=(t // tm,),
        in_specs=[_row_spec(tm, R_PROJ), _const_spec((1, R_PROJ)), _const_spec((N_HEADS, HEAD_DIM, HEAD_DIM)),
                  _const_spec((1, R_PROJ)), _const_spec((LORA_W + LORA_A, 2 * WIDTH)), vec, vec,
                  _const_spec((LORA_G, WIDTH)), vec, vec, vec, vec, vec,
                  _const_spec((WIDTH, WIDTH)), _const_spec((tm, tm)), _const_spec((tm, tm))],
        out_specs=[_row_spec(tm, WIDTH),
                   pl.BlockSpec((N_HEADS, HEAD_DIM, HEAD_DIM), lambda i: (0, 0, 0)),
                   pl.BlockSpec((1, R_PROJ), lambda i: (0, 0))],
        out_shape=[jax.ShapeDtypeStruct((t, WIDTH), F32),
                   jax.ShapeDtypeStruct((N_HEADS, HEAD_DIM, HEAD_DIM), F32),
                   jax.ShapeDtypeStruct((1, R_PROJ), F32)],
        scratch_shapes=[pltpu.VMEM((1, R_PROJ), F32), pltpu.VMEM((N_HEADS, HEAD_DIM, HEAD_DIM), F32)] + [tile] * 9,
        compiler_params=_cparams("arbitrary"),
    )(pr, shift0, jnp.swapaxes(s0, 1, 2), rp["mu"], rp["wcat"], rp["w0"], rp["a0"], rp["g2"], rp["k_k"], rp["k_a"],
      rp["r_k"], rp["lnx_g"], rp["lnx_b"], rp["ones"], tri, blk)


def _rwkv_step_kernel(pr_ref, shift_ref, s_ref, mu_ref, wcat_ref, w0_ref, a0_ref, g2_ref, kk_ref, ka_ref,
                      rk_ref, lg_ref, lb_ref, ones_ref, out_ref, snew_ref):
    ones = ones_ref[...]
    r, k_eff, v, kk, b, lw, g = _rwkv_pointwise(pr_ref[...], shift_ref[...], mu_ref[...], wcat_ref[...],
                                                w0_ref[...], a0_ref[...], g2_ref[...], kk_ref[...], ka_ref[...], ones)
    decay = jnp.exp(lw)
    eye = (lax.broadcasted_iota(jnp.int32, (HEAD_DIM, HEAD_DIM), 0)
           == lax.broadcasted_iota(jnp.int32, (HEAD_DIM, HEAD_DIM), 1))[None]
    ys = []
    for h in range(N_HEADS):
        over_v = lambda z: z[:, None, h * HEAD_DIM:(h + 1) * HEAD_DIM]
        s = s_ref[:, h]
        s_kk = jnp.sum(s * over_v(kk), axis=2, keepdims=True)
        v_col = jnp.sum(jnp.where(eye, over_v(v), 0.0), axis=2, keepdims=True)
        s_new = s * over_v(decay) - s_kk * over_v(b) + v_col * over_v(k_eff)
        y_col = jnp.sum(s_new * over_v(r), axis=2, keepdims=True)
        ys.append(jnp.sum(jnp.where(eye, y_col, 0.0), axis=1))
        snew_ref[:, h] = s_new
    y = jnp.concatenate(ys, axis=1)
    out_ref[...] = _rwkv_post(y, r, k_eff, v, g, rk_ref[...], lg_ref[...], lb_ref[...], ones)


def _rwkv_step(pr, shift, state, rp, nb):
    n = pr.shape[0]
    vec = _const_spec((1, WIDTH))
    state_spec = pl.BlockSpec((nb, N_HEADS, HEAD_DIM, HEAD_DIM), lambda i: (i, 0, 0, 0))
    return pl.pallas_call(
        _rwkv_step_kernel,
        grid=(n // nb,),
        in_specs=[_row_spec(nb, R_PROJ), _row_spec(nb, R_PROJ), state_spec,
                  _const_spec((1, R_PROJ)), _const_spec((LORA_W + LORA_A, 2 * WIDTH)), vec, vec,
                  _const_spec((LORA_G, WIDTH)), vec, vec, vec, vec, vec, _const_spec((WIDTH, WIDTH))],
        out_specs=[_row_spec(nb, WIDTH), state_spec],
        out_shape=[jax.ShapeDtypeStruct((n, WIDTH), F32), jax.ShapeDtypeStruct(state.shape, F32)],
        compiler_params=_cparams("parallel"),
    )(pr, shift, state, rp["mu"], rp["wcat"], rp["w0"], rp["a0"], rp["g2"], rp["k_k"], rp["k_a"], rp["r_k"],
      rp["lnx_g"], rp["lnx_b"], rp["ones"])


def _moba_prompt_kernel(q_ref, km_ref, k_ref, vt_ref, o_ref, qm_scr, sel_scr, m_scr, acc_scr,
                        s0_scr, s1_scr, cm_scr):
    c = pl.program_id(0)
    tq = q_ref.shape[0]
    nblk = km_ref.shape[0]
    lane = lax.broadcasted_iota(jnp.int32, (1, LANES), 1)
    blk_id = lax.broadcasted_iota(jnp.int32, (nblk, tq), 0)
    key_i = lax.broadcasted_iota(jnp.int32, (MOBA_BLOCK, tq), 0)
    qry_i = lax.broadcasted_iota(jnp.int32, (MOBA_BLOCK, tq), 1)

    def head_lanes(h):
        return (lane < HEAD_DIM) if h % 2 == 0 else (lane >= HEAD_DIM)

    def values(h, b):
        rows = slice(h * VT_ROWS, (h + 1) * VT_ROWS)
        return vt_ref[rows, pl.ds(pl.multiple_of(b * MOBA_BLOCK, MOBA_BLOCK), MOBA_BLOCK)]

    def keys(h, b):
        ps = slice((h // 2) * LANES, (h // 2 + 1) * LANES)
        return k_ref[pl.ds(pl.multiple_of(b * MOBA_BLOCK, MOBA_BLOCK), MOBA_BLOCK), ps]

    for h in range(N_HEADS):
        ps = slice((h // 2) * LANES, (h // 2 + 1) * LANES)
        qm = jnp.where(head_lanes(h), q_ref[:, ps].astype(F32), 0.0).astype(MXU_DT)
        qm_scr[h] = qm
        gate = lax.dot_general(km_ref[:, ps], qm, _NT, preferred_element_type=F32)
        gate = jnp.where(blk_id < c, gate, NEG_INF)
        sel = jnp.zeros(gate.shape, jnp.bool_)
        for _ in range(MOBA_TOPK):
            top = jnp.max(gate, axis=0, keepdims=True)
            idx = jnp.min(jnp.where(gate == top, blk_id, nblk), axis=0, keepdims=True)
            hit = blk_id == idx
            sel = sel | hit
            gate = jnp.where(hit, NEG_INF, gate)
        sel_scr[h] = sel.astype(F32)
        s = lax.dot_general(keys(h, c), qm, _NT, preferred_element_type=F32)
        s = jnp.where(key_i <= qry_i, s, NEG_INF)
        m = jnp.max(s, axis=0, keepdims=True)
        p = jnp.exp2(s - m).astype(MXU_DT)
        m_scr[h] = m
        acc_scr[h] = jnp.dot(values(h, c), p, preferred_element_type=F32)

    def scores(b, s_scr, slot):
        for h in range(N_HEADS):
            s = lax.dot_general(keys(h, b), qm_scr[h], _NT, preferred_element_type=F32)
            s_scr[h] = s
            cm_scr[slot, h] = jnp.max(s, axis=0, keepdims=True)

    def absorb(b, s_scr, slot):
        for h in range(N_HEADS):
            picked = sel_scr[h, pl.ds(b, 1), :] > 0.0
            m_old = m_scr[h]
            m_new = jnp.where(picked, jnp.maximum(m_old, cm_scr[slot, h]), m_old)
            p = jnp.exp2(s_scr[h] - jnp.where(picked, m_new, jnp.inf)).astype(MXU_DT)
            acc_scr[h] = acc_scr[h] * jnp.exp2(m_old - m_new) + jnp.dot(values(h, b), p, preferred_element_type=F32)
            m_scr[h] = m_new

    def block_pair(i, carry):
        scores(2 * i, s0_scr, 0)
        scores(2 * i + 1, s1_scr, 1)
        absorb(2 * i, s0_scr, 0)
        absorb(2 * i + 1, s1_scr, 1)
        return carry

    lax.fori_loop(0, c // 2, block_pair, 0)

    @pl.when(c % 2 == 1)
    def _():
        scores(c - 1, s0_scr, 0)
        absorb(c - 1, s0_scr, 0)

    for pair in range(N_HEADS // 2):
        outs = []
        for h in (2 * pair, 2 * pair + 1):
            a = acc_scr[h]
            outs.append((a[:HEAD_DIM] / a[HEAD_DIM:HEAD_DIM + 1]).T)
        o_ref[:, pair * LANES:(pair + 1) * LANES] = jnp.concatenate(outs, axis=1)


def _moba_prompt(q, kmean, kb, vt):
    t = q.shape[0]
    nblk = kmean.shape[0]
    tq = MOBA_BLOCK
    return pl.pallas_call(
        _moba_prompt_kernel,
        grid=(t // tq,),
        in_specs=[_row_spec(tq, WIDTH), _const_spec((nblk, WIDTH)), _const_spec((t, WIDTH)),
                  _const_spec(vt.shape)],
        out_specs=_row_spec(tq, WIDTH),
        out_shape=jax.ShapeDtypeStruct((t, WIDTH), F32),
        scratch_shapes=[pltpu.VMEM((N_HEADS, tq, LANES), MXU_DT), pltpu.VMEM((N_HEADS, nblk, tq), F32),
                        pltpu.VMEM((N_HEADS, 1, tq), F32), pltpu.VMEM((N_HEADS, VT_ROWS, tq), F32),
                        pltpu.VMEM((N_HEADS, MOBA_BLOCK, tq), F32), pltpu.VMEM((N_HEADS, MOBA_BLOCK, tq), F32),
                        pltpu.VMEM((2, N_HEADS, 1, tq), F32)],
        compiler_params=_cparams("parallel"),
    )(q, kmean, kb, vt)


def _moba_sample_score_kernel(pt_ref, q_ref, knew_ref, *refs):
    del pt_ref
    pg = SAMPLE_PAGES_PER_STEP
    pages = refs[:pg]
    p_ref, pself_ref, idx_ref, sc_scr, gate_scr = refs[pg:]
    j = pl.program_id(1)
    nblk = gate_scr.shape[1]
    rnd = lambda z: z.astype(MXU_DT).astype(F32)
    q = q_ref[0]
    for t in range(pg // 2):
        halves = (pages[2 * t][0], pages[2 * t + 1][0])
        sc = jnp.concatenate([jnp.sum(page * q, axis=1, keepdims=True) for page in halves], axis=2)
        gate = jnp.broadcast_to(jnp.sum(sc, axis=2, keepdims=True), (N_HEADS, 1, LANES))
        blk = pl.ds(j * (pg // 2) + t, 1)
        for h in range(N_HEADS):
            sc_scr[h, blk, :] = sc[h]
            gate_scr[h, blk, :] = gate[h]

    @pl.when(j == pl.num_programs(1) - 1)
    def _():
        gate = gate_scr[...]
        blk_id = lax.broadcasted_iota(jnp.int32, gate.shape, 1)
        sel = jnp.zeros(gate.shape, jnp.bool_)
        for r in range(MOBA_TOPK):
            top = jnp.max(gate, axis=1, keepdims=True)
            idx = jnp.min(jnp.where(gate == top, blk_id, nblk), axis=1, keepdims=True)
            hit = blk_id == idx
            sel = sel | hit
            gate = jnp.where(hit, NEG_INF, gate)
            idx_ref[0, r] = idx
        sel_f = jnp.where(sel, 1.0, 0.0)
        picked = jnp.concatenate([sel_f] * (MOBA_BLOCK // LANES), axis=2) > 0.0
        s = jnp.where(picked, sc_scr[...], NEG_INF)
        s_self = jnp.sum(q * rnd(knew_ref[0]), axis=1, keepdims=True)
        m = jnp.maximum(jnp.max(jnp.max(s, axis=2, keepdims=True), axis=1, keepdims=True), s_self)
        p = jnp.exp(s - m[:, :, :1])
        p_self = jnp.exp(s_self - m)
        inv = 1.0 / (jnp.sum(jnp.sum(p, axis=2, keepdims=True), axis=1, keepdims=True) + p_self)
        p_ref[0] = p * inv[:, :, :1]
        pself_ref[0] = p_self * inv


def _moba_sample_scores(q_lanes, knew_lanes, cache_kt, page_table):
    n, n_pages = page_table.shape
    pg = SAMPLE_PAGES_PER_STEP
    nblk = n_pages * PAGE_SIZE // MOBA_BLOCK
    page_block = (1, N_HEADS, HEAD_DIM, PAGE_SIZE)

    def page_spec(e):
        return pl.BlockSpec(page_block, lambda i, j, pt: (pt[i * n_pages + j * pg + e], 0, 0, 0))

    per_seq = pl.BlockSpec(page_block, lambda i, j, pt: (i, 0, 0, 0))
    grid_spec = pltpu.PrefetchScalarGridSpec(
        num_scalar_prefetch=1,
        grid=(n, n_pages // pg),
        in_specs=[per_seq, per_seq] + [page_spec(e) for e in range(pg)],
        out_specs=[pl.BlockSpec((1, N_HEADS, nblk, MOBA_BLOCK), lambda i, j, pt: (i, 0, 0, 0)),
                   pl.BlockSpec((1, N_HEADS, 1, LANES), lambda i, j, pt: (i, 0, 0, 0)),
                   pl.BlockSpec((1, MOBA_TOPK, N_HEADS, 1, LANES), lambda i, j, pt: (i, 0, 0, 0, 0))],
        scratch_shapes=[pltpu.VMEM((N_HEADS, nblk, MOBA_BLOCK), F32), pltpu.VMEM((N_HEADS, nblk, LANES), F32)],
    )
    return pl.pallas_call(
        _moba_sample_score_kernel,
        grid_spec=grid_spec,
        out_shape=[jax.ShapeDtypeStruct((n, N_HEADS, nblk, MOBA_BLOCK), F32),
                   jax.ShapeDtypeStruct((n, N_HEADS, 1, LANES), F32),
                   jax.ShapeDtypeStruct((n, MOBA_TOPK, N_HEADS, 1, LANES), jnp.int32)],
        compiler_params=_cparams("parallel", "arbitrary"),
    )(page_table.reshape(-1), q_lanes, knew_lanes, *([cache_kt] * pg))


def _moba_sample_value_kernel(phys_ref, blk_ref, p_ref, pself_ref, vnew_ref, *refs):
    del phys_ref
    pages_per_block = MOBA_BLOCK // PAGE_SIZE
    v_refs, o_ref = refs[:-1], refs[-1]
    i = pl.program_id(0)
    rnd = lambda z: z.astype(MXU_DT).astype(F32)
    for h in range(N_HEADS):
        acc = rnd(pself_ref[0, h][:, :HEAD_DIM]) * rnd(vnew_ref[0, h])
        for r in range(MOBA_TOPK):
            slot = h * MOBA_TOPK + r
            p_row = p_ref[0, h, pl.ds(blk_ref[i * N_HEADS * MOBA_TOPK + slot], 1), :]
            for e in range(pages_per_block):
                p_rows = jnp.broadcast_to(p_row[:, e * PAGE_SIZE:(e + 1) * PAGE_SIZE], (N_HEADS, PAGE_SIZE))
                v_t = v_refs[slot * pages_per_block + e][0, 0]
                acc = acc + _dg(p_rows, v_t, _NT)[0:1]
        o_ref[0, h] = acc


def _moba_sample_values(p, p_self, v_new, cache_vt, phys, blocks):
    n, _, nblk, _ = p.shape
    pages_per_block = MOBA_BLOCK // PAGE_SIZE
    per_seq = N_HEADS * MOBA_TOPK * pages_per_block

    def page_spec(h, e):
        return pl.BlockSpec((1, 1, HEAD_DIM, PAGE_SIZE),
                            lambda i, ph, bk: (ph[i * per_seq + h * MOBA_TOPK * pages_per_block + e], h, 0, 0))

    whole = lambda shape: pl.BlockSpec((1,) + shape, lambda i, ph, bk: (i,) + (0,) * len(shape))
    grid_spec = pltpu.PrefetchScalarGridSpec(
        num_scalar_prefetch=2,
        grid=(n,),
        in_specs=[whole((N_HEADS, nblk, MOBA_BLOCK)), whole((N_HEADS, 1, LANES)), whole((N_HEADS, 1, HEAD_DIM))]
        + [page_spec(h, e) for h in range(N_HEADS) for e in range(MOBA_TOPK * pages_per_block)],
        out_specs=whole((N_HEADS, 1, HEAD_DIM)),
    )
    return pl.pallas_call(
        _moba_sample_value_kernel,
        grid_spec=grid_spec,
        out_shape=jax.ShapeDtypeStruct((n, N_HEADS, 1, HEAD_DIM), F32),
        compiler_params=_cparams("parallel"),
    )(phys, blocks, p, p_self, v_new.reshape(n, N_HEADS, 1, HEAD_DIM), *([cache_vt] * per_seq))


def _moba_sample(q, k_new, v_new, cache_kt, cache_vt, page_table):
    n = q.shape[0]
    lanes = lambda z: jnp.broadcast_to(z.astype(F32).reshape(n, N_HEADS, HEAD_DIM, 1),
                                       (n, N_HEADS, HEAD_DIM, PAGE_SIZE))
    p, p_self, idx = _moba_sample_scores(lanes(q), lanes(k_new), cache_kt, page_table)
    pages_per_block = MOBA_BLOCK // PAGE_SIZE
    blocks = jnp.transpose(idx[..., 0, 0], (0, 2, 1))
    logi = (blocks[..., None] * pages_per_block + jnp.arange(pages_per_block, dtype=jnp.int32)).reshape(
        n, N_HEADS, -1)
    phys = jnp.take_along_axis(page_table[:, None, :], logi, axis=2)
    out = _moba_sample_values(p, p_self, v_new, cache_vt, phys.reshape(-1), blocks.reshape(-1))
    return out.reshape(n, WIDTH)


def _merge_kernel(x_ref, r_ref, a_ref, gr_ref, gm_ref, gt_ref, wr_ref, wm_ref, wo_ref, lg_ref, lb_ref, o_ref):
    br = jax.nn.sigmoid(gr_ref[...]) * _dot(r_ref[...], wr_ref[...])
    bm = jax.nn.sigmoid(gm_ref[...]) * _dot(a_ref[...], wm_ref[...])
    merged = _dot(br + bm, wo_ref[...])
    o_ref[...] = _layer_norm(DN_ALPHA * x_ref[...] + gt_ref[...] * merged, lg_ref[...], lb_ref[...])


def _merge(x, r_out, att, g_r, g_m, gate, wr, wm, wo, ln_g, ln_b, tm):
    t = x.shape[0]
    return pl.pallas_call(
        _merge_kernel,
        grid=(t // tm,),
        in_specs=[_row_spec(tm, D_MODEL), _row_spec(tm, WIDTH), _row_spec(tm, WIDTH), _row_spec(tm, D_MODEL),
                  _row_spec(tm, D_MODEL), _mod_spec(gate, tm),
                  _const_spec((WIDTH, D_MODEL)), _const_spec((WIDTH, D_MODEL)), _const_spec((D_MODEL, D_MODEL)),
                  _const_spec((1, D_MODEL)), _const_spec((1, D_MODEL))],
        out_specs=_row_spec(tm, D_MODEL),
        out_shape=jax.ShapeDtypeStruct((t, D_MODEL), F32),
        compiler_params=_cparams("parallel"),
    )(x, r_out, att, g_r, g_m, gate, wr, wm, wo, ln_g, ln_b)


def _tile(t, want):
    return want if t % want == 0 else t


def _layer(x, mod, pos, wts, rp, rwkv_fn, moba_fn, prompt):
    t = x.shape[0]
    tm = _tile(t, 512)
    row = lambda z: z.reshape(1, -1)
    x1 = _ffn(x, mod[0], mod[1], mod[2], wts["ffn1_gate"], wts["ffn1_up"], wts["ffn1_down"],
              row(wts["ln1_g"]), row(wts["ln1_b"]), tm)
    cos, s_up, s_dn = _rope_tables(pos)
    proj = _inproj(x1, mod[3], mod[4], wts["w_in"], cos, s_up, s_dn, tm, prompt)
    pr, q, k_new, v_new, g_r, g_m, kb = proj[:7]
    r_out, wkv_new, shift_new = rwkv_fn(pr)
    att = moba_fn(q, k_new, v_new, kb, proj[7:])
    x2 = _merge(x1, r_out, att, g_r, g_m, mod[5], wts["w_br_rwkv"], wts["w_br_moba"], wts["w_out"],
                row(wts["ln2_g"]), row(wts["ln2_b"]), tm)
    y = _ffn(x2, mod[6], mod[7], mod[8], wts["ffn2_gate"], wts["ffn2_up"], wts["ffn2_down"],
             row(wts["ln3_g"]), row(wts["ln3_b"]), tm)
    return y, k_new, v_new, wkv_new, shift_new


def kernel(x_prompt, x_sample, cache_k, cache_v, state_rwkv_wkv, state_rwkv_shift, page_table, c_prompt, c_sample, w_ada, b_ada, ffn1_gate, ffn1_up, ffn1_down, ln1_g, ln1_b, w_in, mu_shift, rwkv_w0, rwkv_w2, rwkv_a0, rwkv_a2, rwkv_g2, rwkv_k_k, rwkv_k_a, rwkv_r_k, rwkv_lnx_g, rwkv_lnx_b, w_br_rwkv, w_br_moba, w_out, ln2_g, ln2_b, ffn2_gate, ffn2_up, ffn2_down, ln3_g, ln3_b):
    assert x_prompt.shape[0] == 1 and x_sample.shape[1] == 1 and w_ada.shape[0] == 1
    tp = x_prompt.shape[1]
    ns = x_sample.shape[0]
    n_pages = page_table.shape[1]
    past_len = n_pages * PAGE_SIZE
    assert tp % MOBA_BLOCK == 0 and past_len % MOBA_BLOCK == 0 and n_pages % SAMPLE_PAGES_PER_STEP == 0

    bf = lambda z: z[0].astype(MXU_DT)
    wts = {"ffn1_gate": bf(ffn1_gate), "ffn1_up": bf(ffn1_up), "ffn1_down": bf(ffn1_down),
           "ffn2_gate": bf(ffn2_gate), "ffn2_up": bf(ffn2_up), "ffn2_down": bf(ffn2_down),
           "w_in": bf(w_in), "w_br_rwkv": bf(w_br_rwkv), "w_br_moba": bf(w_br_moba), "w_out": bf(w_out),
           "ln1_g": ln1_g[0], "ln1_b": ln1_b[0], "ln2_g": ln2_g[0], "ln2_b": ln2_b[0],
           "ln3_g": ln3_g[0], "ln3_b": ln3_b[0]}
    zw = jnp.zeros((LORA_W, WIDTH), MXU_DT)
    head_of = jnp.arange(WIDTH) // HEAD_DIM
    row = lambda z: z.reshape(1, -1)
    rp = {"mu": row(mu_shift[0]),
          "wcat": jnp.concatenate([jnp.concatenate([bf(rwkv_w2), zw], axis=1),
                                   jnp.concatenate([zw, bf(rwkv_a2)], axis=1)], axis=0),
          "w0": row(rwkv_w0[0]), "a0": row(rwkv_a0[0]), "g2": bf(rwkv_g2),
          "k_k": row(rwkv_k_k[0]), "k_a": row(rwkv_k_a[0]), "r_k": row(rwkv_r_k[0]),
          "lnx_g": row(rwkv_lnx_g[0]), "lnx_b": row(rwkv_lnx_b[0]),
          "ones": (head_of[:, None] == head_of[None, :]).astype(MXU_DT)}

    n_mod = ((1 + ns + 7) // 8) * 8
    c_all = jnp.concatenate([c_prompt, c_sample, jnp.zeros((n_mod - 1 - ns, D_MODEL), F32)], axis=0)
    ada = _ada(c_all, w_ada[0], b_ada)
    mod_p = [ada[0:1, j * D_MODEL:(j + 1) * D_MODEL] for j in range(9)]
    mod_s = [ada[1:1 + ns, j * D_MODEL:(j + 1) * D_MODEL] for j in range(9)]

    def rwkv_p(pr):
        r_out, s_fin, shift = _rwkv_prompt(pr, jnp.zeros((1, R_PROJ), F32),
                                           jnp.zeros((N_HEADS, HEAD_DIM, HEAD_DIM), F32), rp, _tile(tp, 256))
        return r_out, s_fin, shift

    def moba_p(q, k_new, v_new, kb, extra):
        vt, kmean = extra
        return _moba_prompt(q, kmean.reshape(-1, WIDTH).astype(MXU_DT), kb, vt)

    y_p, k_p, v_p, wkv_p, shift_p = _layer(x_prompt[0], mod_p, jnp.arange(tp, dtype=jnp.int32), wts, rp,
                                           rwkv_p, moba_p, True)

    ck, cv = jnp.transpose(cache_k[0], (0, 2, 3, 1)), jnp.transpose(cache_v[0], (0, 2, 3, 1))

    def rwkv_s(pr):
        r_out, s_new = _rwkv_step(pr, state_rwkv_shift[0], state_rwkv_wkv[0], rp, _tile(ns, 8))
        return r_out, s_new, pr

    def moba_s(q, k_new, v_new, kb, extra):
        return _moba_sample(q, k_new, v_new, ck, cv, page_table)

    y_s, k_s, v_s, wkv_s, shift_s = _layer(x_sample[:, 0], mod_s, jnp.full((ns,), past_len, jnp.int32), wts, rp,
                                           rwkv_s, moba_s, False)

    heads = lambda z, n, t: z.reshape(1, n, t, N_HEADS, HEAD_DIM)
    heads_t = lambda z: jnp.transpose(z.reshape(N_HEADS, HEAD_DIM, tp), (2, 0, 1))[None, None]
    return (y_p[None], y_s[:, None],
            heads_t(k_p), heads_t(v_p), heads(k_s, ns, 1), heads(v_s, ns, 1),
            wkv_p[None, None], wkv_s[None], shift_p[None], shift_s[None])
```

```python
import functools

import jax
import jax.numpy as jnp
from jax import lax
from jax.experimental import pallas as pl
from jax.experimental.pallas import tpu as pltpu

F32 = jnp.float32
MXU_DT = jnp.bfloat16

D_MODEL = 1024
PAGE_SIZE = 128
N_HEADS = 8
HEAD_DIM = 64
WIDTH = N_HEADS * HEAD_DIM
LORA_W = 64
LORA_A = 64
LORA_G = 128
R_PROJ = 3 * WIDTH + LORA_W + LORA_A + LORA_G
GN_EPS = 64e-5
MOBA_BLOCK = 256
MOBA_TOPK = 3
ROT_DIM = HEAD_DIM // 4
ROPE_THETA = 500000.0
IN_PROJ = R_PROJ + 3 * WIDTH + 2 * D_MODEL
D_FF = 2816
LN_EPS = 1e-5
DN_ALPHA = 2.0 ** 0.25
CHUNK = 64
LANES = 128
VMEM_LIMIT = 56 * 1024 * 1024
SAMPLE_PAGES_PER_STEP = 32
FFN_ROWS = 1024
FFN_CHUNKS = 11
VT_ROWS = HEAD_DIM + 16

NEG_INF = float("-inf")
LOG2_E = 1.4426950408889634


def _cparams(*sem, vmem=VMEM_LIMIT):
    return pltpu.CompilerParams(dimension_semantics=sem, vmem_limit_bytes=vmem)


def _const_spec(shape):
    return pl.BlockSpec(shape, lambda *_: (0,) * len(shape), pipeline_mode=pl.Buffered(1))


def _row_spec(tm, width):
    return pl.BlockSpec((tm, width), lambda i: (i, 0))


def _mod_spec(arr, tm):
    if arr.shape[0] == 1:
        return pl.BlockSpec((1, arr.shape[1]), lambda i: (0, 0))
    return pl.BlockSpec((tm, arr.shape[1]), lambda i: (i, 0))


def _dot(a, b):
    return jnp.dot(a.astype(MXU_DT), b.astype(MXU_DT), preferred_element_type=F32)


_NN = (((1,), (0,)), ((), ()))
_NT = (((1,), (1,)), ((), ()))
_TN = (((0,), (0,)), ((), ()))


def _dg(a, b, dn):
    return lax.dot_general(a.astype(MXU_DT), b.astype(MXU_DT), dn, preferred_element_type=F32)


def _split(x):
    hi = x.astype(MXU_DT)
    lo = (x - hi.astype(F32)).astype(MXU_DT)
    return hi, lo


def _seg_sum(x, ones):
    hi, lo = _split(x)
    return (jnp.dot(hi, ones, preferred_element_type=F32)
            + jnp.dot(lo, ones, preferred_element_type=F32))


def _layer_norm(y, g, b):
    mu = jnp.mean(y, axis=-1, keepdims=True)
    yc = y - mu
    var = jnp.mean(yc * yc, axis=-1, keepdims=True)
    return yc * lax.rsqrt(var + LN_EPS) * g + b


def _ada_kernel(c_ref, w_ref, b_ref, o_ref):
    c = c_ref[...]
    o_ref[...] = _dot(c * jax.nn.sigmoid(c), w_ref[...]) + b_ref[...]


def _ada(c, w_ada, b_ada):
    n, d = c.shape
    nout = w_ada.shape[1]
    tn = 1024
    return pl.pallas_call(
        _ada_kernel,
        grid=(nout // tn,),
        in_specs=[pl.BlockSpec((n, d), lambda j: (0, 0)),
                  pl.BlockSpec((d, tn), lambda j: (0, j)),
                  pl.BlockSpec((1, tn), lambda j: (0, j))],
        out_specs=pl.BlockSpec((n, tn), lambda j: (0, j)),
        out_shape=jax.ShapeDtypeStruct((n, nout), F32),
        compiler_params=_cparams("parallel"),
    )(c, w_ada, b_ada)


def _ffn_kernel(x_ref, sh_ref, sc_ref, gt_ref, wg_ref, wu_ref, wd_ref, lg_ref, lb_ref, o_ref, *, n_ff):
    x = x_ref[...]
    h = (x * (1.0 + sc_ref[...]) + sh_ref[...]).astype(MXU_DT)
    step = D_FF // n_ff
    f = None
    for j in range(n_ff):
        cs = slice(j * step, (j + 1) * step)
        g = jnp.dot(h, wg_ref[:, cs], preferred_element_type=F32)
        u = jnp.dot(h, wu_ref[:, cs], preferred_element_type=F32)
        a = (g * jax.nn.sigmoid(g) * u).astype(MXU_DT)
        part = jnp.dot(a, wd_ref[cs, :], preferred_element_type=F32)
        f = part if f is None else f + part
    y = DN_ALPHA * x + (0.5 * gt_ref[...]) * f
    o_ref[...] = _layer_norm(y, lg_ref[...], lb_ref[...])


def _ffn(x, shift, scale, gate, wg, wu, wd, ln_g, ln_b, tm):
    t = x.shape[0]
    return pl.pallas_call(
        functools.partial(_ffn_kernel, n_ff=FFN_CHUNKS),
        grid=(t // tm,),
        in_specs=[_row_spec(tm, D_MODEL), _mod_spec(shift, tm), _mod_spec(scale, tm), _mod_spec(gate, tm),
                  _const_spec((D_MODEL, D_FF)), _const_spec((D_MODEL, D_FF)), _const_spec((D_FF, D_MODEL)),
                  _const_spec((1, D_MODEL)), _const_spec((1, D_MODEL))],
        out_specs=_row_spec(tm, D_MODEL),
        out_shape=jax.ShapeDtypeStruct((t, D_MODEL), F32),
        compiler_params=_cparams("parallel"),
    )(x, shift, scale, gate, wg, wu, wd, ln_g, ln_b)


def _rotate(x, cos, s_up, s_dn):
    return x * cos + pltpu.roll(x, LANES - ROT_DIM // 2, 1) * s_up + pltpu.roll(x, ROT_DIM // 2, 1) * s_dn


def _inproj_kernel(x_ref, sh_ref, sc_ref, w_ref, cos_ref, sup_ref, sdn_ref,
                   pr_ref, q_ref, k_ref, v_ref, gr_ref, gm_ref, kb_ref, vt_ref, km_ref, *, q_scale):
    h = (x_ref[...] * (1.0 + sc_ref[...]) + sh_ref[...]).astype(MXU_DT)
    tm = h.shape[0]

    def proj(lo, width):
        return jnp.dot(h, w_ref[:, lo:lo + width], preferred_element_type=F32)

    pr_ref[...] = proj(0, R_PROJ)
    cos, s_up, s_dn = cos_ref[...], sup_ref[...], sdn_ref[...]
    q = proj(R_PROJ, WIDTH)
    k = proj(R_PROJ + WIDTH, WIDTH)
    k_rot = []
    for j in range(WIDTH // LANES):
        ls = slice(j * LANES, (j + 1) * LANES)
        qj = _rotate(q[:, ls], cos, s_up, s_dn)
        kj = _rotate(k[:, ls], cos, s_up, s_dn)
        q_ref[:, ls] = (qj * q_scale).astype(MXU_DT)
        kb_ref[:, ls] = kj.astype(MXU_DT)
        k_rot.append(kj)
        if km_ref is not None:
            km_ref[0, :, ls] = jnp.sum(kj.reshape(tm // MOBA_BLOCK, MOBA_BLOCK, LANES), axis=1) * (1.0 / MOBA_BLOCK)
    k_rot = jnp.concatenate(k_rot, axis=1)
    v = proj(R_PROJ + 2 * WIDTH, WIDTH)
    if vt_ref is None:
        k_ref[...] = k_rot
        v_ref[...] = v
    else:
        v_t = v.T
        k_ref[...] = k_rot.T
        v_ref[...] = v_t
        for head in range(N_HEADS):
            base = head * VT_ROWS
            vt_ref[base:base + HEAD_DIM, :] = v_t[head * HEAD_DIM:(head + 1) * HEAD_DIM, :].astype(MXU_DT)
            vt_ref[base + HEAD_DIM:base + VT_ROWS, :] = jnp.ones((VT_ROWS - HEAD_DIM, tm), MXU_DT)
    gr_ref[...] = proj(R_PROJ + 3 * WIDTH, D_MODEL)
    gm_ref[...] = proj(R_PROJ + 3 * WIDTH + D_MODEL, D_MODEL)


def _inproj_prompt_kernel(*refs):
    _inproj_kernel(*refs, q_scale=HEAD_DIM ** -0.5 * LOG2_E)


def _inproj_sample_kernel(*refs):
    _inproj_kernel(*refs, None, None, q_scale=HEAD_DIM ** -0.5)


def _inproj(x, shift, scale, w_in, cos, s_up, s_dn, tm, prompt):
    t = x.shape[0]
    out_shape = [jax.ShapeDtypeStruct((t, R_PROJ), F32),
                 jax.ShapeDtypeStruct((t, WIDTH), MXU_DT),
                 jax.ShapeDtypeStruct((t, WIDTH), F32),
                 jax.ShapeDtypeStruct((t, WIDTH), F32),
                 jax.ShapeDtypeStruct((t, D_MODEL), F32),
                 jax.ShapeDtypeStruct((t, D_MODEL), F32),
                 jax.ShapeDtypeStruct((t, WIDTH), MXU_DT)]
    out_specs = [_row_spec(tm, R_PROJ), _row_spec(tm, WIDTH), _row_spec(tm, WIDTH), _row_spec(tm, WIDTH),
                 _row_spec(tm, D_MODEL), _row_spec(tm, D_MODEL), _row_spec(tm, WIDTH)]
    if prompt:
        nblk = tm // MOBA_BLOCK
        vt_rows = N_HEADS * VT_ROWS
        out_shape[2] = out_shape[3] = jax.ShapeDtypeStruct((WIDTH, t), F32)
        out_specs[2] = out_specs[3] = pl.BlockSpec((WIDTH, tm), lambda i: (0, i))
        out_shape += [jax.ShapeDtypeStruct((vt_rows, t), MXU_DT),
                      jax.ShapeDtypeStruct((t // tm, nblk, WIDTH), F32)]
        out_specs += [pl.BlockSpec((vt_rows, tm), lambda i: (0, i)),
                      pl.BlockSpec((1, nblk, WIDTH), lambda i: (i, 0, 0))]
    return pl.pallas_call(
        _inproj_prompt_kernel if prompt else _inproj_sample_kernel,
        grid=(t // tm,),
        in_specs=[_row_spec(tm, D_MODEL), _mod_spec(shift, tm), _mod_spec(scale, tm),
                  _const_spec((D_MODEL, IN_PROJ)),
                  _row_spec(tm, LANES), _row_spec(tm, LANES), _row_spec(tm, LANES)],
        out_specs=out_specs,
        out_shape=out_shape,
        compiler_params=_cparams("parallel"),
    )(x, shift, scale, w_in, cos, s_up, s_dn)


def _rope_tables(pos):
    half = ROT_DIM // 2
    inv = ROPE_THETA ** (-jnp.arange(half, dtype=F32) * 2.0 / ROT_DIM)
    ang = pos.astype(F32)[:, None] * inv[None, :]
    cos, sin = jnp.cos(ang), jnp.sin(ang)
    t = pos.shape[0]
    zeros = lambda n: jnp.zeros((t, n), F32)
    per_head = lambda parts: jnp.tile(jnp.concatenate(parts, axis=1), (1, LANES // HEAD_DIM))
    c = per_head([cos, cos, jnp.ones((t, HEAD_DIM - ROT_DIM), F32)])
    s_up = per_head([-sin, zeros(HEAD_DIM - half)])
    s_dn = per_head([zeros(half), sin, zeros(HEAD_DIM - ROT_DIM)])
    return c, s_up, s_dn


def _rwkv_pointwise(pr, prev, mu, wcat, w0, a0, g2, k_k, k_a, ones):
    xs = pr + (prev - pr) * mu
    r = xs[:, 0:WIDTH]
    k = xs[:, WIDTH:2 * WIDTH]
    v = xs[:, 2 * WIDTH:3 * WIDTH]
    la = xs[:, 3 * WIDTH:3 * WIDTH + LORA_W + LORA_A]
    gl = xs[:, 3 * WIDTH + LORA_W + LORA_A:R_PROJ]
    lane = lax.broadcasted_iota(jnp.int32, la.shape, 1)
    la = jnp.where(lane < LORA_W, jnp.tanh(la), la)
    wa = _dot(la, wcat)
    w = w0 + wa[:, :WIDTH]
    a = jax.nn.sigmoid(a0 + wa[:, WIDTH:])
    g = _dot(jax.nn.sigmoid(gl), g2)
    z = -w
    softplus = jnp.maximum(z, 0.0) + jnp.log1p(jnp.exp(-jnp.abs(z)))
    log_decay = -jnp.exp(-softplus - 0.5)
    kk = k * k_k
    kk = kk / jnp.maximum(jnp.sqrt(_seg_sum(kk * kk, ones)), 1e-12)
    k_eff = k * (1.0 + (a - 1.0) * k_a)
    return r, k_eff, v, kk, kk * a, log_decay, g


def _rwkv_post(y, r, k_eff, v, g, r_k, lnx_g, lnx_b, ones):
    inv = 1.0 / HEAD_DIM
    yc = y - _seg_sum(y, ones) * inv
    yv = _seg_sum(yc * yc, ones) * inv
    yn = yc * lax.rsqrt(yv + GN_EPS) * lnx_g + lnx_b
    bonus = _seg_sum(r * k_eff * r_k, ones) * v
    return (yn + bonus) * g


def _rwkv_prompt_kernel(pr_ref, shift0_ref, s0_ref, mu_ref, wcat_ref, w0_ref, a0_ref, g2_ref, kk_ref, ka_ref,
                        rk_ref, lg_ref, lb_ref, ones_ref, tri_ref, blk_ref,
                        out_ref, sfin_ref, shift_ref,
                        prev_scr, st_scr, rt_scr, kp_scr, bt_scr, kt_scr, v_scr, bc_scr, kc_scr, pc_scr, y_scr):
    i = pl.program_id(0)

    @pl.when(i == 0)
    def _():
        prev_scr[...] = shift0_ref[...]
        st_scr[...] = s0_ref[...]

    pr = pr_ref[...]
    tm = pr.shape[0]
    row = lax.broadcasted_iota(jnp.int32, (tm, 1), 0)
    prev = jnp.where(row == 0, prev_scr[...], pltpu.roll(pr, 1, 0))
    ones = ones_ref[...]
    r, k_eff, v, kk, b, lw, g = _rwkv_pointwise(pr, prev, mu_ref[...], wcat_ref[...], w0_ref[...], a0_ref[...],
                                                g2_ref[...], kk_ref[...], ka_ref[...], ones)
    l_hi = lw.astype(MXU_DT)
    rem = lw - l_hi.astype(F32)
    l_mid = rem.astype(MXU_DT)
    l_lo = (rem - l_mid.astype(F32)).astype(MXU_DT)
    cum = lambda m: (jnp.dot(m, l_hi, preferred_element_type=F32)
                     + (jnp.dot(m, l_mid, preferred_element_type=F32) + jnp.dot(m, l_lo, preferred_element_type=F32)))
    L = cum(tri_ref[...])
    LC = cum(blk_ref[...])
    e_inv = jnp.exp(-L)
    e_end = jnp.exp(LC - L)
    rt_scr[...] = r * jnp.exp(L)
    kp_scr[...] = kk * jnp.exp(L - lw)
    bt_scr[...] = b * e_inv
    kt_scr[...] = k_eff * e_inv
    bc_scr[...] = b * e_end
    kc_scr[...] = k_eff * e_end
    pc_scr[...] = jnp.exp(LC)
    v_scr[...] = v

    ri = lax.broadcasted_iota(jnp.int32, (CHUNK, CHUNK), 0)
    ci = lax.broadcasted_iota(jnp.int32, (CHUNK, CHUNK), 1)
    strict, incl, eye = ri > ci, ri >= ci, ri == ci

    n_chunks = tm // CHUNK
    probs = [(c, h) for c in range(n_chunks) for h in range(N_HEADS)]
    each = lambda fn, *lists: [fn(*args) for args in zip(*lists)]

    def blocks(ref):
        return [ref[c * CHUNK:(c + 1) * CHUNK, h * HEAD_DIM:(h + 1) * HEAD_DIM] for c, h in probs]

    def blocks_t(ref):
        per_chunk = [ref[c * CHUNK:(c + 1) * CHUNK, :].T for c in range(n_chunks)]
        return [per_chunk[c][h * HEAD_DIM:(h + 1) * HEAD_DIM, :] for c, h in probs]

    kp, rt, v_b = blocks(kp_scr), blocks(rt_scr), blocks(v_scr)
    bk = each(lambda a, b_: jnp.concatenate([a, b_], axis=0), blocks(bt_scr), blocks(kt_scr))
    bk = each(lambda b_: b_.astype(MXU_DT), bk)
    v_m = each(lambda a: a.astype(MXU_DT), v_b)
    akk = each(lambda a, b_: _dg(a, b_, _NT), kp, bk)
    arr = each(lambda a, b_: _dg(a, b_, _NT), rt, bk)
    a_zb = [jnp.where(strict, a[:, :CHUNK], 0.0) for a in akk]
    a_vk = [jnp.where(strict, a[:, CHUNK:], 0.0) for a in akk]
    a_rb = [jnp.where(incl, a[:, :CHUNK], 0.0) for a in arr]
    a_rk = [jnp.where(incl, a[:, CHUNK:], 0.0) for a in arr]
    u0 = each(lambda a, w, v_: jnp.concatenate([a, _dg(w, v_, _NN)], axis=1), kp, a_vk, v_m)
    half = lambda s: (ri // (2 * s) == ci // (2 * s)) & (ri % (2 * s) >= s) & (ci % (2 * s) < s)
    t_inv = [jnp.where(eye, 1.0, 0.0) - jnp.where(half(1), a, 0.0) for a in a_zb]
    s_blk = 2
    while s_blk < CHUNK:
        lower = half(s_blk)
        t_inv = each(lambda t_, a: t_ - _dg(_dg(t_, jnp.where(lower, a, 0.0), _NN), t_, _NN), t_inv, a_zb)
        s_blk *= 2
    u = each(lambda t_, u_: _dg(t_, u_, _NN), t_inv, u0)
    u_m = each(lambda a: a.astype(MXU_DT), u)
    gmat = each(lambda a, u_: _dg(a, u_, _NN), a_rb, u_m)
    r_eff = each(lambda a, g_: a - g_[:, :HEAD_DIM], rt, gmat)
    y0 = each(lambda a, b_, g_: _dg(a, b_, _NN) - g_[:, HEAD_DIM:], a_rk, v_m, gmat)
    bu = each(lambda a, u_: _dg(a, u_, _NN), blocks_t(bc_scr), u_m)
    pc = [pc_scr[c * CHUNK:c * CHUNK + 1, h * HEAD_DIM:(h + 1) * HEAD_DIM] for c, h in probs]
    m_t = each(lambda p_, b_: jnp.where(eye, p_, 0.0) - b_[:, :HEAD_DIM], pc, bu)
    n_t = each(lambda a, v_, c_: _dg(a, v_, _NN) - c_[:, HEAD_DIM:], blocks_t(kc_scr), v_m, bu)
    st = [st_scr[h] for h in range(N_HEADS)]
    for c in range(n_chunks):
        sl = slice(c * N_HEADS, (c + 1) * N_HEADS)
        st_m = each(lambda a: a.astype(MXU_DT), st)
        y_scr[c * CHUNK:(c + 1) * CHUNK, :] = jnp.concatenate(
            each(lambda a, s_, b_: _dg(a, s_, _NN) + b_, r_eff[sl], st_m, y0[sl]), axis=1)
        st = each(lambda m_, s_, n_: _dg(m_, s_, _NN) + n_, m_t[sl], st_m, n_t[sl])
    for h in range(N_HEADS):
        st_scr[h] = st[h]
    out_ref[...] = _rwkv_post(y_scr[...], r, k_eff, v, g, rk_ref[...], lg_ref[...], lb_ref[...], ones)
    prev_scr[...] = pr[tm - 1:tm, :]
    shift_ref[...] = pr[tm - 1:tm, :]

    @pl.when(i == pl.num_programs(0) - 1)
    def _():
        for h in range(N_HEADS):
            sfin_ref[h] = st[h].T


def _rwkv_prompt(pr, shift0, s0, rp, tm):
    t = pr.shape[0]
    ch = lax.broadcasted_iota(jnp.int32, (tm, tm), 0) // CHUNK == lax.broadcasted_iota(jnp.int32, (tm, tm), 1) // CHUNK
    low = lax.broadcasted_iota(jnp.int32, (tm, tm), 0) >= lax.broadcasted_iota(jnp.int32, (tm, tm), 1)
    tri = (ch & low).astype(MXU_DT)
    blk = ch.astype(MXU_DT)
    vec = _const_spec((1, WIDTH))
    tile = pltpu.VMEM((tm, WIDTH), F32)
    return pl.pallas_call(
        _rwkv_prompt_kernel,
        grid=(t // tm,),
        in_specs=[_row_spec(tm, R_PROJ), _const_spec((1, R_PROJ)), _const_spec((N_HEADS, HEAD_DIM, HEAD_DIM)),
                  _const_spec((1, R_PROJ)), _const_spec((LORA_W + LORA_A, 2 * WIDTH)), vec, vec,
                  _const_spec((LORA_G, WIDTH)), vec, vec, vec, vec, vec,
                  _const_spec((WIDTH, WIDTH)), _const_spec((tm, tm)), _const_spec((tm, tm))],
        out_specs=[_row_spec(tm, WIDTH),
                   pl.BlockSpec((N_HEADS, HEAD_DIM, HEAD_DIM), lambda i: (0, 0, 0)),
                   pl.BlockSpec((1, R_PROJ), lambda i: (0, 0))],
        out_shape=[jax.ShapeDtypeStruct((t, WIDTH), F32),
                   jax.ShapeDtypeStruct((N_HEADS, HEAD_DIM, HEAD_DIM), F32),
                   jax.ShapeDtypeStruct((1, R_PROJ), F32)],
        scratch_shapes=[pltpu.VMEM((1, R_PROJ), F32), pltpu.VMEM((N_HEADS, HEAD_DIM, HEAD_DIM), F32)] + [tile] * 9,
        compiler_params=_cparams("arbitrary"),
    )(pr, shift0, jnp.swapaxes(s0, 1, 2), rp["mu"], rp["wcat"], rp["w0"], rp["a0"], rp["g2"], rp["k_k"], rp["k_a"],
      rp["r_k"], rp["lnx_g"], rp["lnx_b"], rp["ones"], tri, blk)


def _rwkv_step_kernel(pr_ref, shift_ref, s_ref, mu_ref, wcat_ref, w0_ref, a0_ref, g2_ref, kk_ref, ka_ref,
                      rk_ref, lg_ref, lb_ref, ones_ref, out_ref, snew_ref):
    ones = ones_ref[...]
    r, k_eff, v, kk, b, lw, g = _rwkv_pointwise(pr_ref[...], shift_ref[...], mu_ref[...], wcat_ref[...],
                                                w0_ref[...], a0_ref[...], g2_ref[...], kk_ref[...], ka_ref[...], ones)
    decay = jnp.exp(lw)
    eye = (lax.broadcasted_iota(jnp.int32, (HEAD_DIM, HEAD_DIM), 0)
           == lax.broadcasted_iota(jnp.int32, (HEAD_DIM, HEAD_DIM), 1))[None]
    ys = []
    for h in range(N_HEADS):
        over_v = lambda z: z[:, None, h * HEAD_DIM:(h + 1) * HEAD_DIM]
        s = s_ref[:, h]
        s_kk = jnp.sum(s * over_v(kk), axis=2, keepdims=True)
        v_col = jnp.sum(jnp.where(eye, over_v(v), 0.0), axis=2, keepdims=True)
        s_new = s * over_v(decay) - s_kk * over_v(b) + v_col * over_v(k_eff)
        y_col = jnp.sum(s_new * over_v(r), axis=2, keepdims=True)
        ys.append(jnp.sum(jnp.where(eye, y_col, 0.0), axis=1))
        snew_ref[:, h] = s_new
    y = jnp.concatenate(ys, axis=1)
    out_ref[...] = _rwkv_post(y, r, k_eff, v, g, rk_ref[...], lg_ref[...], lb_ref[...], ones)


def _rwkv_step(pr, shift, state, rp, nb):
    n = pr.shape[0]
    vec = _const_spec((1, WIDTH))
    state_spec = pl.BlockSpec((nb, N_HEADS, HEAD_DIM, HEAD_DIM), lambda i: (i, 0, 0, 0))
    return pl.pallas_call(
        _rwkv_step_kernel,
        grid=(n // nb,),
        in_specs=[_row_spec(nb, R_PROJ), _row_spec(nb, R_PROJ), state_spec,
                  _const_spec((1, R_PROJ)), _const_spec((LORA_W + LORA_A, 2 * WIDTH)), vec, vec,
                  _const_spec((LORA_G, WIDTH)), vec, vec, vec, vec, vec, _const_spec((WIDTH, WIDTH))],
        out_specs=[_row_spec(nb, WIDTH), state_spec],
        out_shape=[jax.ShapeDtypeStruct((n, WIDTH), F32), jax.ShapeDtypeStruct(state.shape, F32)],
        compiler_params=_cparams("parallel"),
    )(pr, shift, state, rp["mu"], rp["wcat"], rp["w0"], rp["a0"], rp["g2"], rp["k_k"], rp["k_a"], rp["r_k"],
      rp["lnx_g"], rp["lnx_b"], rp["ones"])


def _moba_prompt_kernel(q_ref, km_ref, k_ref, vt_ref, o_ref, qm_scr, sel_scr, m_scr, acc_scr,
                        s0_scr, s1_scr, cm_scr):
    c = pl.program_id(0)
    tq = q_ref.shape[0]
    nblk = km_ref.shape[0]
    lane = lax.broadcasted_iota(jnp.int32, (1, LANES), 1)
    blk_id = lax.broadcasted_iota(jnp.int32, (nblk, tq), 0)
    key_i = lax.broadcasted_iota(jnp.int32, (MOBA_BLOCK, tq), 0)
    qry_i = lax.broadcasted_iota(jnp.int32, (MOBA_BLOCK, tq), 1)

    def head_lanes(h):
        return (lane < HEAD_DIM) if h % 2 == 0 else (lane >= HEAD_DIM)

    def values(h, b):
        rows = slice(h * VT_ROWS, (h + 1) * VT_ROWS)
        return vt_ref[rows, pl.ds(pl.multiple_of(b * MOBA_BLOCK, MOBA_BLOCK), MOBA_BLOCK)]

    def keys(h, b):
        ps = slice((h // 2) * LANES, (h // 2 + 1) * LANES)
        return k_ref[pl.ds(pl.multiple_of(b * MOBA_BLOCK, MOBA_BLOCK), MOBA_BLOCK), ps]

    for h in range(N_HEADS):
        ps = slice((h // 2) * LANES, (h // 2 + 1) * LANES)
        qm = jnp.where(head_lanes(h), q_ref[:, ps].astype(F32), 0.0).astype(MXU_DT)
        qm_scr[h] = qm
        gate = lax.dot_general(km_ref[:, ps], qm, _NT, preferred_element_type=F32)
        gate = jnp.where(blk_id < c, gate, NEG_INF)
        sel = jnp.zeros(gate.shape, jnp.bool_)
        for _ in range(MOBA_TOPK):
            top = jnp.max(gate, axis=0, keepdims=True)
            idx = jnp.min(jnp.where(gate == top, blk_id, nblk), axis=0, keepdims=True)
            hit = blk_id == idx
            sel = sel | hit
            gate = jnp.where(hit, NEG_INF, gate)
        sel_scr[h] = sel.astype(F32)
        s = lax.dot_general(keys(h, c), qm, _NT, preferred_element_type=F32)
        s = jnp.where(key_i <= qry_i, s, NEG_INF)
        m = jnp.max(s, axis=0, keepdims=True)
        p = jnp.exp2(s - m).astype(MXU_DT)
        m_scr[h] = m
        acc_scr[h] = jnp.dot(values(h, c), p, preferred_element_type=F32)

    def scores(b, s_scr, slot):
        for h in range(N_HEADS):
            s = lax.dot_general(keys(h, b), qm_scr[h], _NT, preferred_element_type=F32)
            s_scr[h] = s
            cm_scr[slot, h] = jnp.max(s, axis=0, keepdims=True)

    def absorb(b, s_scr, slot):
        for h in range(N_HEADS):
            picked = sel_scr[h, pl.ds(b, 1), :] > 0.0
            m_old = m_scr[h]
            m_new = jnp.where(picked, jnp.maximum(m_old, cm_scr[slot, h]), m_old)
            p = jnp.exp2(s_scr[h] - jnp.where(picked, m_new, jnp.inf)).astype(MXU_DT)
            acc_scr[h] = acc_scr[h] * jnp.exp2(m_old - m_new) + jnp.dot(values(h, b), p, preferred_element_type=F32)
            m_scr[h] = m_new

    def block_pair(i, carry):
        scores(2 * i, s0_scr, 0)
        scores(2 * i + 1, s1_scr, 1)
        absorb(2 * i, s0_scr, 0)
        absorb(2 * i + 1, s1_scr, 1)
        return carry

    lax.fori_loop(0, c // 2, block_pair, 0)

    @pl.when(c % 2 == 1)
    def _():
        scores(c - 1, s0_scr, 0)
        absorb(c - 1, s0_scr, 0)

    for pair in range(N_HEADS // 2):
        outs = []
        for h in (2 * pair, 2 * pair + 1):
            a = acc_scr[h]
            outs.append((a[:HEAD_DIM] / a[HEAD_DIM:HEAD_DIM + 1]).T)
        o_ref[:, pair * LANES:(pair + 1) * LANES] = jnp.concatenate(outs, axis=1)


def _moba_prompt(q, kmean, kb, vt):
    t = q.shape[0]
    nblk = kmean.shape[0]
    tq = MOBA_BLOCK
    return pl.pallas_call(
        _moba_prompt_kernel,
        grid=(t // tq,),
        in_specs=[_row_spec(tq, WIDTH), _const_spec((nblk, WIDTH)), _const_spec((t, WIDTH)),
                  _const_spec(vt.shape)],
        out_specs=_row_spec(tq, WIDTH),
        out_shape=jax.ShapeDtypeStruct((t, WIDTH), F32),
        scratch_shapes=[pltpu.VMEM((N_HEADS, tq, LANES), MXU_DT), pltpu.VMEM((N_HEADS, nblk, tq), F32),
                        pltpu.VMEM((N_HEADS, 1, tq), F32), pltpu.VMEM((N_HEADS, VT_ROWS, tq), F32),
                        pltpu.VMEM((N_HEADS, MOBA_BLOCK, tq), F32), pltpu.VMEM((N_HEADS, MOBA_BLOCK, tq), F32),
                        pltpu.VMEM((2, N_HEADS, 1, tq), F32)],
        compiler_params=_cparams("parallel"),
    )(q, kmean, kb, vt)


def _moba_sample_score_kernel(pt_ref, q_ref, knew_ref, *refs):
    del pt_ref
    pg = SAMPLE_PAGES_PER_STEP
    pages = refs[:pg]
    p_ref, pself_ref, idx_ref, sc_scr, gate_scr = refs[pg:]
    j = pl.program_id(1)
    nblk = gate_scr.shape[1]
    rnd = lambda z: z.astype(MXU_DT).astype(F32)
    q = q_ref[0]
    for t in range(pg // 2):
        halves = (pages[2 * t][0], pages[2 * t + 1][0])
        sc = jnp.concatenate([jnp.sum(page * q, axis=1, keepdims=True) for page in halves], axis=2)
        gate = jnp.broadcast_to(jnp.sum(sc, axis=2, keepdims=True), (N_HEADS, 1, LANES))
        blk = pl.ds(j * (pg // 2) + t, 1)
        for h in range(N_HEADS):
            sc_scr[h, blk, :] = sc[h]
            gate_scr[h, blk, :] = gate[h]

    @pl.when(j == pl.num_programs(1) - 1)
    def _():
        gate = gate_scr[...]
        blk_id = lax.broadcasted_iota(jnp.int32, gate.shape, 1)
        sel = jnp.zeros(gate.shape, jnp.bool_)
        for r in range(MOBA_TOPK):
            top = jnp.max(gate, axis=1, keepdims=True)
            idx = jnp.min(jnp.where(gate == top, blk_id, nblk), axis=1, keepdims=True)
            hit = blk_id == idx
            sel = sel | hit
            gate = jnp.where(hit, NEG_INF, gate)
            idx_ref[0, r] = idx
        sel_f = jnp.where(sel, 1.0, 0.0)
        picked = jnp.concatenate([sel_f] * (MOBA_BLOCK // LANES), axis=2) > 0.0
        s = jnp.where(picked, sc_scr[...], NEG_INF)
        s_self = jnp.sum(q * rnd(knew_ref[0]), axis=1, keepdims=True)
        m = jnp.maximum(jnp.max(jnp.max(s, axis=2, keepdims=True), axis=1, keepdims=True), s_self)
        p = jnp.exp(s - m[:, :, :1])
        p_self = jnp.exp(s_self - m)
        inv = 1.0 / (jnp.sum(jnp.sum(p, axis=2, keepdims=True), axis=1, keepdims=True) + p_self)
        p_ref[0] = p * inv[:, :, :1]
        pself_ref[0] = p_self * inv


def _moba_sample_scores(q_lanes, knew_lanes, cache_kt, page_table):
    n, n_pages = page_table.shape
    pg = SAMPLE_PAGES_PER_STEP
    nblk = n_pages * PAGE_SIZE // MOBA_BLOCK
    page_block = (1, N_HEADS, HEAD_DIM, PAGE_SIZE)

    def page_spec(e):
        return pl.BlockSpec(page_block, lambda i, j, pt: (pt[i * n_pages + j * pg + e], 0, 0, 0))

    per_seq = pl.BlockSpec(page_block, lambda i, j, pt: (i, 0, 0, 0))
    grid_spec = pltpu.PrefetchScalarGridSpec(
        num_scalar_prefetch=1,
        grid=(n, n_pages // pg),
        in_specs=[per_seq, per_seq] + [page_spec(e) for e in range(pg)],
        out_specs=[pl.BlockSpec((1, N_HEADS, nblk, MOBA_BLOCK), lambda i, j, pt: (i, 0, 0, 0)),
                   pl.BlockSpec((1, N_HEADS, 1, LANES), lambda i, j, pt: (i, 0, 0, 0)),
                   pl.BlockSpec((1, MOBA_TOPK, N_HEADS, 1, LANES), lambda i, j, pt: (i, 0, 0, 0, 0))],
        scratch_shapes=[pltpu.VMEM((N_HEADS, nblk, MOBA_BLOCK), F32), pltpu.VMEM((N_HEADS, nblk, LANES), F32)],
    )
    return pl.pallas_call(
        _moba_sample_score_kernel,
        grid_spec=grid_spec,
        out_shape=[jax.ShapeDtypeStruct((n, N_HEADS, nblk, MOBA_BLOCK), F32),
                   jax.ShapeDtypeStruct((n, N_HEADS, 1, LANES), F32),
                   jax.ShapeDtypeStruct((n, MOBA_TOPK, N_HEADS, 1, LANES), jnp.int32)],
        compiler_params=_cparams("parallel", "arbitrary"),
    )(page_table.reshape(-1), q_lanes, knew_lanes, *([cache_kt] * pg))


def _moba_sample_value_kernel(phys_ref, blk_ref, p_ref, pself_ref, vnew_ref, *refs):
    del phys_ref
    pages_per_block = MOBA_BLOCK // PAGE_SIZE
    v_refs, o_ref = refs[:-1], refs[-1]
    i = pl.program_id(0)
    rnd = lambda z: z.astype(MXU_DT).astype(F32)
    for h in range(N_HEADS):
        acc = rnd(pself_ref[0, h][:, :HEAD_DIM]) * rnd(vnew_ref[0, h])
        for r in range(MOBA_TOPK):
            slot = h * MOBA_TOPK + r
            p_row = p_ref[0, h, pl.ds(blk_ref[i * N_HEADS * MOBA_TOPK + slot], 1), :]
            for e in range(pages_per_block):
                p_rows = jnp.broadcast_to(p_row[:, e * PAGE_SIZE:(e + 1) * PAGE_SIZE], (N_HEADS, PAGE_SIZE))
                v_t = v_refs[slot * pages_per_block + e][0, 0]
                acc = acc + _dg(p_rows, v_t, _NT)[0:1]
        o_ref[0, h] = acc


def _moba_sample_values(p, p_self, v_new, cache_vt, phys, blocks):
    n, _, nblk, _ = p.shape
    pages_per_block = MOBA_BLOCK // PAGE_SIZE
    per_seq = N_HEADS * MOBA_TOPK * pages_per_block

    def page_spec(h, e):
        return pl.BlockSpec((1, 1, HEAD_DIM, PAGE_SIZE),
                            lambda i, ph, bk: (ph[i * per_seq + h * MOBA_TOPK * pages_per_block + e], h, 0, 0))

    whole = lambda shape: pl.BlockSpec((1,) + shape, lambda i, ph, bk: (i,) + (0,) * len(shape))
    grid_spec = pltpu.PrefetchScalarGridSpec(
        num_scalar_prefetch=2,
        grid=(n,),
        in_specs=[whole((N_HEADS, nblk, MOBA_BLOCK)), whole((N_HEADS, 1, LANES)), whole((N_HEADS, 1, HEAD_DIM))]
        + [page_spec(h, e) for h in range(N_HEADS) for e in range(MOBA_TOPK * pages_per_block)],
        out_specs=whole((N_HEADS, 1, HEAD_DIM)),
    )
    return pl.pallas_call(
        _moba_sample_value_kernel,
        grid_spec=grid_spec,
        out_shape=jax.ShapeDtypeStruct((n, N_HEADS, 1, HEAD_DIM), F32),
        compiler_params=_cparams("parallel"),
    )(phys, blocks, p, p_self, v_new.reshape(n, N_HEADS, 1, HEAD_DIM), *([cache_vt] * per_seq))


def _moba_sample(q, k_new, v_new, cache_kt, cache_vt, page_table):
    n = q.shape[0]
    lanes = lambda z: jnp.broadcast_to(z.astype(F32).reshape(n, N_HEADS, HEAD_DIM, 1),
                                       (n, N_HEADS, HEAD_DIM, PAGE_SIZE))
    p, p_self, idx = _moba_sample_scores(lanes(q), lanes(k_new), cache_kt, page_table)
    pages_per_block = MOBA_BLOCK // PAGE_SIZE
    blocks = jnp.transpose(idx[..., 0, 0], (0, 2, 1))
    logi = (blocks[..., None] * pages_per_block + jnp.arange(pages_per_block, dtype=jnp.int32)).reshape(
        n, N_HEADS, -1)
    phys = jnp.take_along_axis(page_table[:, None, :], logi, axis=2)
    out = _moba_sample_values(p, p_self, v_new, cache_vt, phys.reshape(-1), blocks.reshape(-1))
    return out.reshape(n, WIDTH)


def _merge_kernel(x_ref, r_ref, a_ref, gr_ref, gm_ref, gt_ref, wr_ref, wm_ref, wo_ref, lg_ref, lb_ref, o_ref):
    br = jax.nn.sigmoid(gr_ref[...]) * _dot(r_ref[...], wr_ref[...])
    bm = jax.nn.sigmoid(gm_ref[...]) * _dot(a_ref[...], wm_ref[...])
    merged = _dot(br + bm, wo_ref[...])
    o_ref[...] = _layer_norm(DN_ALPHA * x_ref[...] + gt_ref[...] * merged, lg_ref[...], lb_ref[...])


def _merge(x, r_out, att, g_r, g_m, gate, wr, wm, wo, ln_g, ln_b, tm):
    t = x.shape[0]
    return pl.pallas_call(
        _merge_kernel,
        grid=(t // tm,),
        in_specs=[_row_spec(tm, D_MODEL), _row_spec(tm, WIDTH), _row_spec(tm, WIDTH), _row_spec(tm, D_MODEL),
                  _row_spec(tm, D_MODEL), _mod_spec(gate, tm),
                  _const_spec((WIDTH, D_MODEL)), _const_spec((WIDTH, D_MODEL)), _const_spec((D_MODEL, D_MODEL)),
                  _const_spec((1, D_MODEL)), _const_spec((1, D_MODEL))],
        out_specs=_row_spec(tm, D_MODEL),
        out_shape=jax.ShapeDtypeStruct((t, D_MODEL), F32),
        compiler_params=_cparams("parallel"),
    )(x, r_out, att, g_r, g_m, gate, wr, wm, wo, ln_g, ln_b)


def _tile(t, want):
    return want if t % want == 0 else t


def _layer(x, mod, pos, wts, rp, rwkv_fn, moba_fn, prompt):
    t = x.shape[0]
    tm = _tile(t, 512)
    row = lambda z: z.reshape(1, -1)
    x1 = _ffn(x, mod[0], mod[1], mod[2], wts["ffn1_gate"], wts["ffn1_up"], wts["ffn1_down"],
              row(wts["ln1_g"]), row(wts["ln1_b"]), _tile(t, FFN_ROWS))
    cos, s_up, s_dn = _rope_tables(pos)
    proj = _inproj(x1, mod[3], mod[4], wts["w_in"], cos, s_up, s_dn, tm, prompt)
    pr, q, k_new, v_new, g_r, g_m, kb = proj[:7]
    r_out, wkv_new, shift_new = rwkv_fn(pr)
    att = moba_fn(q, k_new, v_new, kb, proj[7:])
    x2 = _merge(x1, r_out, att, g_r, g_m, mod[5], wts["w_br_rwkv"], wts["w_br_moba"], wts["w_out"],
                row(wts["ln2_g"]), row(wts["ln2_b"]), tm)
    y = _ffn(x2, mod[6], mod[7], mod[8], wts["ffn2_gate"], wts["ffn2_up"], wts["ffn2_down"],
             row(wts["ln3_g"]), row(wts["ln3_b"]), _tile(t, FFN_ROWS))
    return y, k_new, v_new, wkv_new, shift_new


def kernel(x_prompt, x_sample, cache_k, cache_v, state_rwkv_wkv, state_rwkv_shift, page_table, c_prompt, c_sample, w_ada, b_ada, ffn1_gate, ffn1_up, ffn1_down, ln1_g, ln1_b, w_in, mu_shift, rwkv_w0, rwkv_w2, rwkv_a0, rwkv_a2, rwkv_g2, rwkv_k_k, rwkv_k_a, rwkv_r_k, rwkv_lnx_g, rwkv_lnx_b, w_br_rwkv, w_br_moba, w_out, ln2_g, ln2_b, ffn2_gate, ffn2_up, ffn2_down, ln3_g, ln3_b):
    assert x_prompt.shape[0] == 1 and x_sample.shape[1] == 1 and w_ada.shape[0] == 1
    tp = x_prompt.shape[1]
    ns = x_sample.shape[0]
    n_pages = page_table.shape[1]
    past_len = n_pages * PAGE_SIZE
    assert tp % MOBA_BLOCK == 0 and past_len % MOBA_BLOCK == 0 and n_pages % SAMPLE_PAGES_PER_STEP == 0

    bf = lambda z: z[0].astype(MXU_DT)
    wts = {"ffn1_gate": bf(ffn1_gate), "ffn1_up": bf(ffn1_up), "ffn1_down": bf(ffn1_down),
           "ffn2_gate": bf(ffn2_gate), "ffn2_up": bf(ffn2_up), "ffn2_down": bf(ffn2_down),
           "w_in": bf(w_in), "w_br_rwkv": bf(w_br_rwkv), "w_br_moba": bf(w_br_moba), "w_out": bf(w_out),
           "ln1_g": ln1_g[0], "ln1_b": ln1_b[0], "ln2_g": ln2_g[0], "ln2_b": ln2_b[0],
           "ln3_g": ln3_g[0], "ln3_b": ln3_b[0]}
    zw = jnp.zeros((LORA_W, WIDTH), MXU_DT)
    head_of = jnp.arange(WIDTH) // HEAD_DIM
    row = lambda z: z.reshape(1, -1)
    rp = {"mu": row(mu_shift[0]),
          "wcat": jnp.concatenate([jnp.concatenate([bf(rwkv_w2), zw], axis=1),
                                   jnp.concatenate([zw, bf(rwkv_a2)], axis=1)], axis=0),
          "w0": row(rwkv_w0[0]), "a0": row(rwkv_a0[0]), "g2": bf(rwkv_g2),
          "k_k": row(rwkv_k_k[0]), "k_a": row(rwkv_k_a[0]), "r_k": row(rwkv_r_k[0]),
          "lnx_g": row(rwkv_lnx_g[0]), "lnx_b": row(rwkv_lnx_b[0]),
          "ones": (head_of[:, None] == head_of[None, :]).astype(MXU_DT)}

    n_mod = ((1 + ns + 7) // 8) * 8
    c_all = jnp.concatenate([c_prompt, c_sample, jnp.zeros((n_mod - 1 - ns, D_MODEL), F32)], axis=0)
    ada = _ada(c_all, w_ada[0], b_ada)
    mod_p = [ada[0:1, j * D_MODEL:(j + 1) * D_MODEL] for j in range(9)]
    mod_s = [ada[1:1 + ns, j * D_MODEL:(j + 1) * D_MODEL] for j in range(9)]

    def rwkv_p(pr):
        r_out, s_fin, shift = _rwkv_prompt(pr, jnp.zeros((1, R_PROJ), F32),
                                           jnp.zeros((N_HEADS, HEAD_DIM, HEAD_DIM), F32), rp, _tile(tp, 256))
        return r_out, s_fin, shift

    def moba_p(q, k_new, v_new, kb, extra):
        vt, kmean = extra
        return _moba_prompt(q, kmean.reshape(-1, WIDTH).astype(MXU_DT), kb, vt)

    y_p, k_p, v_p, wkv_p, shift_p = _layer(x_prompt[0], mod_p, jnp.arange(tp, dtype=jnp.int32), wts, rp,
                                           rwkv_p, moba_p, True)

    ck, cv = jnp.transpose(cache_k[0], (0, 2, 3, 1)), jnp.transpose(cache_v[0], (0, 2, 3, 1))

    def rwkv_s(pr):
        r_out, s_new = _rwkv_step(pr, state_rwkv_shift[0], state_rwkv_wkv[0], rp, _tile(ns, 8))
        return r_out, s_new, pr

    def moba_s(q, k_new, v_new, kb, extra):
        return _moba_sample(q, k_new, v_new, ck, cv, page_table)

    y_s, k_s, v_s, wkv_s, shift_s = _layer(x_sample[:, 0], mod_s, jnp.full((ns,), past_len, jnp.int32), wts, rp,
                                           rwkv_s, moba_s, False)

    heads = lambda z, n, t: z.reshape(1, n, t, N_HEADS, HEAD_DIM)
    heads_t = lambda z: jnp.transpose(z.reshape(N_HEADS, HEAD_DIM, tp), (2, 0, 1))[None, None]
    return (y_p[None], y_s[:, None],
            heads_t(k_p), heads_t(v_p), heads(k_s, ns, 1), heads(v_s, ns, 1),
            wkv_p[None, None], wkv_s[None], shift_p[None], shift_s[None])
```

```python
import functools

import jax
import jax.numpy as jnp
from jax import lax
from jax.experimental import pallas as pl
from jax.experimental.pallas import tpu as pltpu

F32 = jnp.float32
MXU_DT = jnp.bfloat16

D_MODEL = 1024
PAGE_SIZE = 128
N_HEADS = 8
HEAD_DIM = 64
WIDTH = N_HEADS * HEAD_DIM
LORA_W = 64
LORA_A = 64
LORA_G = 128
R_PROJ = 3 * WIDTH + LORA_W + LORA_A + LORA_G
GN_EPS = 64e-5
MOBA_BLOCK = 256
MOBA_TOPK = 3
ROT_DIM = HEAD_DIM // 4
ROPE_THETA = 500000.0
IN_PROJ = R_PROJ + 3 * WIDTH + 2 * D_MODEL
D_FF = 2816
LN_EPS = 1e-5
DN_ALPHA = 2.0 ** 0.25
CHUNK = 64
LANES = 128
VMEM_LIMIT = 56 * 1024 * 1024
SAMPLE_PAGES_PER_STEP = 64
FFN_ROWS = 1024
FFN_CHUNKS = 11
RWKV_GROUP = 4
VT_ROWS = HEAD_DIM + 16

NEG_INF = float("-inf")
LOG2_E = 1.4426950408889634


def _cparams(*sem, vmem=VMEM_LIMIT):
    return pltpu.CompilerParams(dimension_semantics=sem, vmem_limit_bytes=vmem)


def _const_spec(shape):
    return pl.BlockSpec(shape, lambda *_: (0,) * len(shape), pipeline_mode=pl.Buffered(1))


def _row_spec(tm, width):
    return pl.BlockSpec((tm, width), lambda i: (i, 0))


def _mod_spec(arr, tm):
    if arr.shape[0] == 1:
        return pl.BlockSpec((1, arr.shape[1]), lambda i: (0, 0))
    return pl.BlockSpec((tm, arr.shape[1]), lambda i: (i, 0))


def _dot(a, b):
    return jnp.dot(a.astype(MXU_DT), b.astype(MXU_DT), preferred_element_type=F32)


_NN = (((1,), (0,)), ((), ()))
_NT = (((1,), (1,)), ((), ()))
_TN = (((0,), (0,)), ((), ()))


def _dg(a, b, dn):
    return lax.dot_general(a.astype(MXU_DT), b.astype(MXU_DT), dn, preferred_element_type=F32)


def _split(x):
    hi = x.astype(MXU_DT)
    lo = (x - hi.astype(F32)).astype(MXU_DT)
    return hi, lo


def _seg_sum(x, ones):
    hi, lo = _split(x)
    return (jnp.dot(hi, ones, preferred_element_type=F32)
            + jnp.dot(lo, ones, preferred_element_type=F32))


def _layer_norm(y, g, b):
    mu = jnp.mean(y, axis=-1, keepdims=True)
    yc = y - mu
    var = jnp.mean(yc * yc, axis=-1, keepdims=True)
    return yc * lax.rsqrt(var + LN_EPS) * g + b


def _ada_kernel(c_ref, w_ref, b_ref, o_ref):
    c = c_ref[...]
    o_ref[...] = _dot(c * jax.nn.sigmoid(c), w_ref[...]) + b_ref[...]


def _ada(c, w_ada, b_ada):
    n, d = c.shape
    nout = w_ada.shape[1]
    tn = 1024
    return pl.pallas_call(
        _ada_kernel,
        grid=(nout // tn,),
        in_specs=[pl.BlockSpec((n, d), lambda j: (0, 0)),
                  pl.BlockSpec((d, tn), lambda j: (0, j)),
                  pl.BlockSpec((1, tn), lambda j: (0, j))],
        out_specs=pl.BlockSpec((n, tn), lambda j: (0, j)),
        out_shape=jax.ShapeDtypeStruct((n, nout), F32),
        compiler_params=_cparams("parallel"),
    )(c, w_ada, b_ada)


def _ffn_kernel(x_ref, sh_ref, sc_ref, gt_ref, wg_ref, wu_ref, wd_ref, lg_ref, lb_ref, o_ref, *, n_ff):
    x = x_ref[...]
    h = (x * (1.0 + sc_ref[...]) + sh_ref[...]).astype(MXU_DT)
    step = D_FF // n_ff
    f = None
    for j in range(n_ff):
        cs = slice(j * step, (j + 1) * step)
        g = jnp.dot(h, wg_ref[:, cs], preferred_element_type=F32)
        u = jnp.dot(h, wu_ref[:, cs], preferred_element_type=F32)
        a = (g * jax.nn.sigmoid(g) * u).astype(MXU_DT)
        part = jnp.dot(a, wd_ref[cs, :], preferred_element_type=F32)
        f = part if f is None else f + part
    y = DN_ALPHA * x + (0.5 * gt_ref[...]) * f
    o_ref[...] = _layer_norm(y, lg_ref[...], lb_ref[...])


def _ffn(x, shift, scale, gate, wg, wu, wd, ln_g, ln_b, tm):
    t = x.shape[0]
    return pl.pallas_call(
        functools.partial(_ffn_kernel, n_ff=FFN_CHUNKS),
        grid=(t // tm,),
        in_specs=[_row_spec(tm, D_MODEL), _mod_spec(shift, tm), _mod_spec(scale, tm), _mod_spec(gate, tm),
                  _const_spec((D_MODEL, D_FF)), _const_spec((D_MODEL, D_FF)), _const_spec((D_FF, D_MODEL)),
                  _const_spec((1, D_MODEL)), _const_spec((1, D_MODEL))],
        out_specs=_row_spec(tm, D_MODEL),
        out_shape=jax.ShapeDtypeStruct((t, D_MODEL), F32),
        compiler_params=_cparams("parallel"),
    )(x, shift, scale, gate, wg, wu, wd, ln_g, ln_b)


def _rotate(x, cos, s_up, s_dn):
    return x * cos + pltpu.roll(x, LANES - ROT_DIM // 2, 1) * s_up + pltpu.roll(x, ROT_DIM // 2, 1) * s_dn


def _inproj_kernel(x_ref, sh_ref, sc_ref, w_ref, cos_ref, sup_ref, sdn_ref,
                   pr_ref, q_ref, k_ref, v_ref, gr_ref, gm_ref, kb_ref, vt_ref, km_ref, *, q_scale):
    h = (x_ref[...] * (1.0 + sc_ref[...]) + sh_ref[...]).astype(MXU_DT)
    tm = h.shape[0]

    def proj(lo, width):
        return jnp.dot(h, w_ref[:, lo:lo + width], preferred_element_type=F32)

    pr_ref[...] = proj(0, R_PROJ)
    cos, s_up, s_dn = cos_ref[...], sup_ref[...], sdn_ref[...]
    q = proj(R_PROJ, WIDTH)
    k = proj(R_PROJ + WIDTH, WIDTH)
    k_rot = []
    for j in range(WIDTH // LANES):
        ls = slice(j * LANES, (j + 1) * LANES)
        qj = _rotate(q[:, ls], cos, s_up, s_dn)
        kj = _rotate(k[:, ls], cos, s_up, s_dn)
        q_ref[:, ls] = (qj * q_scale).astype(MXU_DT)
        kb_ref[:, ls] = kj.astype(MXU_DT)
        k_rot.append(kj)
        if km_ref is not None:
            km_ref[0, :, ls] = jnp.sum(kj.reshape(tm // MOBA_BLOCK, MOBA_BLOCK, LANES), axis=1) * (1.0 / MOBA_BLOCK)
    k_rot = jnp.concatenate(k_rot, axis=1)
    v = proj(R_PROJ + 2 * WIDTH, WIDTH)
    if vt_ref is None:
        k_ref[...] = k_rot
        v_ref[...] = v
    else:
        v_t = v.T
        k_ref[...] = k_rot.T
        v_ref[...] = v_t
        for head in range(N_HEADS):
            base = head * VT_ROWS
            vt_ref[base:base + HEAD_DIM, :] = v_t[head * HEAD_DIM:(head + 1) * HEAD_DIM, :].astype(MXU_DT)
            vt_ref[base + HEAD_DIM:base + VT_ROWS, :] = jnp.ones((VT_ROWS - HEAD_DIM, tm), MXU_DT)
    gr_ref[...] = proj(R_PROJ + 3 * WIDTH, D_MODEL)
    gm_ref[...] = proj(R_PROJ + 3 * WIDTH + D_MODEL, D_MODEL)


def _inproj_prompt_kernel(*refs):
    _inproj_kernel(*refs, q_scale=HEAD_DIM ** -0.5 * LOG2_E)


def _inproj_sample_kernel(*refs):
    _inproj_kernel(*refs, None, None, q_scale=HEAD_DIM ** -0.5)


def _inproj(x, shift, scale, w_in, cos, s_up, s_dn, tm, prompt):
    t = x.shape[0]
    out_shape = [jax.ShapeDtypeStruct((t, R_PROJ), F32),
                 jax.ShapeDtypeStruct((t, WIDTH), MXU_DT),
                 jax.ShapeDtypeStruct((t, WIDTH), F32),
                 jax.ShapeDtypeStruct((t, WIDTH), F32),
                 jax.ShapeDtypeStruct((t, D_MODEL), F32),
                 jax.ShapeDtypeStruct((t, D_MODEL), F32),
                 jax.ShapeDtypeStruct((t, WIDTH), MXU_DT)]
    out_specs = [_row_spec(tm, R_PROJ), _row_spec(tm, WIDTH), _row_spec(tm, WIDTH), _row_spec(tm, WIDTH),
                 _row_spec(tm, D_MODEL), _row_spec(tm, D_MODEL), _row_spec(tm, WIDTH)]
    if prompt:
        nblk = tm // MOBA_BLOCK
        vt_rows = N_HEADS * VT_ROWS
        out_shape[2] = out_shape[3] = jax.ShapeDtypeStruct((WIDTH, t), F32)
        out_specs[2] = out_specs[3] = pl.BlockSpec((WIDTH, tm), lambda i: (0, i))
        out_shape += [jax.ShapeDtypeStruct((vt_rows, t), MXU_DT),
                      jax.ShapeDtypeStruct((t // tm, nblk, WIDTH), F32)]
        out_specs += [pl.BlockSpec((vt_rows, tm), lambda i: (0, i)),
                      pl.BlockSpec((1, nblk, WIDTH), lambda i: (i, 0, 0))]
    return pl.pallas_call(
        _inproj_prompt_kernel if prompt else _inproj_sample_kernel,
        grid=(t // tm,),
        in_specs=[_row_spec(tm, D_MODEL), _mod_spec(shift, tm), _mod_spec(scale, tm),
                  _const_spec((D_MODEL, IN_PROJ)),
                  _row_spec(tm, LANES), _row_spec(tm, LANES), _row_spec(tm, LANES)],
        out_specs=out_specs,
        out_shape=out_shape,
        compiler_params=_cparams("parallel"),
    )(x, shift, scale, w_in, cos, s_up, s_dn)


def _rope_tables(pos):
    half = ROT_DIM // 2
    inv = ROPE_THETA ** (-jnp.arange(half, dtype=F32) * 2.0 / ROT_DIM)
    ang = pos.astype(F32)[:, None] * inv[None, :]
    cos, sin = jnp.cos(ang), jnp.sin(ang)
    t = pos.shape[0]
    zeros = lambda n: jnp.zeros((t, n), F32)
    per_head = lambda parts: jnp.tile(jnp.concatenate(parts, axis=1), (1, LANES // HEAD_DIM))
    c = per_head([cos, cos, jnp.ones((t, HEAD_DIM - ROT_DIM), F32)])
    s_up = per_head([-sin, zeros(HEAD_DIM - half)])
    s_dn = per_head([zeros(half), sin, zeros(HEAD_DIM - ROT_DIM)])
    return c, s_up, s_dn


def _rwkv_pointwise(pr, prev, mu, wcat, w0, a0, g2, k_k, k_a, ones):
    xs = pr + (prev - pr) * mu
    r = xs[:, 0:WIDTH]
    k = xs[:, WIDTH:2 * WIDTH]
    v = xs[:, 2 * WIDTH:3 * WIDTH]
    la = xs[:, 3 * WIDTH:3 * WIDTH + LORA_W + LORA_A]
    gl = xs[:, 3 * WIDTH + LORA_W + LORA_A:R_PROJ]
    lane = lax.broadcasted_iota(jnp.int32, la.shape, 1)
    la = jnp.where(lane < LORA_W, jnp.tanh(la), la)
    wa = _dot(la, wcat)
    w = w0 + wa[:, :WIDTH]
    a = jax.nn.sigmoid(a0 + wa[:, WIDTH:])
    g = _dot(jax.nn.sigmoid(gl), g2)
    z = -w
    softplus = jnp.maximum(z, 0.0) + jnp.log1p(jnp.exp(-jnp.abs(z)))
    log_decay = -jnp.exp(-softplus - 0.5)
    kk = k * k_k
    kk = kk / jnp.maximum(jnp.sqrt(_seg_sum(kk * kk, ones)), 1e-12)
    k_eff = k * (1.0 + (a - 1.0) * k_a)
    return r, k_eff, v, kk, kk * a, log_decay, g


def _rwkv_post(y, r, k_eff, v, g, r_k, lnx_g, lnx_b, ones):
    inv = 1.0 / HEAD_DIM
    yc = y - _seg_sum(y, ones) * inv
    yv = _seg_sum(yc * yc, ones) * inv
    yn = yc * lax.rsqrt(yv + GN_EPS) * lnx_g + lnx_b
    bonus = _seg_sum(r * k_eff * r_k, ones) * v
    return (yn + bonus) * g


def _rwkv_prompt_kernel(pr_ref, shift0_ref, s0_ref, mu_ref, wcat_ref, w0_ref, a0_ref, g2_ref, kk_ref, ka_ref,
                        rk_ref, lg_ref, lb_ref, ones_ref, tri_ref, blk_ref,
                        out_ref, sfin_ref, shift_ref,
                        prev_scr, st_scr, rt_scr, kp_scr, bt_scr, kt_scr, v_scr, bc_scr, kc_scr, pc_scr, y_scr):
    i = pl.program_id(0)

    @pl.when(i == 0)
    def _():
        prev_scr[...] = shift0_ref[...]
        st_scr[...] = s0_ref[...]

    pr = pr_ref[...]
    tm = pr.shape[0]
    row = lax.broadcasted_iota(jnp.int32, (tm, 1), 0)
    prev = jnp.where(row == 0, prev_scr[...], pltpu.roll(pr, 1, 0))
    ones = ones_ref[...]
    r, k_eff, v, kk, b, lw, g = _rwkv_pointwise(pr, prev, mu_ref[...], wcat_ref[...], w0_ref[...], a0_ref[...],
                                                g2_ref[...], kk_ref[...], ka_ref[...], ones)
    l_hi = lw.astype(MXU_DT)
    rem = lw - l_hi.astype(F32)
    l_mid = rem.astype(MXU_DT)
    l_lo = (rem - l_mid.astype(F32)).astype(MXU_DT)
    cum = lambda m: (jnp.dot(m, l_hi, preferred_element_type=F32)
                     + (jnp.dot(m, l_mid, preferred_element_type=F32) + jnp.dot(m, l_lo, preferred_element_type=F32)))
    L = cum(tri_ref[...])
    LC = cum(blk_ref[...])
    e_inv = jnp.exp(-L)
    e_end = jnp.exp(LC - L)
    rt_scr[...] = r * jnp.exp(L)
    kp_scr[...] = kk * jnp.exp(L - lw)
    bt_scr[...] = b * e_inv
    kt_scr[...] = k_eff * e_inv
    bc_scr[...] = b * e_end
    kc_scr[...] = k_eff * e_end
    pc_scr[...] = jnp.exp(LC)
    v_scr[...] = v

    ri = lax.broadcasted_iota(jnp.int32, (CHUNK, CHUNK), 0)
    ci = lax.broadcasted_iota(jnp.int32, (CHUNK, CHUNK), 1)
    strict, incl, eye = ri > ci, ri >= ci, ri == ci

    n_chunks = tm // CHUNK
    each = lambda fn, *lists: [fn(*args) for args in zip(*lists)]

    def solve(chunks):
        probs = [(c, h) for c in chunks for h in range(N_HEADS)]

        def blocks(ref):
            return [ref[c * CHUNK:(c + 1) * CHUNK, h * HEAD_DIM:(h + 1) * HEAD_DIM] for c, h in probs]

        def blocks_t(ref):
            per_chunk = {c: ref[c * CHUNK:(c + 1) * CHUNK, :].T for c in chunks}
            return [per_chunk[c][h * HEAD_DIM:(h + 1) * HEAD_DIM, :] for c, h in probs]

        kp, rt, v_b = blocks(kp_scr), blocks(rt_scr), blocks(v_scr)
        bk = each(lambda a, b_: jnp.concatenate([a, b_], axis=0), blocks(bt_scr), blocks(kt_scr))
        bk = each(lambda b_: b_.astype(MXU_DT), bk)
        v_m = each(lambda a: a.astype(MXU_DT), v_b)
        akk = each(lambda a, b_: _dg(a, b_, _NT), kp, bk)
        arr = each(lambda a, b_: _dg(a, b_, _NT), rt, bk)
        a_zb = [jnp.where(strict, a[:, :CHUNK], 0.0) for a in akk]
        a_vk = [jnp.where(strict, a[:, CHUNK:], 0.0) for a in akk]
        a_rb = [jnp.where(incl, a[:, :CHUNK], 0.0) for a in arr]
        a_rk = [jnp.where(incl, a[:, CHUNK:], 0.0) for a in arr]
        u0 = each(lambda a, w, v_: jnp.concatenate([a, _dg(w, v_, _NN)], axis=1), kp, a_vk, v_m)
        half = lambda s: (ri // (2 * s) == ci // (2 * s)) & (ri % (2 * s) >= s) & (ci % (2 * s) < s)
        t_inv = [jnp.where(eye, 1.0, 0.0) - jnp.where(half(1), a, 0.0) for a in a_zb]
        s_blk = 2
        while s_blk < CHUNK:
            lower = half(s_blk)
            t_inv = each(lambda t_, a: t_ - _dg(_dg(t_, jnp.where(lower, a, 0.0), _NN), t_, _NN), t_inv, a_zb)
            s_blk *= 2
        u = each(lambda t_, u_: _dg(t_, u_, _NN), t_inv, u0)
        u_m = each(lambda a: a.astype(MXU_DT), u)
        gmat = each(lambda a, u_: _dg(a, u_, _NN), a_rb, u_m)
        r_eff = each(lambda a, g_: a - g_[:, :HEAD_DIM], rt, gmat)
        y0 = each(lambda a, b_, g_: _dg(a, b_, _NN) - g_[:, HEAD_DIM:], a_rk, v_m, gmat)
        bu = each(lambda a, u_: _dg(a, u_, _NN), blocks_t(bc_scr), u_m)
        pc = [pc_scr[c * CHUNK:c * CHUNK + 1, h * HEAD_DIM:(h + 1) * HEAD_DIM] for c, h in probs]
        m_t = each(lambda p_, b_: jnp.where(eye, p_, 0.0) - b_[:, :HEAD_DIM], pc, bu)
        n_t = each(lambda a, v_, c_: _dg(a, v_, _NN) - c_[:, HEAD_DIM:], blocks_t(kc_scr), v_m, bu)
        return r_eff, y0, m_t, n_t

    parts = [solve(range(c0, c0 + RWKV_GROUP)) for c0 in range(0, n_chunks, RWKV_GROUP)]
    r_eff, y0, m_t, n_t = ([x for part in parts for x in part[i]] for i in range(4))
    st = [st_scr[h] for h in range(N_HEADS)]
    for c in range(n_chunks):
        sl = slice(c * N_HEADS, (c + 1) * N_HEADS)
        st_m = each(lambda a: a.astype(MXU_DT), st)
        y_scr[c * CHUNK:(c + 1) * CHUNK, :] = jnp.concatenate(
            each(lambda a, s_, b_: _dg(a, s_, _NN) + b_, r_eff[sl], st_m, y0[sl]), axis=1)
        st = each(lambda m_, s_, n_: _dg(m_, s_, _NN) + n_, m_t[sl], st_m, n_t[sl])
    for h in range(N_HEADS):
        st_scr[h] = st[h]
    out_ref[...] = _rwkv_post(y_scr[...], r, k_eff, v, g, rk_ref[...], lg_ref[...], lb_ref[...], ones)
    prev_scr[...] = pr[tm - 1:tm, :]
    shift_ref[...] = pr[tm - 1:tm, :]

    @pl.when(i == pl.num_programs(0) - 1)
    def _():
        for h in range(N_HEADS):
            sfin_ref[h] = st[h].T


def _rwkv_prompt(pr, shift0, s0, rp, tm):
    t = pr.shape[0]
    ch = lax.broadcasted_iota(jnp.int32, (tm, tm), 0) // CHUNK == lax.broadcasted_iota(jnp.int32, (tm, tm), 1) // CHUNK
    low = lax.broadcasted_iota(jnp.int32, (tm, tm), 0) >= lax.broadcasted_iota(jnp.int32, (tm, tm), 1)
    tri = (ch & low).astype(MXU_DT)
    blk = ch.astype(MXU_DT)
    vec = _const_spec((1, WIDTH))
    tile = pltpu.VMEM((tm, WIDTH), F32)
    return pl.pallas_call(
        _rwkv_prompt_kernel,
        grid=(t // tm,),
        in_specs=[_row_spec(tm, R_PROJ), _const_spec((1, R_PROJ)), _const_spec((N_HEADS, HEAD_DIM, HEAD_DIM)),
                  _const_spec((1, R_PROJ)), _const_spec((LORA_W + LORA_A, 2 * WIDTH)), vec, vec,
                  _const_spec((LORA_G, WIDTH)), vec, vec, vec, vec, vec,
                  _const_spec((WIDTH, WIDTH)), _const_spec((tm, tm)), _const_spec((tm, tm))],
        out_specs=[_row_spec(tm, WIDTH),
                   pl.BlockSpec((N_HEADS, HEAD_DIM, HEAD_DIM), lambda i: (0, 0, 0)),
                   pl.BlockSpec((1, R_PROJ), lambda i: (0, 0))],
        out_shape=[jax.ShapeDtypeStruct((t, WIDTH), F32),
                   jax.ShapeDtypeStruct((N_HEADS, HEAD_DIM, HEAD_DIM), F32),
                   jax.ShapeDtypeStruct((1, R_PROJ), F32)],
        scratch_shapes=[pltpu.VMEM((1, R_PROJ), F32), pltpu.VMEM((N_HEADS, HEAD_DIM, HEAD_DIM), F32)] + [tile] * 9,
        compiler_params=_cparams("arbitrary"),
    )(pr, shift0, jnp.swapaxes(s0, 1, 2), rp["mu"], rp["wcat"], rp["w0"], rp["a0"], rp["g2"], rp["k_k"], rp["k_a"],
      rp["r_k"], rp["lnx_g"], rp["lnx_b"], rp["ones"], tri, blk)


def _rwkv_step_kernel(pr_ref, shift_ref, s_ref, mu_ref, wcat_ref, w0_ref, a0_ref, g2_ref, kk_ref, ka_ref,
                      rk_ref, lg_ref, lb_ref, ones_ref, out_ref, snew_ref):
    ones = ones_ref[...]
    r, k_eff, v, kk, b, lw, g = _rwkv_pointwise(pr_ref[...], shift_ref[...], mu_ref[...], wcat_ref[...],
                                                w0_ref[...], a0_ref[...], g2_ref[...], kk_ref[...], ka_ref[...], ones)
    decay = jnp.exp(lw)
    eye = (lax.broadcasted_iota(jnp.int32, (HEAD_DIM, HEAD_DIM), 0)
           == lax.broadcasted_iota(jnp.int32, (HEAD_DIM, HEAD_DIM), 1))[None]
    ys = []
    for h in range(N_HEADS):
        over_v = lambda z: z[:, None, h * HEAD_DIM:(h + 1) * HEAD_DIM]
        s = s_ref[:, h]
        s_kk = jnp.sum(s * over_v(kk), axis=2, keepdims=True)
        v_col = jnp.sum(jnp.where(eye, over_v(v), 0.0), axis=2, keepdims=True)
        s_new = s * over_v(decay) - s_kk * over_v(b) + v_col * over_v(k_eff)
        y_col = jnp.sum(s_new * over_v(r), axis=2, keepdims=True)
        ys.append(jnp.sum(jnp.where(eye, y_col, 0.0), axis=1))
        snew_ref[:, h] = s_new
    y = jnp.concatenate(ys, axis=1)
    out_ref[...] = _rwkv_post(y, r, k_eff, v, g, rk_ref[...], lg_ref[...], lb_ref[...], ones)


def _rwkv_step(pr, shift, state, rp, nb):
    n = pr.shape[0]
    vec = _const_spec((1, WIDTH))
    state_spec = pl.BlockSpec((nb, N_HEADS, HEAD_DIM, HEAD_DIM), lambda i: (i, 0, 0, 0))
    return pl.pallas_call(
        _rwkv_step_kernel,
        grid=(n // nb,),
        in_specs=[_row_spec(nb, R_PROJ), _row_spec(nb, R_PROJ), state_spec,
                  _const_spec((1, R_PROJ)), _const_spec((LORA_W + LORA_A, 2 * WIDTH)), vec, vec,
                  _const_spec((LORA_G, WIDTH)), vec, vec, vec, vec, vec, _const_spec((WIDTH, WIDTH))],
        out_specs=[_row_spec(nb, WIDTH), state_spec],
        out_shape=[jax.ShapeDtypeStruct((n, WIDTH), F32), jax.ShapeDtypeStruct(state.shape, F32)],
        compiler_params=_cparams("parallel"),
    )(pr, shift, state, rp["mu"], rp["wcat"], rp["w0"], rp["a0"], rp["g2"], rp["k_k"], rp["k_a"], rp["r_k"],
      rp["lnx_g"], rp["lnx_b"], rp["ones"])


def _moba_prompt_kernel(q_ref, km_ref, k_ref, vt_ref, o_ref, qm_scr, sel_scr, m_scr, acc_scr,
                        s0_scr, s1_scr, s2_scr, s3_scr, cm_scr):
    c = pl.program_id(0)
    tq = q_ref.shape[0]
    nblk = km_ref.shape[0]
    lane = lax.broadcasted_iota(jnp.int32, (1, LANES), 1)
    blk_id = lax.broadcasted_iota(jnp.int32, (nblk, tq), 0)
    key_i = lax.broadcasted_iota(jnp.int32, (MOBA_BLOCK, tq), 0)
    qry_i = lax.broadcasted_iota(jnp.int32, (MOBA_BLOCK, tq), 1)

    def head_lanes(h):
        return (lane < HEAD_DIM) if h % 2 == 0 else (lane >= HEAD_DIM)

    def values(h, b):
        rows = slice(h * VT_ROWS, (h + 1) * VT_ROWS)
        return vt_ref[rows, pl.ds(pl.multiple_of(b * MOBA_BLOCK, MOBA_BLOCK), MOBA_BLOCK)]

    def keys(h, b):
        ps = slice((h // 2) * LANES, (h // 2 + 1) * LANES)
        return k_ref[pl.ds(pl.multiple_of(b * MOBA_BLOCK, MOBA_BLOCK), MOBA_BLOCK), ps]

    all_heads = range(N_HEADS)
    pair_lanes = lambda h: slice((h // 2) * LANES, (h // 2 + 1) * LANES)
    for h in all_heads:
        qm_scr[h] = jnp.where(head_lanes(h), q_ref[:, pair_lanes(h)].astype(F32), 0.0).astype(MXU_DT)
    gates = [jnp.where(blk_id < c, lax.dot_general(km_ref[:, pair_lanes(h)], qm_scr[h], _NT,
                                                   preferred_element_type=F32), NEG_INF)
             for h in all_heads]
    sels = [jnp.zeros(gates[0].shape, jnp.bool_) for _ in all_heads]
    for _ in range(MOBA_TOPK):
        tops = [jnp.max(g_, axis=0, keepdims=True) for g_ in gates]
        hits = [blk_id == jnp.min(jnp.where(g_ == t_, blk_id, nblk), axis=0, keepdims=True)
                for g_, t_ in zip(gates, tops)]
        sels = [s_ | h_ for s_, h_ in zip(sels, hits)]
        gates = [jnp.where(h_, NEG_INF, g_) for g_, h_ in zip(gates, hits)]
    for h in all_heads:
        sel_scr[h] = sels[h].astype(F32)
    for h in all_heads:
        s = lax.dot_general(keys(h, c), qm_scr[h], _NT, preferred_element_type=F32)
        s = jnp.where(key_i <= qry_i, s, NEG_INF)
        s0_scr[h] = s
        m_scr[h] = jnp.max(s, axis=0, keepdims=True)
    for h in all_heads:
        p = jnp.exp2(s0_scr[h] - m_scr[h]).astype(MXU_DT)
        acc_scr[h] = jnp.dot(values(h, c), p, preferred_element_type=F32)

    def scores(b, s_scr, slot):
        for h in range(N_HEADS):
            s = lax.dot_general(keys(h, b), qm_scr[h], _NT, preferred_element_type=F32)
            s_scr[h] = s
            cm_scr[slot, h] = jnp.max(s, axis=0, keepdims=True)

    def absorb(b, s_scr, slot):
        for h in range(N_HEADS):
            picked = sel_scr[h, pl.ds(b, 1), :] > 0.0
            m_old = m_scr[h]
            m_new = jnp.where(picked, jnp.maximum(m_old, cm_scr[slot, h]), m_old)
            p = jnp.exp2(s_scr[h] - jnp.where(picked, m_new, jnp.inf)).astype(MXU_DT)
            acc_scr[h] = acc_scr[h] * jnp.exp2(m_old - m_new) + jnp.dot(values(h, b), p, preferred_element_type=F32)
            m_scr[h] = m_new

    bufs = (s0_scr, s1_scr, s2_scr, s3_scr)

    def block_group(i, carry):
        for j, buf in enumerate(bufs):
            scores(len(bufs) * i + j, buf, j)
        for j, buf in enumerate(bufs):
            absorb(len(bufs) * i + j, buf, j)
        return carry

    def single_block(b, carry):
        scores(b, s0_scr, 0)
        absorb(b, s0_scr, 0)
        return carry

    n_groups = c // len(bufs)
    lax.fori_loop(0, n_groups, block_group, 0)
    lax.fori_loop(n_groups * len(bufs), c, single_block, 0)

    for pair in range(N_HEADS // 2):
        outs = []
        for h in (2 * pair, 2 * pair + 1):
            a = acc_scr[h]
            outs.append((a[:HEAD_DIM] / a[HEAD_DIM:HEAD_DIM + 1]).T)
        o_ref[:, pair * LANES:(pair + 1) * LANES] = jnp.concatenate(outs, axis=1)


def _moba_prompt(q, kmean, kb, vt):
    t = q.shape[0]
    nblk = kmean.shape[0]
    tq = MOBA_BLOCK
    return pl.pallas_call(
        _moba_prompt_kernel,
        grid=(t // tq,),
        in_specs=[_row_spec(tq, WIDTH), _const_spec((nblk, WIDTH)), _const_spec((t, WIDTH)),
                  _const_spec(vt.shape)],
        out_specs=_row_spec(tq, WIDTH),
        out_shape=jax.ShapeDtypeStruct((t, WIDTH), F32),
        scratch_shapes=[pltpu.VMEM((N_HEADS, tq, LANES), MXU_DT), pltpu.VMEM((N_HEADS, nblk, tq), F32),
                        pltpu.VMEM((N_HEADS, 1, tq), F32), pltpu.VMEM((N_HEADS, VT_ROWS, tq), F32),
                        ] + [pltpu.VMEM((N_HEADS, MOBA_BLOCK, tq), F32)] * 4 + [
                        pltpu.VMEM((4, N_HEADS, 1, tq), F32)],
        compiler_params=_cparams("parallel"),
    )(q, kmean, kb, vt)


def _moba_sample_score_kernel(pt_ref, q_ref, knew_ref, *refs):
    del pt_ref
    pg = SAMPLE_PAGES_PER_STEP
    pages = refs[:pg]
    p_ref, pself_ref, idx_ref, sc_scr, gate_scr = refs[pg:]
    j = pl.program_id(1)
    nblk = gate_scr.shape[1]
    rnd = lambda z: z.astype(MXU_DT).astype(F32)
    q = q_ref[0]
    for t in range(pg // 2):
        halves = (pages[2 * t][0], pages[2 * t + 1][0])
        sc = jnp.concatenate([jnp.sum(page * q, axis=1, keepdims=True) for page in halves], axis=2)
        gate = jnp.broadcast_to(jnp.sum(sc, axis=2, keepdims=True), (N_HEADS, 1, LANES))
        blk = pl.ds(j * (pg // 2) + t, 1)
        for h in range(N_HEADS):
            sc_scr[h, blk, :] = sc[h]
            gate_scr[h, blk, :] = gate[h]

    @pl.when(j == pl.num_programs(1) - 1)
    def _():
        gate = gate_scr[...]
        blk_id = lax.broadcasted_iota(jnp.int32, gate.shape, 1)
        sel = jnp.zeros(gate.shape, jnp.bool_)
        for r in range(MOBA_TOPK):
            top = jnp.max(gate, axis=1, keepdims=True)
            idx = jnp.min(jnp.where(gate == top, blk_id, nblk), axis=1, keepdims=True)
            hit = blk_id == idx
            sel = sel | hit
            gate = jnp.where(hit, NEG_INF, gate)
            idx_ref[0, r] = idx
        sel_f = jnp.where(sel, 1.0, 0.0)
        picked = jnp.concatenate([sel_f] * (MOBA_BLOCK // LANES), axis=2) > 0.0
        s = jnp.where(picked, sc_scr[...], NEG_INF)
        s_self = jnp.sum(q * rnd(knew_ref[0]), axis=1, keepdims=True)
        m = jnp.maximum(jnp.max(jnp.max(s, axis=2, keepdims=True), axis=1, keepdims=True), s_self)
        p = jnp.exp(s - m[:, :, :1])
        p_self = jnp.exp(s_self - m)
        inv = 1.0 / (jnp.sum(jnp.sum(p, axis=2, keepdims=True), axis=1, keepdims=True) + p_self)
        p_ref[0] = p * inv[:, :, :1]
        pself_ref[0] = p_self * inv


def _moba_sample_scores(q_lanes, knew_lanes, cache_kt, page_table):
    n, n_pages = page_table.shape
    pg = SAMPLE_PAGES_PER_STEP
    nblk = n_pages * PAGE_SIZE // MOBA_BLOCK
    page_block = (1, N_HEADS, HEAD_DIM, PAGE_SIZE)

    def page_spec(e):
        return pl.BlockSpec(page_block, lambda i, j, pt: (pt[i * n_pages + j * pg + e], 0, 0, 0))

    per_seq = pl.BlockSpec(page_block, lambda i, j, pt: (i, 0, 0, 0))
    grid_spec = pltpu.PrefetchScalarGridSpec(
        num_scalar_prefetch=1,
        grid=(n, n_pages // pg),
        in_specs=[per_seq, per_seq] + [page_spec(e) for e in range(pg)],
        out_specs=[pl.BlockSpec((1, N_HEADS, nblk, MOBA_BLOCK), lambda i, j, pt: (i, 0, 0, 0)),
                   pl.BlockSpec((1, N_HEADS, 1, LANES), lambda i, j, pt: (i, 0, 0, 0)),
                   pl.BlockSpec((1, MOBA_TOPK, N_HEADS, 1, LANES), lambda i, j, pt: (i, 0, 0, 0, 0))],
        scratch_shapes=[pltpu.VMEM((N_HEADS, nblk, MOBA_BLOCK), F32), pltpu.VMEM((N_HEADS, nblk, LANES), F32)],
    )
    return pl.pallas_call(
        _moba_sample_score_kernel,
        grid_spec=grid_spec,
        out_shape=[jax.ShapeDtypeStruct((n, N_HEADS, nblk, MOBA_BLOCK), F32),
                   jax.ShapeDtypeStruct((n, N_HEADS, 1, LANES), F32),
                   jax.ShapeDtypeStruct((n, MOBA_TOPK, N_HEADS, 1, LANES), jnp.int32)],
        compiler_params=_cparams("parallel", "arbitrary"),
    )(page_table.reshape(-1), q_lanes, knew_lanes, *([cache_kt] * pg))


def _moba_sample_value_kernel(phys_ref, blk_ref, p_ref, pself_ref, vnew_ref, *refs):
    del phys_ref
    pages_per_block = MOBA_BLOCK // PAGE_SIZE
    v_refs, o_ref = refs[:-1], refs[-1]
    i = pl.program_id(0)
    rnd = lambda z: z.astype(MXU_DT).astype(F32)
    for h in range(N_HEADS):
        acc = rnd(pself_ref[0, h][:, :HEAD_DIM]) * rnd(vnew_ref[0, h])
        for r in range(MOBA_TOPK):
            slot = h * MOBA_TOPK + r
            p_row = p_ref[0, h, pl.ds(blk_ref[i * N_HEADS * MOBA_TOPK + slot], 1), :]
            for e in range(pages_per_block):
                p_rows = jnp.broadcast_to(p_row[:, e * PAGE_SIZE:(e + 1) * PAGE_SIZE], (N_HEADS, PAGE_SIZE))
                v_t = v_refs[slot * pages_per_block + e][0, 0]
                acc = acc + _dg(p_rows, v_t, _NT)[0:1]
        o_ref[0, h] = acc


def _moba_sample_values(p, p_self, v_new, cache_vt, phys, blocks):
    n, _, nblk, _ = p.shape
    pages_per_block = MOBA_BLOCK // PAGE_SIZE
    per_seq = N_HEADS * MOBA_TOPK * pages_per_block

    def page_spec(h, e):
        return pl.BlockSpec((1, 1, HEAD_DIM, PAGE_SIZE),
                            lambda i, ph, bk: (ph[i * per_seq + h * MOBA_TOPK * pages_per_block + e], h, 0, 0))

    whole = lambda shape: pl.BlockSpec((1,) + shape, lambda i, ph, bk: (i,) + (0,) * len(shape))
    grid_spec = pltpu.PrefetchScalarGridSpec(
        num_scalar_prefetch=2,
        grid=(n,),
        in_specs=[whole((N_HEADS, nblk, MOBA_BLOCK)), whole((N_HEADS, 1, LANES)), whole((N_HEADS, 1, HEAD_DIM))]
        + [page_spec(h, e) for h in range(N_HEADS) for e in range(MOBA_TOPK * pages_per_block)],
        out_specs=whole((N_HEADS, 1, HEAD_DIM)),
    )
    return pl.pallas_call(
        _moba_sample_value_kernel,
        grid_spec=grid_spec,
        out_shape=jax.ShapeDtypeStruct((n, N_HEADS, 1, HEAD_DIM), F32),
        compiler_params=_cparams("parallel"),
    )(phys, blocks, p, p_self, v_new.reshape(n, N_HEADS, 1, HEAD_DIM), *([cache_vt] * per_seq))


def _moba_sample(q, k_new, v_new, cache_kt, cache_vt, page_table):
    n = q.shape[0]
    lanes = lambda z: jnp.broadcast_to(z.astype(F32).reshape(n, N_HEADS, HEAD_DIM, 1),
                                       (n, N_HEADS, HEAD_DIM, PAGE_SIZE))
    p, p_self, idx = _moba_sample_scores(lanes(q), lanes(k_new), cache_kt, page_table)
    pages_per_block = MOBA_BLOCK // PAGE_SIZE
    blocks = jnp.transpose(idx[..., 0, 0], (0, 2, 1))
    logi = (blocks[..., None] * pages_per_block + jnp.arange(pages_per_block, dtype=jnp.int32)).reshape(
        n, N_HEADS, -1)
    phys = jnp.take_along_axis(page_table[:, None, :], logi, axis=2)
    out = _moba_sample_values(p, p_self, v_new, cache_vt, phys.reshape(-1), blocks.reshape(-1))
    return out.reshape(n, WIDTH)


def _merge_kernel(x_ref, r_ref, a_ref, gr_ref, gm_ref, gt_ref, wr_ref, wm_ref, wo_ref, lg_ref, lb_ref, o_ref):
    br = jax.nn.sigmoid(gr_ref[...]) * _dot(r_ref[...], wr_ref[...])
    bm = jax.nn.sigmoid(gm_ref[...]) * _dot(a_ref[...], wm_ref[...])
    merged = _dot(br + bm, wo_ref[...])
    o_ref[...] = _layer_norm(DN_ALPHA * x_ref[...] + gt_ref[...] * merged, lg_ref[...], lb_ref[...])


def _merge(x, r_out, att, g_r, g_m, gate, wr, wm, wo, ln_g, ln_b, tm):
    t = x.shape[0]
    return pl.pallas_call(
        _merge_kernel,
        grid=(t // tm,),
        in_specs=[_row_spec(tm, D_MODEL), _row_spec(tm, WIDTH), _row_spec(tm, WIDTH), _row_spec(tm, D_MODEL),
                  _row_spec(tm, D_MODEL), _mod_spec(gate, tm),
                  _const_spec((WIDTH, D_MODEL)), _const_spec((WIDTH, D_MODEL)), _const_spec((D_MODEL, D_MODEL)),
                  _const_spec((1, D_MODEL)), _const_spec((1, D_MODEL))],
        out_specs=_row_spec(tm, D_MODEL),
        out_shape=jax.ShapeDtypeStruct((t, D_MODEL), F32),
        compiler_params=_cparams("parallel"),
    )(x, r_out, att, g_r, g_m, gate, wr, wm, wo, ln_g, ln_b)


def _tile(t, want):
    return want if t % want == 0 else t


def _layer(x, mod, pos, wts, rp, rwkv_fn, moba_fn, prompt):
    t = x.shape[0]
    tm = _tile(t, 512)
    row = lambda z: z.reshape(1, -1)
    x1 = _ffn(x, mod[0], mod[1], mod[2], wts["ffn1_gate"], wts["ffn1_up"], wts["ffn1_down"],
              row(wts["ln1_g"]), row(wts["ln1_b"]), _tile(t, FFN_ROWS))
    cos, s_up, s_dn = _rope_tables(pos)
    proj = _inproj(x1, mod[3], mod[4], wts["w_in"], cos, s_up, s_dn, tm, prompt)
    pr, q, k_new, v_new, g_r, g_m, kb = proj[:7]
    r_out, wkv_new, shift_new = rwkv_fn(pr)
    att = moba_fn(q, k_new, v_new, kb, proj[7:])
    x2 = _merge(x1, r_out, att, g_r, g_m, mod[5], wts["w_br_rwkv"], wts["w_br_moba"], wts["w_out"],
                row(wts["ln2_g"]), row(wts["ln2_b"]), _tile(t, FFN_ROWS))
    y = _ffn(x2, mod[6], mod[7], mod[8], wts["ffn2_gate"], wts["ffn2_up"], wts["ffn2_down"],
             row(wts["ln3_g"]), row(wts["ln3_b"]), _tile(t, FFN_ROWS))
    return y, k_new, v_new, wkv_new, shift_new


def kernel(x_prompt, x_sample, cache_k, cache_v, state_rwkv_wkv, state_rwkv_shift, page_table, c_prompt, c_sample, w_ada, b_ada, ffn1_gate, ffn1_up, ffn1_down, ln1_g, ln1_b, w_in, mu_shift, rwkv_w0, rwkv_w2, rwkv_a0, rwkv_a2, rwkv_g2, rwkv_k_k, rwkv_k_a, rwkv_r_k, rwkv_lnx_g, rwkv_lnx_b, w_br_rwkv, w_br_moba, w_out, ln2_g, ln2_b, ffn2_gate, ffn2_up, ffn2_down, ln3_g, ln3_b):
    assert x_prompt.shape[0] == 1 and x_sample.shape[1] == 1 and w_ada.shape[0] == 1
    tp = x_prompt.shape[1]
    ns = x_sample.shape[0]
    n_pages = page_table.shape[1]
    past_len = n_pages * PAGE_SIZE
    assert tp % MOBA_BLOCK == 0 and past_len % MOBA_BLOCK == 0 and n_pages % SAMPLE_PAGES_PER_STEP == 0

    bf = lambda z: z[0].astype(MXU_DT)
    wts = {"ffn1_gate": bf(ffn1_gate), "ffn1_up": bf(ffn1_up), "ffn1_down": bf(ffn1_down),
           "ffn2_gate": bf(ffn2_gate), "ffn2_up": bf(ffn2_up), "ffn2_down": bf(ffn2_down),
           "w_in": bf(w_in), "w_br_rwkv": bf(w_br_rwkv), "w_br_moba": bf(w_br_moba), "w_out": bf(w_out),
           "ln1_g": ln1_g[0], "ln1_b": ln1_b[0], "ln2_g": ln2_g[0], "ln2_b": ln2_b[0],
           "ln3_g": ln3_g[0], "ln3_b": ln3_b[0]}
    zw = jnp.zeros((LORA_W, WIDTH), MXU_DT)
    head_of = jnp.arange(WIDTH) // HEAD_DIM
    row = lambda z: z.reshape(1, -1)
    rp = {"mu": row(mu_shift[0]),
          "wcat": jnp.concatenate([jnp.concatenate([bf(rwkv_w2), zw], axis=1),
                                   jnp.concatenate([zw, bf(rwkv_a2)], axis=1)], axis=0),
          "w0": row(rwkv_w0[0]), "a0": row(rwkv_a0[0]), "g2": bf(rwkv_g2),
          "k_k": row(rwkv_k_k[0]), "k_a": row(rwkv_k_a[0]), "r_k": row(rwkv_r_k[0]),
          "lnx_g": row(rwkv_lnx_g[0]), "lnx_b": row(rwkv_lnx_b[0]),
          "ones": (head_of[:, None] == head_of[None, :]).astype(MXU_DT)}

    n_mod = ((1 + ns + 7) // 8) * 8
    c_all = jnp.concatenate([c_prompt, c_sample, jnp.zeros((n_mod - 1 - ns, D_MODEL), F32)], axis=0)
    ada = _ada(c_all, w_ada[0], b_ada)
    mod_p = [ada[0:1, j * D_MODEL:(j + 1) * D_MODEL] for j in range(9)]
    mod_s = [ada[1:1 + ns, j * D_MODEL:(j + 1) * D_MODEL] for j in range(9)]

    def rwkv_p(pr):
        r_out, s_fin, shift = _rwkv_prompt(pr, jnp.zeros((1, R_PROJ), F32),
                                           jnp.zeros((N_HEADS, HEAD_DIM, HEAD_DIM), F32), rp, _tile(tp, 256))
        return r_out, s_fin, shift

    def moba_p(q, k_new, v_new, kb, extra):
        vt, kmean = extra
        return _moba_prompt(q, kmean.reshape(-1, WIDTH).astype(MXU_DT), kb, vt)

    y_p, k_p, v_p, wkv_p, shift_p = _layer(x_prompt[0], mod_p, jnp.arange(tp, dtype=jnp.int32), wts, rp,
                                           rwkv_p, moba_p, True)

    ck, cv = jnp.transpose(cache_k[0], (0, 2, 3, 1)), jnp.transpose(cache_v[0], (0, 2, 3, 1))

    def rwkv_s(pr):
        r_out, s_new = _rwkv_step(pr, state_rwkv_shift[0], state_rwkv_wkv[0], rp, _tile(ns, 8))
        return r_out, s_new, pr

    def moba_s(q, k_new, v_new, kb, extra):
        return _moba_sample(q, k_new, v_new, ck, cv, page_table)

    y_s, k_s, v_s, wkv_s, shift_s = _layer(x_sample[:, 0], mod_s, jnp.full((ns,), past_len, jnp.int32), wts, rp,
                                           rwkv_s, moba_s, False)

    heads = lambda z, n, t: z.reshape(1, n, t, N_HEADS, HEAD_DIM)
    heads_t = lambda z: jnp.transpose(z.reshape(N_HEADS, HEAD_DIM, tp), (2, 0, 1))[None, None]
    return (y_p[None], y_s[:, None],
            heads_t(k_p), heads_t(v_p), heads(k_s, ns, 1), heads(v_s, ns, 1),
            wkv_p[None, None], wkv_s[None], shift_p[None], shift_s[None])
```

```python
import functools

import jax
import jax.numpy as jnp
from jax import lax
from jax.experimental import pallas as pl
from jax.experimental.pallas import tpu as pltpu

F32 = jnp.float32
MXU_DT = jnp.bfloat16

D_MODEL = 1024
PAGE_SIZE = 128
N_HEADS = 8
HEAD_DIM = 64
WIDTH = N_HEADS * HEAD_DIM
LORA_W = 64
LORA_A = 64
LORA_G = 128
R_PROJ = 3 * WIDTH + LORA_W + LORA_A + LORA_G
GN_EPS = 64e-5
MOBA_BLOCK = 256
MOBA_TOPK = 3
ROT_DIM = HEAD_DIM // 4
ROPE_THETA = 500000.0
IN_PROJ = R_PROJ + 3 * WIDTH + 2 * D_MODEL
D_FF = 2816
LN_EPS = 1e-5
DN_ALPHA = 2.0 ** 0.25
CHUNK = 64
LANES = 128
VMEM_LIMIT = 56 * 1024 * 1024
SAMPLE_PAGES_PER_STEP = 64
FFN_ROWS = 1024
FFN_CHUNKS = 11
RWKV_GROUP = 4
VT_ROWS = HEAD_DIM + 16

NEG_INF = float("-inf")
LOG2_E = 1.4426950408889634


def _cparams(*sem, vmem=VMEM_LIMIT):
    return pltpu.CompilerParams(dimension_semantics=sem, vmem_limit_bytes=vmem)


def _const_spec(shape):
    return pl.BlockSpec(shape, lambda *_: (0,) * len(shape), pipeline_mode=pl.Buffered(1))


def _row_spec(tm, width):
    return pl.BlockSpec((tm, width), lambda i: (i, 0))


def _mod_spec(arr, tm):
    if arr.shape[0] == 1:
        return pl.BlockSpec((1, arr.shape[1]), lambda i: (0, 0))
    return pl.BlockSpec((tm, arr.shape[1]), lambda i: (i, 0))


def _dot(a, b):
    return jnp.dot(a.astype(MXU_DT), b.astype(MXU_DT), preferred_element_type=F32)


_NN = (((1,), (0,)), ((), ()))
_NT = (((1,), (1,)), ((), ()))


def _dg(a, b, dn):
    return lax.dot_general(a.astype(MXU_DT), b.astype(MXU_DT), dn, preferred_element_type=F32)


def _split(x):
    hi = x.astype(MXU_DT)
    lo = (x - hi.astype(F32)).astype(MXU_DT)
    return hi, lo


def _seg_sum(x, ones):
    hi, lo = _split(x)
    return (jnp.dot(hi, ones, preferred_element_type=F32)
            + jnp.dot(lo, ones, preferred_element_type=F32))


def _layer_norm(y, g, b):
    mu = jnp.mean(y, axis=-1, keepdims=True)
    yc = y - mu
    var = jnp.mean(yc * yc, axis=-1, keepdims=True)
    return yc * lax.rsqrt(var + LN_EPS) * g + b


def _ada_kernel(c_ref, w_ref, b_ref, o_ref):
    c = c_ref[...]
    o_ref[...] = _dot(c * jax.nn.sigmoid(c), w_ref[...]) + b_ref[...]


def _ada(c, w_ada, b_ada):
    n, d = c.shape
    nout = w_ada.shape[1]
    tn = 1024
    return pl.pallas_call(
        _ada_kernel,
        grid=(nout // tn,),
        in_specs=[pl.BlockSpec((n, d), lambda j: (0, 0)),
                  pl.BlockSpec((d, tn), lambda j: (0, j)),
                  pl.BlockSpec((1, tn), lambda j: (0, j))],
        out_specs=pl.BlockSpec((n, tn), lambda j: (0, j)),
        out_shape=jax.ShapeDtypeStruct((n, nout), F32),
        compiler_params=_cparams("parallel"),
    )(c, w_ada, b_ada)


def _ffn_kernel(x_ref, sh_ref, sc_ref, gt_ref, wg_ref, wu_ref, wd_ref, lg_ref, lb_ref, o_ref, *, n_ff):
    x = x_ref[...]
    h = (x * (1.0 + sc_ref[...]) + sh_ref[...]).astype(MXU_DT)
    step = D_FF // n_ff
    f = None
    for j in range(n_ff):
        cs = slice(j * step, (j + 1) * step)
        g = jnp.dot(h, wg_ref[:, cs], preferred_element_type=F32)
        u = jnp.dot(h, wu_ref[:, cs], preferred_element_type=F32)
        a = (g * jax.nn.sigmoid(g) * u).astype(MXU_DT)
        part = jnp.dot(a, wd_ref[cs, :], preferred_element_type=F32)
        f = part if f is None else f + part
    y = DN_ALPHA * x + (0.5 * gt_ref[...]) * f
    o_ref[...] = _layer_norm(y, lg_ref[...], lb_ref[...])


def _ffn(x, shift, scale, gate, wg, wu, wd, ln_g, ln_b, tm):
    t = x.shape[0]
    return pl.pallas_call(
        functools.partial(_ffn_kernel, n_ff=FFN_CHUNKS),
        grid=(t // tm,),
        in_specs=[_row_spec(tm, D_MODEL), _mod_spec(shift, tm), _mod_spec(scale, tm), _mod_spec(gate, tm),
                  _const_spec((D_MODEL, D_FF)), _const_spec((D_MODEL, D_FF)), _const_spec((D_FF, D_MODEL)),
                  _const_spec((1, D_MODEL)), _const_spec((1, D_MODEL))],
        out_specs=_row_spec(tm, D_MODEL),
        out_shape=jax.ShapeDtypeStruct((t, D_MODEL), F32),
        compiler_params=_cparams("parallel"),
    )(x, shift, scale, gate, wg, wu, wd, ln_g, ln_b)


def _rotate(x, cos, s_up, s_dn):
    return x * cos + pltpu.roll(x, LANES - ROT_DIM // 2, 1) * s_up + pltpu.roll(x, ROT_DIM // 2, 1) * s_dn


def _inproj_kernel(x_ref, sh_ref, sc_ref, w_ref, cos_ref, sup_ref, sdn_ref,
                   pr_ref, q_ref, k_ref, v_ref, gr_ref, gm_ref, kb_ref, vt_ref, km_ref, *, q_scale):
    h = (x_ref[...] * (1.0 + sc_ref[...]) + sh_ref[...]).astype(MXU_DT)
    tm = h.shape[0]

    def proj(lo, width):
        return jnp.dot(h, w_ref[:, lo:lo + width], preferred_element_type=F32)

    pr_ref[...] = proj(0, R_PROJ)
    cos, s_up, s_dn = cos_ref[...], sup_ref[...], sdn_ref[...]
    q = proj(R_PROJ, WIDTH)
    k = proj(R_PROJ + WIDTH, WIDTH)
    k_rot = []
    for j in range(WIDTH // LANES):
        ls = slice(j * LANES, (j + 1) * LANES)
        qj = _rotate(q[:, ls], cos, s_up, s_dn)
        kj = _rotate(k[:, ls], cos, s_up, s_dn)
        q_ref[:, ls] = (qj * q_scale).astype(MXU_DT)
        kb_ref[:, ls] = kj.astype(MXU_DT)
        k_rot.append(kj)
        if km_ref is not None:
            km_ref[0, :, ls] = jnp.sum(kj.reshape(tm // MOBA_BLOCK, MOBA_BLOCK, LANES), axis=1) * (1.0 / MOBA_BLOCK)
    k_rot = jnp.concatenate(k_rot, axis=1)
    v = proj(R_PROJ + 2 * WIDTH, WIDTH)
    if vt_ref is None:
        k_ref[...] = k_rot
        v_ref[...] = v
    else:
        v_t = v.T
        k_ref[...] = k_rot.T
        v_ref[...] = v_t
        for head in range(N_HEADS):
            base = head * VT_ROWS
            vt_ref[base:base + HEAD_DIM, :] = v_t[head * HEAD_DIM:(head + 1) * HEAD_DIM, :].astype(MXU_DT)
            vt_ref[base + HEAD_DIM:base + VT_ROWS, :] = jnp.ones((VT_ROWS - HEAD_DIM, tm), MXU_DT)
    gr_ref[...] = proj(R_PROJ + 3 * WIDTH, D_MODEL)
    gm_ref[...] = proj(R_PROJ + 3 * WIDTH + D_MODEL, D_MODEL)


def _inproj_prompt_kernel(*refs):
    _inproj_kernel(*refs, q_scale=HEAD_DIM ** -0.5 * LOG2_E)


def _inproj_sample_kernel(*refs):
    _inproj_kernel(*refs, None, None, q_scale=HEAD_DIM ** -0.5)


def _inproj(x, shift, scale, w_in, cos, s_up, s_dn, tm, prompt):
    t = x.shape[0]
    out_shape = [jax.ShapeDtypeStruct((t, R_PROJ), F32),
                 jax.ShapeDtypeStruct((t, WIDTH), MXU_DT),
                 jax.ShapeDtypeStruct((t, WIDTH), F32),
                 jax.ShapeDtypeStruct((t, WIDTH), F32),
                 jax.ShapeDtypeStruct((t, D_MODEL), F32),
                 jax.ShapeDtypeStruct((t, D_MODEL), F32),
                 jax.ShapeDtypeStruct((t, WIDTH), MXU_DT)]
    out_specs = [_row_spec(tm, R_PROJ), _row_spec(tm, WIDTH), _row_spec(tm, WIDTH), _row_spec(tm, WIDTH),
                 _row_spec(tm, D_MODEL), _row_spec(tm, D_MODEL), _row_spec(tm, WIDTH)]
    if prompt:
        nblk = tm // MOBA_BLOCK
        vt_rows = N_HEADS * VT_ROWS
        out_shape[2] = out_shape[3] = jax.ShapeDtypeStruct((WIDTH, t), F32)
        out_specs[2] = out_specs[3] = pl.BlockSpec((WIDTH, tm), lambda i: (0, i))
        out_shape += [jax.ShapeDtypeStruct((vt_rows, t), MXU_DT),
                      jax.ShapeDtypeStruct((t // tm, nblk, WIDTH), F32)]
        out_specs += [pl.BlockSpec((vt_rows, tm), lambda i: (0, i)),
                      pl.BlockSpec((1, nblk, WIDTH), lambda i: (i, 0, 0))]
    return pl.pallas_call(
        _inproj_prompt_kernel if prompt else _inproj_sample_kernel,
        grid=(t // tm,),
        in_specs=[_row_spec(tm, D_MODEL), _mod_spec(shift, tm), _mod_spec(scale, tm),
                  _const_spec((D_MODEL, IN_PROJ)),
                  _row_spec(tm, LANES), _row_spec(tm, LANES), _row_spec(tm, LANES)],
        out_specs=out_specs,
        out_shape=out_shape,
        compiler_params=_cparams("parallel"),
    )(x, shift, scale, w_in, cos, s_up, s_dn)


def _rope_tables(pos):
    half = ROT_DIM // 2
    inv = ROPE_THETA ** (-jnp.arange(half, dtype=F32) * 2.0 / ROT_DIM)
    ang = pos.astype(F32)[:, None] * inv[None, :]
    cos, sin = jnp.cos(ang), jnp.sin(ang)
    t = pos.shape[0]
    zeros = lambda n: jnp.zeros((t, n), F32)
    per_head = lambda parts: jnp.tile(jnp.concatenate(parts, axis=1), (1, LANES // HEAD_DIM))
    c = per_head([cos, cos, jnp.ones((t, HEAD_DIM - ROT_DIM), F32)])
    s_up = per_head([-sin, zeros(HEAD_DIM - half)])
    s_dn = per_head([zeros(half), sin, zeros(HEAD_DIM - ROT_DIM)])
    return c, s_up, s_dn


def _rwkv_pointwise(pr, prev, mu, wcat, w0, a0, g2, k_k, k_a, ones):
    xs = pr + (prev - pr) * mu
    r = xs[:, 0:WIDTH]
    k = xs[:, WIDTH:2 * WIDTH]
    v = xs[:, 2 * WIDTH:3 * WIDTH]
    la = xs[:, 3 * WIDTH:3 * WIDTH + LORA_W + LORA_A]
    gl = xs[:, 3 * WIDTH + LORA_W + LORA_A:R_PROJ]
    lane = lax.broadcasted_iota(jnp.int32, la.shape, 1)
    la = jnp.where(lane < LORA_W, jnp.tanh(la), la)
    wa = _dot(la, wcat)
    w = w0 + wa[:, :WIDTH]
    a = jax.nn.sigmoid(a0 + wa[:, WIDTH:])
    g = _dot(jax.nn.sigmoid(gl), g2)
    z = -w
    softplus = jnp.maximum(z, 0.0) + jnp.log1p(jnp.exp(-jnp.abs(z)))
    log_decay = -jnp.exp(-softplus - 0.5)
    kk = k * k_k
    kk = kk / jnp.maximum(jnp.sqrt(_seg_sum(kk * kk, ones)), 1e-12)
    k_eff = k * (1.0 + (a - 1.0) * k_a)
    return r, k_eff, v, kk, kk * a, log_decay, g


def _rwkv_post(y, r, k_eff, v, g, r_k, lnx_g, lnx_b, ones):
    inv = 1.0 / HEAD_DIM
    yc = y - _seg_sum(y, ones) * inv
    yv = _seg_sum(yc * yc, ones) * inv
    yn = yc * lax.rsqrt(yv + GN_EPS) * lnx_g + lnx_b
    bonus = _seg_sum(r * k_eff * r_k, ones) * v
    return (yn + bonus) * g


def _rwkv_prompt_kernel(pr_ref, shift0_ref, s0_ref, mu_ref, wcat_ref, w0_ref, a0_ref, g2_ref, kk_ref, ka_ref,
                        rk_ref, lg_ref, lb_ref, ones_ref, tri_ref,
                        out_ref, sfin_ref, shift_ref,
                        prev_scr, st_scr, rt_scr, kp_scr, bt_scr, kt_scr, v_scr, bc_scr, kc_scr, pc_scr, y_scr):
    i = pl.program_id(0)

    @pl.when(i == 0)
    def _():
        prev_scr[...] = shift0_ref[...]
        st_scr[...] = s0_ref[...]

    pr = pr_ref[...]
    tm = pr.shape[0]
    row = lax.broadcasted_iota(jnp.int32, (tm, 1), 0)
    prev = jnp.where(row == 0, prev_scr[...], pltpu.roll(pr, 1, 0))
    ones = ones_ref[...]
    r, k_eff, v, kk, b, lw, g = _rwkv_pointwise(pr, prev, mu_ref[...], wcat_ref[...], w0_ref[...], a0_ref[...],
                                                g2_ref[...], kk_ref[...], ka_ref[...], ones)
    l_hi, l_lo = _split(lw)
    tri = tri_ref[...]
    L = jnp.dot(tri, l_hi, preferred_element_type=F32) + jnp.dot(tri, l_lo, preferred_element_type=F32)
    n_chunks = tm // CHUNK
    LC = jnp.broadcast_to(L.reshape(n_chunks, CHUNK, WIDTH)[:, CHUNK - 1:CHUNK, :],
                          (n_chunks, CHUNK, WIDTH)).reshape(tm, WIDTH)
    e_inv = jnp.exp(-L)
    e_end = jnp.exp(LC - L)
    rt_scr[...] = r * jnp.exp(L)
    kp_scr[...] = kk * jnp.exp(L - lw)
    bt_scr[...] = b * e_inv
    kt_scr[...] = k_eff * e_inv
    bc_scr[...] = b * e_end
    kc_scr[...] = k_eff * e_end
    pc_scr[...] = jnp.exp(LC)
    v_scr[...] = v

    ri = lax.broadcasted_iota(jnp.int32, (CHUNK, CHUNK), 0)
    ci = lax.broadcasted_iota(jnp.int32, (CHUNK, CHUNK), 1)
    strict, incl, eye = ri > ci, ri >= ci, ri == ci

    each = lambda fn, *lists: [fn(*args) for args in zip(*lists)]

    def solve(chunks):
        probs = [(c, h) for c in chunks for h in range(N_HEADS)]

        def blocks(ref):
            return [ref[c * CHUNK:(c + 1) * CHUNK, h * HEAD_DIM:(h + 1) * HEAD_DIM] for c, h in probs]

        def blocks_t(ref):
            per_chunk = {c: ref[c * CHUNK:(c + 1) * CHUNK, :].T for c in chunks}
            return [per_chunk[c][h * HEAD_DIM:(h + 1) * HEAD_DIM, :] for c, h in probs]

        kp, rt, v_b = blocks(kp_scr), blocks(rt_scr), blocks(v_scr)
        bk = each(lambda a, b_: jnp.concatenate([a, b_], axis=0), blocks(bt_scr), blocks(kt_scr))
        bk = each(lambda b_: b_.astype(MXU_DT), bk)
        v_m = each(lambda a: a.astype(MXU_DT), v_b)
        akk = each(lambda a, b_: _dg(a, b_, _NT), kp, bk)
        arr = each(lambda a, b_: _dg(a, b_, _NT), rt, bk)
        a_zb = [jnp.where(strict, a[:, :CHUNK], 0.0) for a in akk]
        a_vk = [jnp.where(strict, a[:, CHUNK:], 0.0) for a in akk]
        a_rb = [jnp.where(incl, a[:, :CHUNK], 0.0) for a in arr]
        a_rk = [jnp.where(incl, a[:, CHUNK:], 0.0) for a in arr]
        u0 = each(lambda a, w, v_: jnp.concatenate([a, _dg(w, v_, _NN)], axis=1), kp, a_vk, v_m)
        half = lambda s: (ri // (2 * s) == ci // (2 * s)) & (ri % (2 * s) >= s) & (ci % (2 * s) < s)
        t_inv = [jnp.where(eye, 1.0, 0.0) - jnp.where(half(1), a, 0.0) for a in a_zb]
        s_blk = 2
        while s_blk < CHUNK:
            lower = half(s_blk)
            t_inv = each(lambda t_, a: t_ - _dg(_dg(t_, jnp.where(lower, a, 0.0), _NN), t_, _NN), t_inv, a_zb)
            s_blk *= 2
        u = each(lambda t_, u_: _dg(t_, u_, _NN), t_inv, u0)
        u_m = each(lambda a: a.astype(MXU_DT), u)
        gmat = each(lambda a, u_: _dg(a, u_, _NN), a_rb, u_m)
        r_eff = each(lambda a, g_: a - g_[:, :HEAD_DIM], rt, gmat)
        y0 = each(lambda a, b_, g_: _dg(a, b_, _NN) - g_[:, HEAD_DIM:], a_rk, v_m, gmat)
        bu = each(lambda a, u_: _dg(a, u_, _NN), blocks_t(bc_scr), u_m)
        pc = [pc_scr[c * CHUNK:c * CHUNK + 1, h * HEAD_DIM:(h + 1) * HEAD_DIM] for c, h in probs]
        m_t = each(lambda p_, b_: jnp.where(eye, p_, 0.0) - b_[:, :HEAD_DIM], pc, bu)
        n_t = each(lambda a, v_, c_: _dg(a, v_, _NN) - c_[:, HEAD_DIM:], blocks_t(kc_scr), v_m, bu)
        return r_eff, y0, m_t, n_t

    parts = [solve(range(c0, c0 + RWKV_GROUP)) for c0 in range(0, n_chunks, RWKV_GROUP)]
    r_eff, y0, m_t, n_t = ([x for part in parts for x in part[i]] for i in range(4))
    st = [st_scr[h] for h in range(N_HEADS)]
    for c in range(n_chunks):
        sl = slice(c * N_HEADS, (c + 1) * N_HEADS)
        st_m = each(lambda a: a.astype(MXU_DT), st)
        y_scr[c * CHUNK:(c + 1) * CHUNK, :] = jnp.concatenate(
            each(lambda a, s_, b_: _dg(a, s_, _NN) + b_, r_eff[sl], st_m, y0[sl]), axis=1)
        st = each(lambda m_, s_, n_: _dg(m_, s_, _NN) + n_, m_t[sl], st_m, n_t[sl])
    for h in range(N_HEADS):
        st_scr[h] = st[h]
    out_ref[...] = _rwkv_post(y_scr[...], r, k_eff, v, g, rk_ref[...], lg_ref[...], lb_ref[...], ones)
    prev_scr[...] = pr[tm - 1:tm, :]
    shift_ref[...] = pr[tm - 1:tm, :]

    @pl.when(i == pl.num_programs(0) - 1)
    def _():
        for h in range(N_HEADS):
            sfin_ref[h] = st[h].T


def _rwkv_prompt(pr, shift0, s0, rp, tm):
    t = pr.shape[0]
    ch = lax.broadcasted_iota(jnp.int32, (tm, tm), 0) // CHUNK == lax.broadcasted_iota(jnp.int32, (tm, tm), 1) // CHUNK
    low = lax.broadcasted_iota(jnp.int32, (tm, tm), 0) >= lax.broadcasted_iota(jnp.int32, (tm, tm), 1)
    tri = (ch & low).astype(MXU_DT)
    vec = _const_spec((1, WIDTH))
    tile = pltpu.VMEM((tm, WIDTH), F32)
    return pl.pallas_call(
        _rwkv_prompt_kernel,
        grid=(t // tm,),
        in_specs=[_row_spec(tm, R_PROJ), _const_spec((1, R_PROJ)), _const_spec((N_HEADS, HEAD_DIM, HEAD_DIM)),
                  _const_spec((1, R_PROJ)), _const_spec((LORA_W + LORA_A, 2 * WIDTH)), vec, vec,
                  _const_spec((LORA_G, WIDTH)), vec, vec, vec, vec, vec,
                  _const_spec((WIDTH, WIDTH)), _const_spec((tm, tm))],
        out_specs=[_row_spec(tm, WIDTH),
                   pl.BlockSpec((N_HEADS, HEAD_DIM, HEAD_DIM), lambda i: (0, 0, 0)),
                   pl.BlockSpec((1, R_PROJ), lambda i: (0, 0))],
        out_shape=[jax.ShapeDtypeStruct((t, WIDTH), F32),
                   jax.ShapeDtypeStruct((N_HEADS, HEAD_DIM, HEAD_DIM), F32),
                   jax.ShapeDtypeStruct((1, R_PROJ), F32)],
        scratch_shapes=[pltpu.VMEM((1, R_PROJ), F32), pltpu.VMEM((N_HEADS, HEAD_DIM, HEAD_DIM), F32)] + [tile] * 9,
        compiler_params=_cparams("arbitrary"),
    )(pr, shift0, jnp.swapaxes(s0, 1, 2), rp["mu"], rp["wcat"], rp["w0"], rp["a0"], rp["g2"], rp["k_k"], rp["k_a"],
      rp["r_k"], rp["lnx_g"], rp["lnx_b"], rp["ones"], tri)


def _rwkv_step_kernel(pr_ref, shift_ref, s_ref, mu_ref, wcat_ref, w0_ref, a0_ref, g2_ref, kk_ref, ka_ref,
                      rk_ref, lg_ref, lb_ref, ones_ref, out_ref, snew_ref):
    ones = ones_ref[...]
    r, k_eff, v, kk, b, lw, g = _rwkv_pointwise(pr_ref[...], shift_ref[...], mu_ref[...], wcat_ref[...],
                                                w0_ref[...], a0_ref[...], g2_ref[...], kk_ref[...], ka_ref[...], ones)
    decay = jnp.exp(lw)
    eye = (lax.broadcasted_iota(jnp.int32, (HEAD_DIM, HEAD_DIM), 0)
           == lax.broadcasted_iota(jnp.int32, (HEAD_DIM, HEAD_DIM), 1))[None]
    ys = []
    for h in range(N_HEADS):
        over_v = lambda z: z[:, None, h * HEAD_DIM:(h + 1) * HEAD_DIM]
        s = s_ref[:, h]
        s_kk = jnp.sum(s * over_v(kk), axis=2, keepdims=True)
        v_col = jnp.sum(jnp.where(eye, over_v(v), 0.0), axis=2, keepdims=True)
        s_new = s * over_v(decay) - s_kk * over_v(b) + v_col * over_v(k_eff)
        y_col = jnp.sum(s_new * over_v(r), axis=2, keepdims=True)
        ys.append(jnp.sum(jnp.where(eye, y_col, 0.0), axis=1))
        snew_ref[:, h] = s_new
    y = jnp.concatenate(ys, axis=1)
    out_ref[...] = _rwkv_post(y, r, k_eff, v, g, rk_ref[...], lg_ref[...], lb_ref[...], ones)


def _rwkv_step(pr, shift, state, rp, nb):
    n = pr.shape[0]
    vec = _const_spec((1, WIDTH))
    state_spec = pl.BlockSpec((nb, N_HEADS, HEAD_DIM, HEAD_DIM), lambda i: (i, 0, 0, 0))
    return pl.pallas_call(
        _rwkv_step_kernel,
        grid=(n // nb,),
        in_specs=[_row_spec(nb, R_PROJ), _row_spec(nb, R_PROJ), state_spec,
                  _const_spec((1, R_PROJ)), _const_spec((LORA_W + LORA_A, 2 * WIDTH)), vec, vec,
                  _const_spec((LORA_G, WIDTH)), vec, vec, vec, vec, vec, _const_spec((WIDTH, WIDTH))],
        out_specs=[_row_spec(nb, WIDTH), state_spec],
        out_shape=[jax.ShapeDtypeStruct((n, WIDTH), F32), jax.ShapeDtypeStruct(state.shape, F32)],
        compiler_params=_cparams("parallel"),
    )(pr, shift, state, rp["mu"], rp["wcat"], rp["w0"], rp["a0"], rp["g2"], rp["k_k"], rp["k_a"], rp["r_k"],
      rp["lnx_g"], rp["lnx_b"], rp["ones"])


def _moba_prompt_kernel(q_ref, km_ref, k_ref, vt_ref, o_ref, qm_scr, sel_scr, m_scr, acc_scr,
                        s0_scr, s1_scr, s2_scr, s3_scr, cm_scr):
    c = pl.program_id(0)
    tq = q_ref.shape[0]
    nblk = km_ref.shape[0]
    lane = lax.broadcasted_iota(jnp.int32, (1, LANES), 1)
    blk_id = lax.broadcasted_iota(jnp.int32, (nblk, tq), 0)
    key_i = lax.broadcasted_iota(jnp.int32, (MOBA_BLOCK, tq), 0)
    qry_i = lax.broadcasted_iota(jnp.int32, (MOBA_BLOCK, tq), 1)

    def head_lanes(h):
        return (lane < HEAD_DIM) if h % 2 == 0 else (lane >= HEAD_DIM)

    def values(h, b):
        rows = slice(h * VT_ROWS, (h + 1) * VT_ROWS)
        return vt_ref[rows, pl.ds(pl.multiple_of(b * MOBA_BLOCK, MOBA_BLOCK), MOBA_BLOCK)]

    def keys(h, b):
        ps = slice((h // 2) * LANES, (h // 2 + 1) * LANES)
        return k_ref[pl.ds(pl.multiple_of(b * MOBA_BLOCK, MOBA_BLOCK), MOBA_BLOCK), ps]

    all_heads = range(N_HEADS)
    pair_lanes = lambda h: slice((h // 2) * LANES, (h // 2 + 1) * LANES)
    for h in all_heads:
        qm_scr[h] = jnp.where(head_lanes(h), q_ref[:, pair_lanes(h)].astype(F32), 0.0).astype(MXU_DT)
    gates = [jnp.where(blk_id < c, lax.dot_general(km_ref[:, pair_lanes(h)], qm_scr[h], _NT,
                                                   preferred_element_type=F32), NEG_INF)
             for h in all_heads]
    sels = [jnp.zeros(gates[0].shape, jnp.bool_) for _ in all_heads]
    for _ in range(MOBA_TOPK):
        tops = [jnp.max(g_, axis=0, keepdims=True) for g_ in gates]
        hits = [blk_id == jnp.min(jnp.where(g_ == t_, blk_id, nblk), axis=0, keepdims=True)
                for g_, t_ in zip(gates, tops)]
        sels = [s_ | h_ for s_, h_ in zip(sels, hits)]
        gates = [jnp.where(h_, NEG_INF, g_) for g_, h_ in zip(gates, hits)]
    for h in all_heads:
        sel_scr[h] = sels[h].astype(F32)
    for h in all_heads:
        s = lax.dot_general(keys(h, c), qm_scr[h], _NT, preferred_element_type=F32)
        s = jnp.where(key_i <= qry_i, s, NEG_INF)
        s0_scr[h] = s
        m_scr[h] = jnp.max(s, axis=0, keepdims=True)
    for h in all_heads:
        p = jnp.exp2(s0_scr[h] - m_scr[h]).astype(MXU_DT)
        acc_scr[h] = jnp.dot(values(h, c), p, preferred_element_type=F32)

    def scores(b, s_scr, slot):
        for h in range(N_HEADS):
            s = lax.dot_general(keys(h, b), qm_scr[h], _NT, preferred_element_type=F32)
            s_scr[h] = s
            cm_scr[slot, h] = jnp.max(s, axis=0, keepdims=True)

    def absorb(b, s_scr, slot):
        for h in range(N_HEADS):
            picked = sel_scr[h, pl.ds(b, 1), :] > 0.0
            m_old = m_scr[h]
            m_new = jnp.where(picked, jnp.maximum(m_old, cm_scr[slot, h]), m_old)
            p = jnp.exp2(s_scr[h] - jnp.where(picked, m_new, jnp.inf)).astype(MXU_DT)
            acc_scr[h] = acc_scr[h] * jnp.exp2(m_old - m_new) + jnp.dot(values(h, b), p, preferred_element_type=F32)
            m_scr[h] = m_new

    bufs = (s0_scr, s1_scr, s2_scr, s3_scr)

    def block_group(i, carry):
        for j, buf in enumerate(bufs):
            scores(len(bufs) * i + j, buf, j)
        for j, buf in enumerate(bufs):
            absorb(len(bufs) * i + j, buf, j)
        return carry

    def single_block(b, carry):
        scores(b, s0_scr, 0)
        absorb(b, s0_scr, 0)
        return carry

    n_groups = c // len(bufs)
    lax.fori_loop(0, n_groups, block_group, 0)
    lax.fori_loop(n_groups * len(bufs), c, single_block, 0)

    for pair in range(N_HEADS // 2):
        outs = []
        for h in (2 * pair, 2 * pair + 1):
            a = acc_scr[h]
            outs.append((a[:HEAD_DIM] / a[HEAD_DIM:HEAD_DIM + 1]).T)
        o_ref[:, pair * LANES:(pair + 1) * LANES] = jnp.concatenate(outs, axis=1)


def _moba_prompt(q, kmean, kb, vt):
    t = q.shape[0]
    nblk = kmean.shape[0]
    tq = MOBA_BLOCK
    return pl.pallas_call(
        _moba_prompt_kernel,
        grid=(t // tq,),
        in_specs=[_row_spec(tq, WIDTH), _const_spec((nblk, WIDTH)), _const_spec((t, WIDTH)),
                  _const_spec(vt.shape)],
        out_specs=_row_spec(tq, WIDTH),
        out_shape=jax.ShapeDtypeStruct((t, WIDTH), F32),
        scratch_shapes=[pltpu.VMEM((N_HEADS, tq, LANES), MXU_DT), pltpu.VMEM((N_HEADS, nblk, tq), F32),
                        pltpu.VMEM((N_HEADS, 1, tq), F32), pltpu.VMEM((N_HEADS, VT_ROWS, tq), F32),
                        ] + [pltpu.VMEM((N_HEADS, MOBA_BLOCK, tq), F32)] * 4 + [
                        pltpu.VMEM((4, N_HEADS, 1, tq), F32)],
        compiler_params=_cparams("parallel"),
    )(q, kmean, kb, vt)


def _moba_sample_score_kernel(pt_ref, q_ref, knew_ref, *refs):
    del pt_ref
    pg = SAMPLE_PAGES_PER_STEP
    pages = refs[:pg]
    p_ref, pself_ref, idx_ref, sc_scr, gate_scr = refs[pg:]
    j = pl.program_id(1)
    nblk = gate_scr.shape[1]
    rnd = lambda z: z.astype(MXU_DT).astype(F32)
    q = q_ref[0]
    for t in range(pg // 2):
        halves = (pages[2 * t][0], pages[2 * t + 1][0])
        sc = jnp.concatenate([jnp.sum(page * q, axis=1, keepdims=True) for page in halves], axis=2)
        gate = jnp.broadcast_to(jnp.sum(sc, axis=2, keepdims=True), (N_HEADS, 1, LANES))
        blk = pl.ds(j * (pg // 2) + t, 1)
        for h in range(N_HEADS):
            sc_scr[h, blk, :] = sc[h]
            gate_scr[h, blk, :] = gate[h]

    @pl.when(j == pl.num_programs(1) - 1)
    def _():
        gate = gate_scr[...]
        blk_id = lax.broadcasted_iota(jnp.int32, gate.shape, 1)
        sel = jnp.zeros(gate.shape, jnp.bool_)
        for r in range(MOBA_TOPK):
            top = jnp.max(gate, axis=1, keepdims=True)
            idx = jnp.min(jnp.where(gate == top, blk_id, nblk), axis=1, keepdims=True)
            hit = blk_id == idx
            sel = sel | hit
            gate = jnp.where(hit, NEG_INF, gate)
            idx_ref[0, r] = idx
        sel_f = jnp.where(sel, 1.0, 0.0)
        picked = jnp.concatenate([sel_f] * (MOBA_BLOCK // LANES), axis=2) > 0.0
        s = jnp.where(picked, sc_scr[...], NEG_INF)
        s_self = jnp.sum(q * rnd(knew_ref[0]), axis=1, keepdims=True)
        m = jnp.maximum(jnp.max(jnp.max(s, axis=2, keepdims=True), axis=1, keepdims=True), s_self)
        p = jnp.exp(s - m[:, :, :1])
        p_self = jnp.exp(s_self - m)
        inv = 1.0 / (jnp.sum(jnp.sum(p, axis=2, keepdims=True), axis=1, keepdims=True) + p_self)
        p_ref[0] = p * inv[:, :, :1]
        pself_ref[0] = p_self * inv


def _moba_sample_scores(q_lanes, knew_lanes, cache_kt, page_table):
    n, n_pages = page_table.shape
    pg = SAMPLE_PAGES_PER_STEP
    nblk = n_pages * PAGE_SIZE // MOBA_BLOCK
    page_block = (1, N_HEADS, HEAD_DIM, PAGE_SIZE)

    def page_spec(e):
        return pl.BlockSpec(page_block, lambda i, j, pt: (pt[i * n_pages + j * pg + e], 0, 0, 0))

    per_seq = pl.BlockSpec(page_block, lambda i, j, pt: (i, 0, 0, 0))
    grid_spec = pltpu.PrefetchScalarGridSpec(
        num_scalar_prefetch=1,
        grid=(n, n_pages // pg),
        in_specs=[per_seq, per_seq] + [page_spec(e) for e in range(pg)],
        out_specs=[pl.BlockSpec((1, N_HEADS, nblk, MOBA_BLOCK), lambda i, j, pt: (i, 0, 0, 0)),
                   pl.BlockSpec((1, N_HEADS, 1, LANES), lambda i, j, pt: (i, 0, 0, 0)),
                   pl.BlockSpec((1, MOBA_TOPK, N_HEADS, 1, LANES), lambda i, j, pt: (i, 0, 0, 0, 0))],
        scratch_shapes=[pltpu.VMEM((N_HEADS, nblk, MOBA_BLOCK), F32), pltpu.VMEM((N_HEADS, nblk, LANES), F32)],
    )
    return pl.pallas_call(
        _moba_sample_score_kernel,
        grid_spec=grid_spec,
        out_shape=[jax.ShapeDtypeStruct((n, N_HEADS, nblk, MOBA_BLOCK), F32),
                   jax.ShapeDtypeStruct((n, N_HEADS, 1, LANES), F32),
                   jax.ShapeDtypeStruct((n, MOBA_TOPK, N_HEADS, 1, LANES), jnp.int32)],
        compiler_params=_cparams("parallel", "arbitrary"),
    )(page_table.reshape(-1), q_lanes, knew_lanes, *([cache_kt] * pg))


def _moba_sample_value_kernel(phys_ref, blk_ref, p_ref, pself_ref, vnew_ref, *refs):
    del phys_ref
    pages_per_block = MOBA_BLOCK // PAGE_SIZE
    v_refs, o_ref = refs[:-1], refs[-1]
    i = pl.program_id(0)
    rnd = lambda z: z.astype(MXU_DT).astype(F32)
    for h in range(N_HEADS):
        acc = rnd(pself_ref[0, h][:, :HEAD_DIM]) * rnd(vnew_ref[0, h])
        for r in range(MOBA_TOPK):
            slot = h * MOBA_TOPK + r
            p_row = p_ref[0, h, pl.ds(blk_ref[i * N_HEADS * MOBA_TOPK + slot], 1), :]
            for e in range(pages_per_block):
                p_rows = jnp.broadcast_to(p_row[:, e * PAGE_SIZE:(e + 1) * PAGE_SIZE], (N_HEADS, PAGE_SIZE))
                v_t = v_refs[slot * pages_per_block + e][0, 0]
                acc = acc + _dg(p_rows, v_t, _NT)[0:1]
        o_ref[0, h] = acc


def _moba_sample_values(p, p_self, v_new, cache_vt, phys, blocks):
    n, _, nblk, _ = p.shape
    pages_per_block = MOBA_BLOCK // PAGE_SIZE
    per_seq = N_HEADS * MOBA_TOPK * pages_per_block

    def page_spec(h, e):
        return pl.BlockSpec((1, 1, HEAD_DIM, PAGE_SIZE),
                            lambda i, ph, bk: (ph[i * per_seq + h * MOBA_TOPK * pages_per_block + e], h, 0, 0))

    whole = lambda shape: pl.BlockSpec((1,) + shape, lambda i, ph, bk: (i,) + (0,) * len(shape))
    grid_spec = pltpu.PrefetchScalarGridSpec(
        num_scalar_prefetch=2,
        grid=(n,),
        in_specs=[whole((N_HEADS, nblk, MOBA_BLOCK)), whole((N_HEADS, 1, LANES)), whole((N_HEADS, 1, HEAD_DIM))]
        + [page_spec(h, e) for h in range(N_HEADS) for e in range(MOBA_TOPK * pages_per_block)],
        out_specs=whole((N_HEADS, 1, HEAD_DIM)),
    )
    return pl.pallas_call(
        _moba_sample_value_kernel,
        grid_spec=grid_spec,
        out_shape=jax.ShapeDtypeStruct((n, N_HEADS, 1, HEAD_DIM), F32),
        compiler_params=_cparams("parallel"),
    )(phys, blocks, p, p_self, v_new.reshape(n, N_HEADS, 1, HEAD_DIM), *([cache_vt] * per_seq))


def _moba_sample(q, k_new, v_new, cache_kt, cache_vt, page_table):
    n = q.shape[0]
    lanes = lambda z: jnp.broadcast_to(z.astype(F32).reshape(n, N_HEADS, HEAD_DIM, 1),
                                       (n, N_HEADS, HEAD_DIM, PAGE_SIZE))
    p, p_self, idx = _moba_sample_scores(lanes(q), lanes(k_new), cache_kt, page_table)
    pages_per_block = MOBA_BLOCK // PAGE_SIZE
    blocks = jnp.transpose(idx[..., 0, 0], (0, 2, 1))
    logi = (blocks[..., None] * pages_per_block + jnp.arange(pages_per_block, dtype=jnp.int32)).reshape(
        n, N_HEADS, -1)
    phys = jnp.take_along_axis(page_table[:, None, :], logi, axis=2)
    out = _moba_sample_values(p, p_self, v_new, cache_vt, phys.reshape(-1), blocks.reshape(-1))
    return out.reshape(n, WIDTH)


def _merge_kernel(x_ref, r_ref, a_ref, gr_ref, gm_ref, gt_ref, wr_ref, wm_ref, wo_ref, lg_ref, lb_ref, o_ref):
    br = jax.nn.sigmoid(gr_ref[...]) * _dot(r_ref[...], wr_ref[...])
    bm = jax.nn.sigmoid(gm_ref[...]) * _dot(a_ref[...], wm_ref[...])
    merged = _dot(br + bm, wo_ref[...])
    o_ref[...] = _layer_norm(DN_ALPHA * x_ref[...] + gt_ref[...] * merged, lg_ref[...], lb_ref[...])


def _merge(x, r_out, att, g_r, g_m, gate, wr, wm, wo, ln_g, ln_b, tm):
    t = x.shape[0]
    return pl.pallas_call(
        _merge_kernel,
        grid=(t // tm,),
        in_specs=[_row_spec(tm, D_MODEL), _row_spec(tm, WIDTH), _row_spec(tm, WIDTH), _row_spec(tm, D_MODEL),
                  _row_spec(tm, D_MODEL), _mod_spec(gate, tm),
                  _const_spec((WIDTH, D_MODEL)), _const_spec((WIDTH, D_MODEL)), _const_spec((D_MODEL, D_MODEL)),
                  _const_spec((1, D_MODEL)), _const_spec((1, D_MODEL))],
        out_specs=_row_spec(tm, D_MODEL),
        out_shape=jax.ShapeDtypeStruct((t, D_MODEL), F32),
        compiler_params=_cparams("parallel"),
    )(x, r_out, att, g_r, g_m, gate, wr, wm, wo, ln_g, ln_b)


def _tile(t, want):
    return want if t % want == 0 else t


def _layer(x, mod, pos, wts, rp, rwkv_fn, moba_fn, prompt):
    t = x.shape[0]
    tm = _tile(t, 512)
    row = lambda z: z.reshape(1, -1)
    x1 = _ffn(x, mod[0], mod[1], mod[2], wts["ffn1_gate"], wts["ffn1_up"], wts["ffn1_down"],
              row(wts["ln1_g"]), row(wts["ln1_b"]), _tile(t, FFN_ROWS))
    cos, s_up, s_dn = _rope_tables(pos)
    proj = _inproj(x1, mod[3], mod[4], wts["w_in"], cos, s_up, s_dn, tm, prompt)
    pr, q, k_new, v_new, g_r, g_m, kb = proj[:7]
    r_out, wkv_new, shift_new = rwkv_fn(pr)
    att = moba_fn(q, k_new, v_new, kb, proj[7:])
    x2 = _merge(x1, r_out, att, g_r, g_m, mod[5], wts["w_br_rwkv"], wts["w_br_moba"], wts["w_out"],
                row(wts["ln2_g"]), row(wts["ln2_b"]), _tile(t, FFN_ROWS))
    y = _ffn(x2, mod[6], mod[7], mod[8], wts["ffn2_gate"], wts["ffn2_up"], wts["ffn2_down"],
             row(wts["ln3_g"]), row(wts["ln3_b"]), _tile(t, FFN_ROWS))
    return y, k_new, v_new, wkv_new, shift_new


def kernel(x_prompt, x_sample, cache_k, cache_v, state_rwkv_wkv, state_rwkv_shift, page_table, c_prompt, c_sample, w_ada, b_ada, ffn1_gate, ffn1_up, ffn1_down, ln1_g, ln1_b, w_in, mu_shift, rwkv_w0, rwkv_w2, rwkv_a0, rwkv_a2, rwkv_g2, rwkv_k_k, rwkv_k_a, rwkv_r_k, rwkv_lnx_g, rwkv_lnx_b, w_br_rwkv, w_br_moba, w_out, ln2_g, ln2_b, ffn2_gate, ffn2_up, ffn2_down, ln3_g, ln3_b):
    assert x_prompt.shape[0] == 1 and x_sample.shape[1] == 1 and w_ada.shape[0] == 1
    tp = x_prompt.shape[1]
    ns = x_sample.shape[0]
    n_pages = page_table.shape[1]
    past_len = n_pages * PAGE_SIZE
    assert tp % MOBA_BLOCK == 0 and past_len % MOBA_BLOCK == 0 and n_pages % SAMPLE_PAGES_PER_STEP == 0

    bf = lambda z: z[0].astype(MXU_DT)
    wts = {"ffn1_gate": bf(ffn1_gate), "ffn1_up": bf(ffn1_up), "ffn1_down": bf(ffn1_down),
           "ffn2_gate": bf(ffn2_gate), "ffn2_up": bf(ffn2_up), "ffn2_down": bf(ffn2_down),
           "w_in": bf(w_in), "w_br_rwkv": bf(w_br_rwkv), "w_br_moba": bf(w_br_moba), "w_out": bf(w_out),
           "ln1_g": ln1_g[0], "ln1_b": ln1_b[0], "ln2_g": ln2_g[0], "ln2_b": ln2_b[0],
           "ln3_g": ln3_g[0], "ln3_b": ln3_b[0]}
    zw = jnp.zeros((LORA_W, WIDTH), MXU_DT)
    head_of = jnp.arange(WIDTH) // HEAD_DIM
    row = lambda z: z.reshape(1, -1)
    rp = {"mu": row(mu_shift[0]),
          "wcat": jnp.concatenate([jnp.concatenate([bf(rwkv_w2), zw], axis=1),
                                   jnp.concatenate([zw, bf(rwkv_a2)], axis=1)], axis=0),
          "w0": row(rwkv_w0[0]), "a0": row(rwkv_a0[0]), "g2": bf(rwkv_g2),
          "k_k": row(rwkv_k_k[0]), "k_a": row(rwkv_k_a[0]), "r_k": row(rwkv_r_k[0]),
          "lnx_g": row(rwkv_lnx_g[0]), "lnx_b": row(rwkv_lnx_b[0]),
          "ones": (head_of[:, None] == head_of[None, :]).astype(MXU_DT)}

    n_mod = ((1 + ns + 7) // 8) * 8
    c_all = jnp.concatenate([c_prompt, c_sample, jnp.zeros((n_mod - 1 - ns, D_MODEL), F32)], axis=0)
    ada = _ada(c_all, w_ada[0], b_ada)
    mod_p = [ada[0:1, j * D_MODEL:(j + 1) * D_MODEL] for j in range(9)]
    mod_s = [ada[1:1 + ns, j * D_MODEL:(j + 1) * D_MODEL] for j in range(9)]

    def rwkv_p(pr):
        r_out, s_fin, shift = _rwkv_prompt(pr, jnp.zeros((1, R_PROJ), F32),
                                           jnp.zeros((N_HEADS, HEAD_DIM, HEAD_DIM), F32), rp, _tile(tp, 256))
        return r_out, s_fin, shift

    def moba_p(q, k_new, v_new, kb, extra):
        vt, kmean = extra
        return _moba_prompt(q, kmean.reshape(-1, WIDTH).astype(MXU_DT), kb, vt)

    y_p, k_p, v_p, wkv_p, shift_p = _layer(x_prompt[0], mod_p, jnp.arange(tp, dtype=jnp.int32), wts, rp,
                                           rwkv_p, moba_p, True)

    ck, cv = jnp.transpose(cache_k[0], (0, 2, 3, 1)), jnp.transpose(cache_v[0], (0, 2, 3, 1))

    def rwkv_s(pr):
        r_out, s_new = _rwkv_step(pr, state_rwkv_shift[0], state_rwkv_wkv[0], rp, _tile(ns, 8))
        return r_out, s_new, pr

    def moba_s(q, k_new, v_new, kb, extra):
        return _moba_sample(q, k_new, v_new, ck, cv, page_table)

    y_s, k_s, v_s, wkv_s, shift_s = _layer(x_sample[:, 0], mod_s, jnp.full((ns,), past_len, jnp.int32), wts, rp,
                                           rwkv_s, moba_s, False)

    heads = lambda z, n, t: z.reshape(1, n, t, N_HEADS, HEAD_DIM)
    heads_t = lambda z: jnp.transpose(z.reshape(N_HEADS, HEAD_DIM, tp), (2, 0, 1))[None, None]
    return (y_p[None], y_s[:, None],
            heads_t(k_p), heads_t(v_p), heads(k_s, ns, 1), heads(v_s, ns, 1),
            wkv_p[None, None], wkv_s[None], shift_p[None], shift_s[None])
```

```python
import functools

import jax
import jax.numpy as jnp
from jax import lax
from jax.experimental import pallas as pl
from jax.experimental.pallas import tpu as pltpu

F32 = jnp.float32
MXU_DT = jnp.bfloat16

D_MODEL = 1024
PAGE_SIZE = 128
N_HEADS = 8
HEAD_DIM = 64
WIDTH = N_HEADS * HEAD_DIM
LORA_W = 64
LORA_A = 64
LORA_G = 128
R_PROJ = 3 * WIDTH + LORA_W + LORA_A + LORA_G
GN_EPS = 64e-5
MOBA_BLOCK = 256
MOBA_TOPK = 3
ROT_DIM = HEAD_DIM // 4
ROPE_THETA = 500000.0
IN_PROJ = R_PROJ + 3 * WIDTH + 2 * D_MODEL
D_FF = 2816
LN_EPS = 1e-5
DN_ALPHA = 2.0 ** 0.25
CHUNK = 64
LANES = 128
VMEM_LIMIT = 56 * 1024 * 1024
SAMPLE_PAGES_PER_STEP = 64
FFN_ROWS = 1024
FFN_CHUNKS = 11
RWKV_GROUP = 4
VT_ROWS = HEAD_DIM + 16

NEG_INF = float("-inf")
LOG2_E = 1.4426950408889634


def _cparams(*sem, vmem=VMEM_LIMIT):
    return pltpu.CompilerParams(dimension_semantics=sem, vmem_limit_bytes=vmem)


def _const_spec(shape):
    return pl.BlockSpec(shape, lambda *_: (0,) * len(shape), pipeline_mode=pl.Buffered(1))


def _row_spec(tm, width):
    return pl.BlockSpec((tm, width), lambda i: (i, 0))


def _mod_spec(arr, tm):
    if arr.shape[0] == 1:
        return pl.BlockSpec((1, arr.shape[1]), lambda i: (0, 0))
    return pl.BlockSpec((tm, arr.shape[1]), lambda i: (i, 0))


def _dot(a, b):
    return jnp.dot(a.astype(MXU_DT), b.astype(MXU_DT), preferred_element_type=F32)


_NN = (((1,), (0,)), ((), ()))
_NT = (((1,), (1,)), ((), ()))


def _dg(a, b, dn):
    return lax.dot_general(a.astype(MXU_DT), b.astype(MXU_DT), dn, preferred_element_type=F32)


def _split(x):
    hi = x.astype(MXU_DT)
    lo = (x - hi.astype(F32)).astype(MXU_DT)
    return hi, lo


def _seg_sum(x, ones, split=True):
    if not split:
        return jnp.dot(x.astype(MXU_DT), ones, preferred_element_type=F32)
    hi, lo = _split(x)
    return (jnp.dot(hi, ones, preferred_element_type=F32)
            + jnp.dot(lo, ones, preferred_element_type=F32))


def _layer_norm(y, g, b):
    mu = jnp.mean(y, axis=-1, keepdims=True)
    yc = y - mu
    var = jnp.mean(yc * yc, axis=-1, keepdims=True)
    return yc * lax.rsqrt(var + LN_EPS) * g + b


def _ada_kernel(c_ref, w_ref, b_ref, o_ref):
    c = c_ref[...]
    o_ref[...] = _dot(c * jax.nn.sigmoid(c), w_ref[...]) + b_ref[...]


def _ada(c, w_ada, b_ada):
    n, d = c.shape
    nout = w_ada.shape[1]
    tn = 1024
    return pl.pallas_call(
        _ada_kernel,
        grid=(nout // tn,),
        in_specs=[pl.BlockSpec((n, d), lambda j: (0, 0)),
                  pl.BlockSpec((d, tn), lambda j: (0, j)),
                  pl.BlockSpec((1, tn), lambda j: (0, j))],
        out_specs=pl.BlockSpec((n, tn), lambda j: (0, j)),
        out_shape=jax.ShapeDtypeStruct((n, nout), F32),
        compiler_params=_cparams("parallel"),
    )(c, w_ada, b_ada)


def _ffn_kernel(x_ref, sh_ref, sc_ref, gt_ref, wg_ref, wu_ref, wd_ref, lg_ref, lb_ref, o_ref, *, n_ff):
    x = x_ref[...]
    h = (x * (1.0 + sc_ref[...]) + sh_ref[...]).astype(MXU_DT)
    step = D_FF // n_ff
    f = None
    for j in range(n_ff):
        cs = slice(j * step, (j + 1) * step)
        g = jnp.dot(h, wg_ref[:, cs], preferred_element_type=F32)
        u = jnp.dot(h, wu_ref[:, cs], preferred_element_type=F32)
        a = (g * jax.nn.sigmoid(g) * u).astype(MXU_DT)
        part = jnp.dot(a, wd_ref[cs, :], preferred_element_type=F32)
        f = part if f is None else f + part
    y = DN_ALPHA * x + (0.5 * gt_ref[...]) * f
    o_ref[...] = _layer_norm(y, lg_ref[...], lb_ref[...])


def _ffn(x, shift, scale, gate, wg, wu, wd, ln_g, ln_b, tm):
    t = x.shape[0]
    return pl.pallas_call(
        functools.partial(_ffn_kernel, n_ff=FFN_CHUNKS),
        grid=(t // tm,),
        in_specs=[_row_spec(tm, D_MODEL), _mod_spec(shift, tm), _mod_spec(scale, tm), _mod_spec(gate, tm),
                  _const_spec((D_MODEL, D_FF)), _const_spec((D_MODEL, D_FF)), _const_spec((D_FF, D_MODEL)),
                  _const_spec((1, D_MODEL)), _const_spec((1, D_MODEL))],
        out_specs=_row_spec(tm, D_MODEL),
        out_shape=jax.ShapeDtypeStruct((t, D_MODEL), F32),
        compiler_params=_cparams("parallel"),
    )(x, shift, scale, gate, wg, wu, wd, ln_g, ln_b)


def _rotate(x, cos, s_up, s_dn):
    return x * cos + pltpu.roll(x, LANES - ROT_DIM // 2, 1) * s_up + pltpu.roll(x, ROT_DIM // 2, 1) * s_dn


def _inproj_kernel(x_ref, sh_ref, sc_ref, w_ref, cos_ref, sup_ref, sdn_ref,
                   pr_ref, q_ref, k_ref, v_ref, gr_ref, gm_ref, kb_ref, vt_ref, km_ref, *, q_scale):
    h = (x_ref[...] * (1.0 + sc_ref[...]) + sh_ref[...]).astype(MXU_DT)
    tm = h.shape[0]

    def proj(lo, width):
        return jnp.dot(h, w_ref[:, lo:lo + width], preferred_element_type=F32)

    pr_ref[...] = proj(0, R_PROJ)
    cos, s_up, s_dn = cos_ref[...], sup_ref[...], sdn_ref[...]
    q = proj(R_PROJ, WIDTH)
    k = proj(R_PROJ + WIDTH, WIDTH)
    k_rot = []
    for j in range(WIDTH // LANES):
        ls = slice(j * LANES, (j + 1) * LANES)
        qj = _rotate(q[:, ls], cos, s_up, s_dn)
        kj = _rotate(k[:, ls], cos, s_up, s_dn)
        q_ref[:, ls] = (qj * q_scale).astype(MXU_DT)
        kb_ref[:, ls] = kj.astype(MXU_DT)
        k_rot.append(kj)
        if km_ref is not None:
            km_ref[0, :, ls] = jnp.sum(kj.reshape(tm // MOBA_BLOCK, MOBA_BLOCK, LANES), axis=1) * (1.0 / MOBA_BLOCK)
    k_rot = jnp.concatenate(k_rot, axis=1)
    v = proj(R_PROJ + 2 * WIDTH, WIDTH)
    if vt_ref is None:
        k_ref[...] = k_rot
        v_ref[...] = v
    else:
        v_t = v.T
        k_ref[...] = k_rot.T
        v_ref[...] = v_t
        for head in range(N_HEADS):
            base = head * VT_ROWS
            vt_ref[base:base + HEAD_DIM, :] = v_t[head * HEAD_DIM:(head + 1) * HEAD_DIM, :].astype(MXU_DT)
            vt_ref[base + HEAD_DIM:base + VT_ROWS, :] = jnp.ones((VT_ROWS - HEAD_DIM, tm), MXU_DT)
    gr_ref[...] = proj(R_PROJ + 3 * WIDTH, D_MODEL)
    gm_ref[...] = proj(R_PROJ + 3 * WIDTH + D_MODEL, D_MODEL)


def _inproj_prompt_kernel(*refs):
    _inproj_kernel(*refs, q_scale=HEAD_DIM ** -0.5 * LOG2_E)


def _inproj_sample_kernel(*refs):
    _inproj_kernel(*refs, None, None, q_scale=HEAD_DIM ** -0.5)


def _inproj(x, shift, scale, w_in, cos, s_up, s_dn, tm, prompt):
    t = x.shape[0]
    out_shape = [jax.ShapeDtypeStruct((t, R_PROJ), F32),
                 jax.ShapeDtypeStruct((t, WIDTH), MXU_DT),
                 jax.ShapeDtypeStruct((t, WIDTH), F32),
                 jax.ShapeDtypeStruct((t, WIDTH), F32),
                 jax.ShapeDtypeStruct((t, D_MODEL), F32),
                 jax.ShapeDtypeStruct((t, D_MODEL), F32),
                 jax.ShapeDtypeStruct((t, WIDTH), MXU_DT)]
    out_specs = [_row_spec(tm, R_PROJ), _row_spec(tm, WIDTH), _row_spec(tm, WIDTH), _row_spec(tm, WIDTH),
                 _row_spec(tm, D_MODEL), _row_spec(tm, D_MODEL), _row_spec(tm, WIDTH)]
    if prompt:
        nblk = tm // MOBA_BLOCK
        vt_rows = N_HEADS * VT_ROWS
        out_shape[2] = out_shape[3] = jax.ShapeDtypeStruct((WIDTH, t), F32)
        out_specs[2] = out_specs[3] = pl.BlockSpec((WIDTH, tm), lambda i: (0, i))
        out_shape += [jax.ShapeDtypeStruct((vt_rows, t), MXU_DT),
                      jax.ShapeDtypeStruct((t // tm, nblk, WIDTH), F32)]
        out_specs += [pl.BlockSpec((vt_rows, tm), lambda i: (0, i)),
                      pl.BlockSpec((1, nblk, WIDTH), lambda i: (i, 0, 0))]
    return pl.pallas_call(
        _inproj_prompt_kernel if prompt else _inproj_sample_kernel,
        grid=(t // tm,),
        in_specs=[_row_spec(tm, D_MODEL), _mod_spec(shift, tm), _mod_spec(scale, tm),
                  _const_spec((D_MODEL, IN_PROJ)),
                  _row_spec(tm, LANES), _row_spec(tm, LANES), _row_spec(tm, LANES)],
        out_specs=out_specs,
        out_shape=out_shape,
        compiler_params=_cparams("parallel"),
    )(x, shift, scale, w_in, cos, s_up, s_dn)


def _rope_tables(pos):
    half = ROT_DIM // 2
    inv = ROPE_THETA ** (-jnp.arange(half, dtype=F32) * 2.0 / ROT_DIM)
    ang = pos.astype(F32)[:, None] * inv[None, :]
    cos, sin = jnp.cos(ang), jnp.sin(ang)
    t = pos.shape[0]
    zeros = lambda n: jnp.zeros((t, n), F32)
    per_head = lambda parts: jnp.tile(jnp.concatenate(parts, axis=1), (1, LANES // HEAD_DIM))
    c = per_head([cos, cos, jnp.ones((t, HEAD_DIM - ROT_DIM), F32)])
    s_up = per_head([-sin, zeros(HEAD_DIM - half)])
    s_dn = per_head([zeros(half), sin, zeros(HEAD_DIM - ROT_DIM)])
    return c, s_up, s_dn


def _rwkv_pointwise(pr, prev, mu, wcat, w0, a0, g2, k_k, k_a, ones):
    xs = pr + (prev - pr) * mu
    r = xs[:, 0:WIDTH]
    k = xs[:, WIDTH:2 * WIDTH]
    v = xs[:, 2 * WIDTH:3 * WIDTH]
    la = xs[:, 3 * WIDTH:3 * WIDTH + LORA_W + LORA_A]
    gl = xs[:, 3 * WIDTH + LORA_W + LORA_A:R_PROJ]
    lane = lax.broadcasted_iota(jnp.int32, la.shape, 1)
    la = jnp.where(lane < LORA_W, jnp.tanh(la), la)
    wa = _dot(la, wcat)
    w = w0 + wa[:, :WIDTH]
    a = jax.nn.sigmoid(a0 + wa[:, WIDTH:])
    g = _dot(jax.nn.sigmoid(gl), g2)
    z = -w
    softplus = jnp.maximum(z, 0.0) + jnp.log1p(jnp.exp(-jnp.abs(z)))
    log_decay = -jnp.exp(-softplus - 0.5)
    kk = k * k_k
    kk = kk / jnp.maximum(jnp.sqrt(_seg_sum(kk * kk, ones)), 1e-12)
    k_eff = k * (1.0 + (a - 1.0) * k_a)
    return r, k_eff, v, kk, kk * a, log_decay, g


def _rwkv_post(y, r, k_eff, v, g, r_k, lnx_g, lnx_b, ones):
    inv = 1.0 / HEAD_DIM
    yc = y - _seg_sum(y, ones, split=False) * inv
    yv = _seg_sum(yc * yc, ones, split=False) * inv
    yn = yc * lax.rsqrt(yv + GN_EPS) * lnx_g + lnx_b
    bonus = _seg_sum(r * k_eff * r_k, ones, split=False) * v
    return (yn + bonus) * g


def _rwkv_prompt_kernel(pr_ref, shift0_ref, s0_ref, mu_ref, wcat_ref, w0_ref, a0_ref, g2_ref, kk_ref, ka_ref,
                        rk_ref, lg_ref, lb_ref, ones_ref, tri_ref,
                        out_ref, sfin_ref, shift_ref,
                        prev_scr, st_scr, rt_scr, kp_scr, bt_scr, kt_scr, v_scr, bc_scr, kc_scr, pc_scr, y_scr):
    i = pl.program_id(0)

    @pl.when(i == 0)
    def _():
        prev_scr[...] = shift0_ref[...]
        st_scr[...] = s0_ref[...]

    pr = pr_ref[...]
    tm = pr.shape[0]
    row = lax.broadcasted_iota(jnp.int32, (tm, 1), 0)
    prev = jnp.where(row == 0, prev_scr[...], pltpu.roll(pr, 1, 0))
    ones = ones_ref[...]
    r, k_eff, v, kk, b, lw, g = _rwkv_pointwise(pr, prev, mu_ref[...], wcat_ref[...], w0_ref[...], a0_ref[...],
                                                g2_ref[...], kk_ref[...], ka_ref[...], ones)
    l_hi, l_lo = _split(lw)
    tri = tri_ref[...]
    L = jnp.dot(tri, l_hi, preferred_element_type=F32) + jnp.dot(tri, l_lo, preferred_element_type=F32)
    n_chunks = tm // CHUNK
    LC = jnp.broadcast_to(L.reshape(n_chunks, CHUNK, WIDTH)[:, CHUNK - 1:CHUNK, :],
                          (n_chunks, CHUNK, WIDTH)).reshape(tm, WIDTH)
    e_inv = jnp.exp(-L)
    e_end = jnp.exp(LC - L)
    rt_scr[...] = r * jnp.exp(L)
    kp_scr[...] = kk * jnp.exp(L - lw)
    bt_scr[...] = b * e_inv
    kt_scr[...] = k_eff * e_inv
    bc_scr[...] = b * e_end
    kc_scr[...] = k_eff * e_end
    pc_scr[...] = jnp.exp(LC)
    v_scr[...] = v

    ri = lax.broadcasted_iota(jnp.int32, (CHUNK, CHUNK), 0)
    ci = lax.broadcasted_iota(jnp.int32, (CHUNK, CHUNK), 1)
    strict, incl, eye = ri > ci, ri >= ci, ri == ci

    each = lambda fn, *lists: [fn(*args) for args in zip(*lists)]

    def solve(chunks):
        probs = [(c, h) for c in chunks for h in range(N_HEADS)]

        def blocks(ref):
            return [ref[c * CHUNK:(c + 1) * CHUNK, h * HEAD_DIM:(h + 1) * HEAD_DIM] for c, h in probs]

        def blocks_t(ref):
            per_chunk = {c: ref[c * CHUNK:(c + 1) * CHUNK, :].T for c in chunks}
            return [per_chunk[c][h * HEAD_DIM:(h + 1) * HEAD_DIM, :] for c, h in probs]

        kp, rt, v_b = blocks(kp_scr), blocks(rt_scr), blocks(v_scr)
        bk = each(lambda a, b_: jnp.concatenate([a, b_], axis=0), blocks(bt_scr), blocks(kt_scr))
        bk = each(lambda b_: b_.astype(MXU_DT), bk)
        v_m = each(lambda a: a.astype(MXU_DT), v_b)
        akk = each(lambda a, b_: _dg(a, b_, _NT), kp, bk)
        arr = each(lambda a, b_: _dg(a, b_, _NT), rt, bk)
        a_zb = [jnp.where(strict, a[:, :CHUNK], 0.0) for a in akk]
        a_vk = [jnp.where(strict, a[:, CHUNK:], 0.0) for a in akk]
        a_rb = [jnp.where(incl, a[:, :CHUNK], 0.0) for a in arr]
        a_rk = [jnp.where(incl, a[:, CHUNK:], 0.0) for a in arr]
        u0 = each(lambda a, w, v_: jnp.concatenate([a, _dg(w, v_, _NN)], axis=1), kp, a_vk, v_m)
        half = lambda s: (ri // (2 * s) == ci // (2 * s)) & (ri % (2 * s) >= s) & (ci % (2 * s) < s)
        t_inv = [jnp.where(eye, 1.0, 0.0) - jnp.where(half(1), a, 0.0) for a in a_zb]
        s_blk = 2
        while s_blk < CHUNK:
            lower = half(s_blk)
            t_inv = each(lambda t_, a: t_ - _dg(_dg(t_, jnp.where(lower, a, 0.0), _NN), t_, _NN), t_inv, a_zb)
            s_blk *= 2
        u = each(lambda t_, u_: _dg(t_, u_, _NN), t_inv, u0)
        u_m = each(lambda a: a.astype(MXU_DT), u)
        gmat = each(lambda a, u_: _dg(a, u_, _NN), a_rb, u_m)
        r_eff = each(lambda a, g_: a - g_[:, :HEAD_DIM], rt, gmat)
        y0 = each(lambda a, b_, g_: _dg(a, b_, _NN) - g_[:, HEAD_DIM:], a_rk, v_m, gmat)
        bu = each(lambda a, u_: _dg(a, u_, _NN), blocks_t(bc_scr), u_m)
        pc = [pc_scr[c * CHUNK:c * CHUNK + 1, h * HEAD_DIM:(h + 1) * HEAD_DIM] for c, h in probs]
        m_t = each(lambda p_, b_: jnp.where(eye, p_, 0.0) - b_[:, :HEAD_DIM], pc, bu)
        n_t = each(lambda a, v_, c_: _dg(a, v_, _NN) - c_[:, HEAD_DIM:], blocks_t(kc_scr), v_m, bu)
        return r_eff, y0, m_t, n_t

    parts = [solve(range(c0, c0 + RWKV_GROUP)) for c0 in range(0, n_chunks, RWKV_GROUP)]
    r_eff, y0, m_t, n_t = ([x for part in parts for x in part[i]] for i in range(4))
    st = [st_scr[h] for h in range(N_HEADS)]
    for c in range(n_chunks):
        sl = slice(c * N_HEADS, (c + 1) * N_HEADS)
        st_m = each(lambda a: a.astype(MXU_DT), st)
        y_scr[c * CHUNK:(c + 1) * CHUNK, :] = jnp.concatenate(
            each(lambda a, s_, b_: _dg(a, s_, _NN) + b_, r_eff[sl], st_m, y0[sl]), axis=1)
        st = each(lambda m_, s_, n_: _dg(m_, s_, _NN) + n_, m_t[sl], st_m, n_t[sl])
    for h in range(N_HEADS):
        st_scr[h] = st[h]
    out_ref[...] = _rwkv_post(y_scr[...], r, k_eff, v, g, rk_ref[...], lg_ref[...], lb_ref[...], ones)
    prev_scr[...] = pr[tm - 1:tm, :]
    shift_ref[...] = pr[tm - 1:tm, :]

    @pl.when(i == pl.num_programs(0) - 1)
    def _():
        for h in range(N_HEADS):
            sfin_ref[h] = st[h].T


def _rwkv_prompt(pr, shift0, s0, rp, tm):
    t = pr.shape[0]
    ch = lax.broadcasted_iota(jnp.int32, (tm, tm), 0) // CHUNK == lax.broadcasted_iota(jnp.int32, (tm, tm), 1) // CHUNK
    low = lax.broadcasted_iota(jnp.int32, (tm, tm), 0) >= lax.broadcasted_iota(jnp.int32, (tm, tm), 1)
    tri = (ch & low).astype(MXU_DT)
    vec = _const_spec((1, WIDTH))
    tile = pltpu.VMEM((tm, WIDTH), F32)
    return pl.pallas_call(
        _rwkv_prompt_kernel,
        grid=(t // tm,),
        in_specs=[_row_spec(tm, R_PROJ), _const_spec((1, R_PROJ)), _const_spec((N_HEADS, HEAD_DIM, HEAD_DIM)),
                  _const_spec((1, R_PROJ)), _const_spec((LORA_W + LORA_A, 2 * WIDTH)), vec, vec,
                  _const_spec((LORA_G, WIDTH)), vec, vec, vec, vec, vec,
                  _const_spec((WIDTH, WIDTH)), _const_spec((tm, tm))],
        out_specs=[_row_spec(tm, WIDTH),
                   pl.BlockSpec((N_HEADS, HEAD_DIM, HEAD_DIM), lambda i: (0, 0, 0)),
                   pl.BlockSpec((1, R_PROJ), lambda i: (0, 0))],
        out_shape=[jax.ShapeDtypeStruct((t, WIDTH), F32),
                   jax.ShapeDtypeStruct((N_HEADS, HEAD_DIM, HEAD_DIM), F32),
                   jax.ShapeDtypeStruct((1, R_PROJ), F32)],
        scratch_shapes=[pltpu.VMEM((1, R_PROJ), F32), pltpu.VMEM((N_HEADS, HEAD_DIM, HEAD_DIM), F32)] + [tile] * 9,
        compiler_params=_cparams("arbitrary"),
    )(pr, shift0, jnp.swapaxes(s0, 1, 2), rp["mu"], rp["wcat"], rp["w0"], rp["a0"], rp["g2"], rp["k_k"], rp["k_a"],
      rp["r_k"], rp["lnx_g"], rp["lnx_b"], rp["ones"], tri)


def _rwkv_step_kernel(pr_ref, shift_ref, s_ref, mu_ref, wcat_ref, w0_ref, a0_ref, g2_ref, kk_ref, ka_ref,
                      rk_ref, lg_ref, lb_ref, ones_ref, out_ref, snew_ref):
    ones = ones_ref[...]
    r, k_eff, v, kk, b, lw, g = _rwkv_pointwise(pr_ref[...], shift_ref[...], mu_ref[...], wcat_ref[...],
                                                w0_ref[...], a0_ref[...], g2_ref[...], kk_ref[...], ka_ref[...], ones)
    decay = jnp.exp(lw)
    eye = (lax.broadcasted_iota(jnp.int32, (HEAD_DIM, HEAD_DIM), 0)
           == lax.broadcasted_iota(jnp.int32, (HEAD_DIM, HEAD_DIM), 1))[None]
    ys = []
    for h in range(N_HEADS):
        over_v = lambda z: z[:, None, h * HEAD_DIM:(h + 1) * HEAD_DIM]
        s = s_ref[:, h]
        s_kk = jnp.sum(s * over_v(kk), axis=2, keepdims=True)
        v_col = jnp.sum(jnp.where(eye, over_v(v), 0.0), axis=2, keepdims=True)
        s_new = s * over_v(decay) - s_kk * over_v(b) + v_col * over_v(k_eff)
        y_col = jnp.sum(s_new * over_v(r), axis=2, keepdims=True)
        ys.append(jnp.sum(jnp.where(eye, y_col, 0.0), axis=1))
        snew_ref[:, h] = s_new
    y = jnp.concatenate(ys, axis=1)
    out_ref[...] = _rwkv_post(y, r, k_eff, v, g, rk_ref[...], lg_ref[...], lb_ref[...], ones)


def _rwkv_step(pr, shift, state, rp, nb):
    n = pr.shape[0]
    vec = _const_spec((1, WIDTH))
    state_spec = pl.BlockSpec((nb, N_HEADS, HEAD_DIM, HEAD_DIM), lambda i: (i, 0, 0, 0))
    return pl.pallas_call(
        _rwkv_step_kernel,
        grid=(n // nb,),
        in_specs=[_row_spec(nb, R_PROJ), _row_spec(nb, R_PROJ), state_spec,
                  _const_spec((1, R_PROJ)), _const_spec((LORA_W + LORA_A, 2 * WIDTH)), vec, vec,
                  _const_spec((LORA_G, WIDTH)), vec, vec, vec, vec, vec, _const_spec((WIDTH, WIDTH))],
        out_specs=[_row_spec(nb, WIDTH), state_spec],
        out_shape=[jax.ShapeDtypeStruct((n, WIDTH), F32), jax.ShapeDtypeStruct(state.shape, F32)],
        compiler_params=_cparams("parallel"),
    )(pr, shift, state, rp["mu"], rp["wcat"], rp["w0"], rp["a0"], rp["g2"], rp["k_k"], rp["k_a"], rp["r_k"],
      rp["lnx_g"], rp["lnx_b"], rp["ones"])


def _moba_prompt_kernel(q_ref, km_ref, k_ref, vt_ref, o_ref, qm_scr, sel_scr, m_scr, acc_scr,
                        s0_scr, s1_scr, s2_scr, s3_scr, cm_scr):
    c = pl.program_id(0)
    tq = q_ref.shape[0]
    nblk = km_ref.shape[0]
    lane = lax.broadcasted_iota(jnp.int32, (1, LANES), 1)
    blk_id = lax.broadcasted_iota(jnp.int32, (nblk, tq), 0)
    key_i = lax.broadcasted_iota(jnp.int32, (MOBA_BLOCK, tq), 0)
    qry_i = lax.broadcasted_iota(jnp.int32, (MOBA_BLOCK, tq), 1)

    def head_lanes(h):
        return (lane < HEAD_DIM) if h % 2 == 0 else (lane >= HEAD_DIM)

    def values(h, b):
        rows = slice(h * VT_ROWS, (h + 1) * VT_ROWS)
        return vt_ref[rows, pl.ds(pl.multiple_of(b * MOBA_BLOCK, MOBA_BLOCK), MOBA_BLOCK)]

    def keys(h, b):
        ps = slice((h // 2) * LANES, (h // 2 + 1) * LANES)
        return k_ref[pl.ds(pl.multiple_of(b * MOBA_BLOCK, MOBA_BLOCK), MOBA_BLOCK), ps]

    all_heads = range(N_HEADS)
    pair_lanes = lambda h: slice((h // 2) * LANES, (h // 2 + 1) * LANES)
    for h in all_heads:
        qm_scr[h] = jnp.where(head_lanes(h), q_ref[:, pair_lanes(h)].astype(F32), 0.0).astype(MXU_DT)
    gates = [jnp.where(blk_id < c, lax.dot_general(km_ref[:, pair_lanes(h)], qm_scr[h], _NT,
                                                   preferred_element_type=F32), NEG_INF)
             for h in all_heads]
    sels = [jnp.zeros(gates[0].shape, jnp.bool_) for _ in all_heads]
    for _ in range(MOBA_TOPK):
        tops = [jnp.max(g_, axis=0, keepdims=True) for g_ in gates]
        hits = [blk_id == jnp.min(jnp.where(g_ == t_, blk_id, nblk), axis=0, keepdims=True)
                for g_, t_ in zip(gates, tops)]
        sels = [s_ | h_ for s_, h_ in zip(sels, hits)]
        gates = [jnp.where(h_, NEG_INF, g_) for g_, h_ in zip(gates, hits)]
    for h in all_heads:
        sel_scr[h] = sels[h].astype(F32)
    for h in all_heads:
        s = lax.dot_general(keys(h, c), qm_scr[h], _NT, preferred_element_type=F32)
        s = jnp.where(key_i <= qry_i, s, NEG_INF)
        s0_scr[h] = s
        m_scr[h] = jnp.max(s, axis=0, keepdims=True)
    for h in all_heads:
        p = jnp.exp2(s0_scr[h] - m_scr[h]).astype(MXU_DT)
        acc_scr[h] = jnp.dot(values(h, c), p, preferred_element_type=F32)

    def scores(b, s_scr, slot):
        for h in range(N_HEADS):
            s = lax.dot_general(keys(h, b), qm_scr[h], _NT, preferred_element_type=F32)
            s_scr[h] = s
            cm_scr[slot, h] = jnp.max(s, axis=0, keepdims=True)

    def absorb(b, s_scr, slot):
        for h in range(N_HEADS):
            picked = sel_scr[h, pl.ds(b, 1), :] > 0.0
            m_old = m_scr[h]
            m_new = jnp.where(picked, jnp.maximum(m_old, cm_scr[slot, h]), m_old)
            p = jnp.exp2(s_scr[h] - jnp.where(picked, m_new, jnp.inf)).astype(MXU_DT)
            acc_scr[h] = acc_scr[h] * jnp.exp2(m_old - m_new) + jnp.dot(values(h, b), p, preferred_element_type=F32)
            m_scr[h] = m_new

    bufs = (s0_scr, s1_scr, s2_scr, s3_scr)

    def block_group(i, carry):
        for j, buf in enumerate(bufs):
            scores(len(bufs) * i + j, buf, j)
        for j, buf in enumerate(bufs):
            absorb(len(bufs) * i + j, buf, j)
        return carry

    def single_block(b, carry):
        scores(b, s0_scr, 0)
        absorb(b, s0_scr, 0)
        return carry

    n_groups = c // len(bufs)
    lax.fori_loop(0, n_groups, block_group, 0)
    lax.fori_loop(n_groups * len(bufs), c, single_block, 0)

    for pair in range(N_HEADS // 2):
        outs = []
        for h in (2 * pair, 2 * pair + 1):
            a = acc_scr[h]
            outs.append((a[:HEAD_DIM] / a[HEAD_DIM:HEAD_DIM + 1]).T)
        o_ref[:, pair * LANES:(pair + 1) * LANES] = jnp.concatenate(outs, axis=1)


def _moba_prompt(q, kmean, kb, vt):
    t = q.shape[0]
    nblk = kmean.shape[0]
    tq = MOBA_BLOCK
    return pl.pallas_call(
        _moba_prompt_kernel,
        grid=(t // tq,),
        in_specs=[_row_spec(tq, WIDTH), _const_spec((nblk, WIDTH)), _const_spec((t, WIDTH)),
                  _const_spec(vt.shape)],
        out_specs=_row_spec(tq, WIDTH),
        out_shape=jax.ShapeDtypeStruct((t, WIDTH), F32),
        scratch_shapes=[pltpu.VMEM((N_HEADS, tq, LANES), MXU_DT), pltpu.VMEM((N_HEADS, nblk, tq), F32),
                        pltpu.VMEM((N_HEADS, 1, tq), F32), pltpu.VMEM((N_HEADS, VT_ROWS, tq), F32),
                        ] + [pltpu.VMEM((N_HEADS, MOBA_BLOCK, tq), F32)] * 4 + [
                        pltpu.VMEM((4, N_HEADS, 1, tq), F32)],
        compiler_params=_cparams("parallel"),
    )(q, kmean, kb, vt)


def _moba_sample_score_kernel(pt_ref, q_ref, knew_ref, *refs):
    del pt_ref
    pg = SAMPLE_PAGES_PER_STEP
    pages = refs[:pg]
    p_ref, pself_ref, idx_ref, sc_scr, gate_scr = refs[pg:]
    j = pl.program_id(1)
    nblk = gate_scr.shape[1]
    rnd = lambda z: z.astype(MXU_DT).astype(F32)
    q = q_ref[0]
    for t in range(pg // 2):
        halves = (pages[2 * t][0], pages[2 * t + 1][0])
        sc = jnp.concatenate([jnp.sum(page * q, axis=1, keepdims=True) for page in halves], axis=2)
        gate = jnp.broadcast_to(jnp.sum(sc, axis=2, keepdims=True), (N_HEADS, 1, LANES))
        blk = pl.ds(j * (pg // 2) + t, 1)
        for h in range(N_HEADS):
            sc_scr[h, blk, :] = sc[h]
            gate_scr[h, blk, :] = gate[h]

    @pl.when(j == pl.num_programs(1) - 1)
    def _():
        gate = gate_scr[...]
        blk_id = lax.broadcasted_iota(jnp.int32, gate.shape, 1)
        sel = jnp.zeros(gate.shape, jnp.bool_)
        for r in range(MOBA_TOPK):
            top = jnp.max(gate, axis=1, keepdims=True)
            idx = jnp.min(jnp.where(gate == top, blk_id, nblk), axis=1, keepdims=True)
            hit = blk_id == idx
            sel = sel | hit
            gate = jnp.where(hit, NEG_INF, gate)
            idx_ref[0, r] = idx
        sel_f = jnp.where(sel, 1.0, 0.0)
        picked = jnp.concatenate([sel_f] * (MOBA_BLOCK // LANES), axis=2) > 0.0
        s = jnp.where(picked, sc_scr[...], NEG_INF)
        s_self = jnp.sum(q * rnd(knew_ref[0]), axis=1, keepdims=True)
        m = jnp.maximum(jnp.max(jnp.max(s, axis=2, keepdims=True), axis=1, keepdims=True), s_self)
        p = jnp.exp(s - m[:, :, :1])
        p_self = jnp.exp(s_self - m)
        inv = 1.0 / (jnp.sum(jnp.sum(p, axis=2, keepdims=True), axis=1, keepdims=True) + p_self)
        p_ref[0] = p * inv[:, :, :1]
        pself_ref[0] = p_self * inv


def _moba_sample_scores(q_lanes, knew_lanes, cache_kt, page_table):
    n, n_pages = page_table.shape
    pg = SAMPLE_PAGES_PER_STEP
    nblk = n_pages * PAGE_SIZE // MOBA_BLOCK
    page_block = (1, N_HEADS, HEAD_DIM, PAGE_SIZE)

    def page_spec(e):
        return pl.BlockSpec(page_block, lambda i, j, pt: (pt[i * n_pages + j * pg + e], 0, 0, 0))

    per_seq = pl.BlockSpec(page_block, lambda i, j, pt: (i, 0, 0, 0))
    grid_spec = pltpu.PrefetchScalarGridSpec(
        num_scalar_prefetch=1,
        grid=(n, n_pages // pg),
        in_specs=[per_seq, per_seq] + [page_spec(e) for e in range(pg)],
        out_specs=[pl.BlockSpec((1, N_HEADS, nblk, MOBA_BLOCK), lambda i, j, pt: (i, 0, 0, 0)),
                   pl.BlockSpec((1, N_HEADS, 1, LANES), lambda i, j, pt: (i, 0, 0, 0)),
                   pl.BlockSpec((1, MOBA_TOPK, N_HEADS, 1, LANES), lambda i, j, pt: (i, 0, 0, 0, 0))],
        scratch_shapes=[pltpu.VMEM((N_HEADS, nblk, MOBA_BLOCK), F32), pltpu.VMEM((N_HEADS, nblk, LANES), F32)],
    )
    return pl.pallas_call(
        _moba_sample_score_kernel,
        grid_spec=grid_spec,
        out_shape=[jax.ShapeDtypeStruct((n, N_HEADS, nblk, MOBA_BLOCK), F32),
                   jax.ShapeDtypeStruct((n, N_HEADS, 1, LANES), F32),
                   jax.ShapeDtypeStruct((n, MOBA_TOPK, N_HEADS, 1, LANES), jnp.int32)],
        compiler_params=_cparams("parallel", "arbitrary"),
    )(page_table.reshape(-1), q_lanes, knew_lanes, *([cache_kt] * pg))


def _moba_sample_value_kernel(phys_ref, blk_ref, p_ref, pself_ref, vnew_ref, *refs):
    del phys_ref
    pages_per_block = MOBA_BLOCK // PAGE_SIZE
    v_refs, o_ref = refs[:-1], refs[-1]
    i = pl.program_id(0)
    rnd = lambda z: z.astype(MXU_DT).astype(F32)
    for h in range(N_HEADS):
        acc = rnd(pself_ref[0, h][:, :HEAD_DIM]) * rnd(vnew_ref[0, h])
        for r in range(MOBA_TOPK):
            slot = h * MOBA_TOPK + r
            p_row = p_ref[0, h, pl.ds(blk_ref[i * N_HEADS * MOBA_TOPK + slot], 1), :]
            for e in range(pages_per_block):
                p_rows = jnp.broadcast_to(p_row[:, e * PAGE_SIZE:(e + 1) * PAGE_SIZE], (N_HEADS, PAGE_SIZE))
                v_t = v_refs[slot * pages_per_block + e][0, 0]
                acc = acc + _dg(p_rows, v_t, _NT)[0:1]
        o_ref[0, h] = acc


def _moba_sample_values(p, p_self, v_new, cache_vt, phys, blocks):
    n, _, nblk, _ = p.shape
    pages_per_block = MOBA_BLOCK // PAGE_SIZE
    per_seq = N_HEADS * MOBA_TOPK * pages_per_block

    def page_spec(h, e):
        return pl.BlockSpec((1, 1, HEAD_DIM, PAGE_SIZE),
                            lambda i, ph, bk: (ph[i * per_seq + h * MOBA_TOPK * pages_per_block + e], h, 0, 0))

    whole = lambda shape: pl.BlockSpec((1,) + shape, lambda i, ph, bk: (i,) + (0,) * len(shape))
    grid_spec = pltpu.PrefetchScalarGridSpec(
        num_scalar_prefetch=2,
        grid=(n,),
        in_specs=[whole((N_HEADS, nblk, MOBA_BLOCK)), whole((N_HEADS, 1, LANES)), whole((N_HEADS, 1, HEAD_DIM))]
        + [page_spec(h, e) for h in range(N_HEADS) for e in range(MOBA_TOPK * pages_per_block)],
        out_specs=whole((N_HEADS, 1, HEAD_DIM)),
    )
    return pl.pallas_call(
        _moba_sample_value_kernel,
        grid_spec=grid_spec,
        out_shape=jax.ShapeDtypeStruct((n, N_HEADS, 1, HEAD_DIM), F32),
        compiler_params=_cparams("parallel"),
    )(phys, blocks, p, p_self, v_new.reshape(n, N_HEADS, 1, HEAD_DIM), *([cache_vt] * per_seq))


def _moba_sample(q, k_new, v_new, cache_kt, cache_vt, page_table):
    n = q.shape[0]
    lanes = lambda z: jnp.broadcast_to(z.astype(F32).reshape(n, N_HEADS, HEAD_DIM, 1),
                                       (n, N_HEADS, HEAD_DIM, PAGE_SIZE))
    p, p_self, idx = _moba_sample_scores(lanes(q), lanes(k_new), cache_kt, page_table)
    pages_per_block = MOBA_BLOCK // PAGE_SIZE
    blocks = jnp.transpose(idx[..., 0, 0], (0, 2, 1))
    logi = (blocks[..., None] * pages_per_block + jnp.arange(pages_per_block, dtype=jnp.int32)).reshape(
        n, N_HEADS, -1)
    phys = jnp.take_along_axis(page_table[:, None, :], logi, axis=2)
    out = _moba_sample_values(p, p_self, v_new, cache_vt, phys.reshape(-1), blocks.reshape(-1))
    return out.reshape(n, WIDTH)


def _merge_kernel(x_ref, r_ref, a_ref, gr_ref, gm_ref, gt_ref, wr_ref, wm_ref, wo_ref, lg_ref, lb_ref, o_ref):
    br = jax.nn.sigmoid(gr_ref[...]) * _dot(r_ref[...], wr_ref[...])
    bm = jax.nn.sigmoid(gm_ref[...]) * _dot(a_ref[...], wm_ref[...])
    merged = _dot(br + bm, wo_ref[...])
    o_ref[...] = _layer_norm(DN_ALPHA * x_ref[...] + gt_ref[...] * merged, lg_ref[...], lb_ref[...])


def _merge(x, r_out, att, g_r, g_m, gate, wr, wm, wo, ln_g, ln_b, tm):
    t = x.shape[0]
    return pl.pallas_call(
        _merge_kernel,
        grid=(t // tm,),
        in_specs=[_row_spec(tm, D_MODEL), _row_spec(tm, WIDTH), _row_spec(tm, WIDTH), _row_spec(tm, D_MODEL),
                  _row_spec(tm, D_MODEL), _mod_spec(gate, tm),
                  _const_spec((WIDTH, D_MODEL)), _const_spec((WIDTH, D_MODEL)), _const_spec((D_MODEL, D_MODEL)),
                  _const_spec((1, D_MODEL)), _const_spec((1, D_MODEL))],
        out_specs=_row_spec(tm, D_MODEL),
        out_shape=jax.ShapeDtypeStruct((t, D_MODEL), F32),
        compiler_params=_cparams("parallel"),
    )(x, r_out, att, g_r, g_m, gate, wr, wm, wo, ln_g, ln_b)


def _tile(t, want):
    return want if t % want == 0 else t


def _layer(x, mod, pos, wts, rp, rwkv_fn, moba_fn, prompt):
    t = x.shape[0]
    tm = _tile(t, 512)
    row = lambda z: z.reshape(1, -1)
    x1 = _ffn(x, mod[0], mod[1], mod[2], wts["ffn1_gate"], wts["ffn1_up"], wts["ffn1_down"],
              row(wts["ln1_g"]), row(wts["ln1_b"]), _tile(t, FFN_ROWS))
    cos, s_up, s_dn = _rope_tables(pos)
    proj = _inproj(x1, mod[3], mod[4], wts["w_in"], cos, s_up, s_dn, tm, prompt)
    pr, q, k_new, v_new, g_r, g_m, kb = proj[:7]
    r_out, wkv_new, shift_new = rwkv_fn(pr)
    att = moba_fn(q, k_new, v_new, kb, proj[7:])
    x2 = _merge(x1, r_out, att, g_r, g_m, mod[5], wts["w_br_rwkv"], wts["w_br_moba"], wts["w_out"],
                row(wts["ln2_g"]), row(wts["ln2_b"]), _tile(t, FFN_ROWS))
    y = _ffn(x2, mod[6], mod[7], mod[8], wts["ffn2_gate"], wts["ffn2_up"], wts["ffn2_down"],
             row(wts["ln3_g"]), row(wts["ln3_b"]), _tile(t, FFN_ROWS))
    return y, k_new, v_new, wkv_new, shift_new


def kernel(x_prompt, x_sample, cache_k, cache_v, state_rwkv_wkv, state_rwkv_shift, page_table, c_prompt, c_sample, w_ada, b_ada, ffn1_gate, ffn1_up, ffn1_down, ln1_g, ln1_b, w_in, mu_shift, rwkv_w0, rwkv_w2, rwkv_a0, rwkv_a2, rwkv_g2, rwkv_k_k, rwkv_k_a, rwkv_r_k, rwkv_lnx_g, rwkv_lnx_b, w_br_rwkv, w_br_moba, w_out, ln2_g, ln2_b, ffn2_gate, ffn2_up, ffn2_down, ln3_g, ln3_b):
    assert x_prompt.shape[0] == 1 and x_sample.shape[1] == 1 and w_ada.shape[0] == 1
    tp = x_prompt.shape[1]
    ns = x_sample.shape[0]
    n_pages = page_table.shape[1]
    past_len = n_pages * PAGE_SIZE
    assert tp % MOBA_BLOCK == 0 and past_len % MOBA_BLOCK == 0 and n_pages % SAMPLE_PAGES_PER_STEP == 0

    bf = lambda z: z[0].astype(MXU_DT)
    wts = {"ffn1_gate": bf(ffn1_gate), "ffn1_up": bf(ffn1_up), "ffn1_down": bf(ffn1_down),
           "ffn2_gate": bf(ffn2_gate), "ffn2_up": bf(ffn2_up), "ffn2_down": bf(ffn2_down),
           "w_in": bf(w_in), "w_br_rwkv": bf(w_br_rwkv), "w_br_moba": bf(w_br_moba), "w_out": bf(w_out),
           "ln1_g": ln1_g[0], "ln1_b": ln1_b[0], "ln2_g": ln2_g[0], "ln2_b": ln2_b[0],
           "ln3_g": ln3_g[0], "ln3_b": ln3_b[0]}
    zw = jnp.zeros((LORA_W, WIDTH), MXU_DT)
    head_of = jnp.arange(WIDTH) // HEAD_DIM
    row = lambda z: z.reshape(1, -1)
    rp = {"mu": row(mu_shift[0]),
          "wcat": jnp.concatenate([jnp.concatenate([bf(rwkv_w2), zw], axis=1),
                                   jnp.concatenate([zw, bf(rwkv_a2)], axis=1)], axis=0),
          "w0": row(rwkv_w0[0]), "a0": row(rwkv_a0[0]), "g2": bf(rwkv_g2),
          "k_k": row(rwkv_k_k[0]), "k_a": row(rwkv_k_a[0]), "r_k": row(rwkv_r_k[0]),
          "lnx_g": row(rwkv_lnx_g[0]), "lnx_b": row(rwkv_lnx_b[0]),
          "ones": (head_of[:, None] == head_of[None, :]).astype(MXU_DT)}

    n_mod = ((1 + ns + 7) // 8) * 8
    c_all = jnp.concatenate([c_prompt, c_sample, jnp.zeros((n_mod - 1 - ns, D_MODEL), F32)], axis=0)
    ada = _ada(c_all, w_ada[0], b_ada)
    mod_p = [ada[0:1, j * D_MODEL:(j + 1) * D_MODEL] for j in range(9)]
    mod_s = [ada[1:1 + ns, j * D_MODEL:(j + 1) * D_MODEL] for j in range(9)]

    def rwkv_p(pr):
        r_out, s_fin, shift = _rwkv_prompt(pr, jnp.zeros((1, R_PROJ), F32),
                                           jnp.zeros((N_HEADS, HEAD_DIM, HEAD_DIM), F32), rp, _tile(tp, 256))
        return r_out, s_fin, shift

    def moba_p(q, k_new, v_new, kb, extra):
        vt, kmean = extra
        return _moba_prompt(q, kmean.reshape(-1, WIDTH).astype(MXU_DT), kb, vt)

    y_p, k_p, v_p, wkv_p, shift_p = _layer(x_prompt[0], mod_p, jnp.arange(tp, dtype=jnp.int32), wts, rp,
                                           rwkv_p, moba_p, True)

    ck, cv = jnp.transpose(cache_k[0], (0, 2, 3, 1)), jnp.transpose(cache_v[0], (0, 2, 3, 1))

    def rwkv_s(pr):
        r_out, s_new = _rwkv_step(pr, state_rwkv_shift[0], state_rwkv_wkv[0], rp, _tile(ns, 8))
        return r_out, s_new, pr

    def moba_s(q, k_new, v_new, kb, extra):
        return _moba_sample(q, k_new, v_new, ck, cv, page_table)

    y_s, k_s, v_s, wkv_s, shift_s = _layer(x_sample[:, 0], mod_s, jnp.full((ns,), past_len, jnp.int32), wts, rp,
                                           rwkv_s, moba_s, False)

    heads = lambda z, n, t: z.reshape(1, n, t, N_HEADS, HEAD_DIM)
    heads_t = lambda z: jnp.transpose(z.reshape(N_HEADS, HEAD_DIM, tp), (2, 0, 1))[None, None]
    return (y_p[None], y_s[:, None],
            heads_t(k_p), heads_t(v_p), heads(k_s, ns, 1), heads(v_s, ns, 1),
            wkv_p[None, None], wkv_s[None], shift_p[None], shift_s[None])
```
